```python
import numpy as np
import jax
import jax.numpy as jnp
from jax import lax

D_MODEL = 2048
BATCH = 2
SEQ = 4096
DEPTH = 4
DEC_BATCH = 8
DEC_SEQ = 8
PAST_LEN = 16384
PAGE_SIZE = 128

N_MEM = 256
HEAD_DIM = 128
ATT_W = D_MODEL // 2
N_ATT_HEADS = ATT_W // HEAD_DIM
DIL_PATTERNS = ((128, 1), (512, 4), (2048, 16))
WIN = max(w for w, _ in DIL_PATTERNS)
N_BUCKETS = 32
MAX_DIST = WIN
CONV_CH = D_MODEL // 4
CONV_K = 31
X_W = D_MODEL // 4
N_X_HEADS = X_W // HEAD_DIM
MIX_W = ATT_W + CONV_CH + X_W
IN_SIZES = (ATT_W, ATT_W, ATT_W, ATT_W, CONV_CH, CONV_CH, CONV_CH, X_W, X_W)
IN_W = sum(IN_SIZES)
SPLIT_AT = tuple(int(s) for s in np.cumsum(IN_SIZES)[:-1])
Q_BLOCK = 128
EPS = 1e-6
NEG = -1e30
SCALE = HEAD_DIM ** -0.5

kernel_name = "hybrid_dilated_conv_memory_decoder_step"


def t5_bucket(dist):
    dist = np.asarray(dist)
    max_exact = N_BUCKETS // 2
    large = max_exact + (np.log(np.maximum(dist, 1) / max_exact)
                         / np.log(MAX_DIST / max_exact) * (N_BUCKETS - max_exact)).astype(np.int32)
    large = np.minimum(large, N_BUCKETS - 1)
    return np.where(dist < max_exact, dist, large).astype(np.int32)


def rms_norm(x, g):
    xf = x.astype(jnp.float32)
    y = xf * lax.rsqrt(jnp.mean(xf * xf, axis=-1, keepdims=True) + EPS)
    return (y * g.astype(jnp.float32)).astype(x.dtype)


def layer_norm(x, g, b):
    xf = x.astype(jnp.float32)
    mu = jnp.mean(xf, axis=-1, keepdims=True)
    var = jnp.mean(jnp.square(xf - mu), axis=-1, keepdims=True)
    y = (xf - mu) * lax.rsqrt(var + EPS) * g.astype(jnp.float32) + b.astype(jnp.float32)
    return y.astype(x.dtype)


def heads(t, n):
    return t.reshape(t.shape[:-1] + (n, HEAD_DIM))


def dilated_attention(q, k_src, v_src, base, lo, rel_bias):
    outs, lses = [], []
    for w, d in DIL_PATTERNS:
        offs = np.arange(w // d + 1) * d
        bias = rel_bias[t5_bucket(offs)].T.astype(jnp.float32)
        idx = base[:, None] - offs[None, :]
        valid = idx >= lo
        idx = jnp.maximum(idx, 0)
        kg = jnp.take(k_src, idx, axis=1)
        vg = jnp.take(v_src, idx, axis=1)
        logits = jnp.einsum('bqhd,bqjhd->bqhj', q, kg,
                            preferred_element_type=jnp.float32) * SCALE + bias
        logits = jnp.where(valid[None, :, None, :], logits, NEG)
        m = jnp.max(logits, axis=-1, keepdims=True)
        p = jnp.exp(logits - m)
        s = jnp.sum(p, axis=-1, keepdims=True)
        outs.append(jnp.einsum('bqhj,bqjhd->bqhd', p, vg.astype(jnp.float32)) / s)
        lses.append((m + jnp.log(s))[..., 0])
    wts = jax.nn.softmax(jnp.stack(lses, axis=0), axis=0)
    o = sum(wts[i][..., None] * outs[i] for i in range(len(DIL_PATTERNS)))
    return o.astype(q.dtype)


def dilated_attention_prompt(q, k, v, rel_bias):
    B, S, H, Dh = q.shape
    pad = jnp.zeros((B, WIN, H, Dh), k.dtype)
    kp = jnp.concatenate([pad, k], axis=1)
    vp = jnp.concatenate([pad, v], axis=1)
    base = WIN + jnp.arange(Q_BLOCK)

    def block(b):
        q0 = b * Q_BLOCK
        qb = lax.dynamic_slice_in_dim(q, q0, Q_BLOCK, axis=1)
        ks = lax.dynamic_slice_in_dim(kp, q0, WIN + Q_BLOCK, axis=1)
        vs = lax.dynamic_slice_in_dim(vp, q0, WIN + Q_BLOCK, axis=1)
        return dilated_attention(qb, ks, vs, base, WIN - q0, rel_bias)

    out = lax.map(block, jnp.arange(S // Q_BLOCK))
    return out.transpose(1, 0, 2, 3, 4).reshape(B, S, H, Dh)


def dilated_attention_sample(q, k_new, v_new, k_buf, v_buf, rel_bias):
    L = k_buf.shape[1]
    k_src = jnp.concatenate([k_buf, k_new], axis=1)
    v_src = jnp.concatenate([v_buf, v_new], axis=1)
    base = L + jnp.arange(q.shape[1])
    o = dilated_attention(q, k_src, v_src, base, 0, rel_bias)
    return o, k_src[:, -L:], v_src[:, -L:]


def causal_dwconv(u_ext, w, b):
    y = lax.conv_general_dilated(u_ext, w[:, None, :].astype(u_ext.dtype), window_strides=(1,),
                                 padding='VALID', dimension_numbers=('NWC', 'WIO', 'NWC'),
                                 feature_group_count=u_ext.shape[-1])
    return y + b


def conv_tail(c, g, b, w_pw2):
    return jax.nn.silu(layer_norm(c, g, b)) @ w_pw2


def cross_attention(q, mk, mv):
    logits = jnp.einsum('bthd,bmhd->bhtm', q, mk, preferred_element_type=jnp.float32) * SCALE
    p = jax.nn.softmax(logits, axis=-1)
    o = jnp.einsum('bhtm,bmhd->bthd', p, mv.astype(jnp.float32))
    return o.astype(q.dtype)


def mix_out(x, a, c, m, gate_att, gate_conv, gate_mem, w_out, g_post):
    y = jnp.concatenate([a * jax.nn.silu(gate_att), c * jax.nn.silu(gate_conv),
                         m * jax.nn.silu(gate_mem)], axis=-1) @ w_out
    return x + rms_norm(y, g_post)


def setup_inputs(seed: int = 0) -> dict:
    key = jax.random.key(seed)
    ks = jax.random.split(key, 20)
    n = jax.random.normal
    l_buf = min(WIN, PAST_LEN)
    return {
        'x_prompt': n(ks[0], (BATCH, SEQ, D_MODEL), jnp.float32),
        'x_sample': n(ks[1], (DEC_BATCH, DEC_SEQ, D_MODEL), jnp.float32),
        'mem_prompt': n(ks[2], (BATCH, N_MEM, D_MODEL), jnp.float32),
        'cache_attn_k': n(ks[3], (DEPTH, DEC_BATCH, l_buf, N_ATT_HEADS, HEAD_DIM), jnp.float32),
        'cache_attn_v': n(ks[4], (DEPTH, DEC_BATCH, l_buf, N_ATT_HEADS, HEAD_DIM), jnp.float32),
        'state_conv': 0.5 * n(ks[5], (DEPTH, DEC_BATCH, CONV_K - 1, CONV_CH), jnp.float32),
        'cache_mem_k': n(ks[6], (DEPTH, DEC_BATCH, N_MEM, N_X_HEADS, HEAD_DIM), jnp.float32),
        'cache_mem_v': n(ks[7], (DEPTH, DEC_BATCH, N_MEM, N_X_HEADS, HEAD_DIM), jnp.float32),
        'rel_bias': 0.1 * n(ks[8], (N_BUCKETS, N_ATT_HEADS), jnp.float32),
        'norm_pre_g': 1.0 + 0.02 * n(ks[9], (DEPTH, D_MODEL), jnp.float32),
        'w_in': n(ks[10], (DEPTH, D_MODEL, IN_W), jnp.float32) * D_MODEL ** -0.5,
        'w_dw': n(ks[11], (DEPTH, CONV_K, CONV_CH), jnp.float32) * CONV_K ** -0.5,
        'b_dw': 0.01 * n(ks[12], (DEPTH, CONV_CH), jnp.float32),
        'ln_conv_g': 1.0 + 0.02 * n(ks[13], (DEPTH, CONV_CH), jnp.float32),
        'ln_conv_b': 0.02 * n(ks[14], (DEPTH, CONV_CH), jnp.float32),
        'w_pw2': n(ks[15], (DEPTH, CONV_CH, CONV_CH), jnp.float32) * CONV_CH ** -0.5,
        'w_mem_kv': n(ks[16], (DEPTH, D_MODEL, 2 * X_W), jnp.float32) * D_MODEL ** -0.5,
        'w_out': n(ks[17], (DEPTH, MIX_W, D_MODEL), jnp.float32) * MIX_W ** -0.5,
        'norm_post_g': 1.0 + 0.02 * n(ks[18], (DEPTH, D_MODEL), jnp.float32),
    }


def reference(x_prompt, x_sample, mem_prompt, cache_attn_k, cache_attn_v, state_conv,
              cache_mem_k, cache_mem_v, rel_bias, norm_pre_g, w_in, w_dw, b_dw,
              ln_conv_g, ln_conv_b, w_pw2, w_mem_kv, w_out, norm_post_g):
    xp, xs = x_prompt, x_sample
    bp, s_len, _ = xp.shape
    bs, t_len, _ = xs.shape
    l_prompt = min(WIN, s_len)
    akp, avp, cvp, mkp, mvp, aks, avs, cvs = [], [], [], [], [], [], [], []
    for li in range(DEPTH):
        qa, ka, va, ga, uv, ug, gc, qm, gm = jnp.split(
            rms_norm(xp, norm_pre_g[li]) @ w_in[li], SPLIT_AT, axis=-1)
        qa, ka, va = heads(qa, N_ATT_HEADS), heads(ka, N_ATT_HEADS), heads(va, N_ATT_HEADS)
        a = dilated_attention_prompt(qa, ka, va, rel_bias).reshape(bp, s_len, ATT_W)
        u = uv * jax.nn.sigmoid(ug)
        u_ext = jnp.concatenate([jnp.zeros((bp, CONV_K - 1, CONV_CH), u.dtype), u], axis=1)
        c = conv_tail(causal_dwconv(u_ext, w_dw[li], b_dw[li]), ln_conv_g[li], ln_conv_b[li], w_pw2[li])
        mk, mv = jnp.split(mem_prompt @ w_mem_kv[li], 2, axis=-1)
        mk, mv = heads(mk, N_X_HEADS), heads(mv, N_X_HEADS)
        m = cross_attention(heads(qm, N_X_HEADS), mk, mv).reshape(bp, s_len, X_W)
        akp.append(ka[:, -l_prompt:])
        avp.append(va[:, -l_prompt:])
        cvp.append(u_ext[:, -(CONV_K - 1):])
        mkp.append(mk)
        mvp.append(mv)
        xp = mix_out(xp, a, c, m, ga, gc, gm, w_out[li], norm_post_g[li])

        qa, ka, va, ga, uv, ug, gc, qm, gm = jnp.split(
            rms_norm(xs, norm_pre_g[li]) @ w_in[li], SPLIT_AT, axis=-1)
        qa, ka, va = heads(qa, N_ATT_HEADS), heads(ka, N_ATT_HEADS), heads(va, N_ATT_HEADS)
        a, k_buf, v_buf = dilated_attention_sample(qa, ka, va, cache_attn_k[li], cache_attn_v[li], rel_bias)
        a = a.reshape(bs, t_len, ATT_W)
        u = uv * jax.nn.sigmoid(ug)
        u_ext = jnp.concatenate([state_conv[li].astype(u.dtype), u], axis=1)
        c = conv_tail(causal_dwconv(u_ext, w_dw[li], b_dw[li]), ln_conv_g[li], ln_conv_b[li], w_pw2[li])
        m = cross_attention(heads(qm, N_X_HEADS), cache_mem_k[li], cache_mem_v[li]).reshape(bs, t_len, X_W)
        aks.append(k_buf)
        avs.append(v_buf)
        cvs.append(u_ext[:, -(CONV_K - 1):])
        xs = mix_out(xs, a, c, m, ga, gc, gm, w_out[li], norm_post_g[li])

    return (xp, xs, jnp.stack(akp), jnp.stack(avp), jnp.stack(cvp), jnp.stack(mkp), jnp.stack(mvp),
            jnp.stack(aks), jnp.stack(avs), jnp.stack(cvs))
```

```python
import functools

import numpy as np
import jax
import jax.numpy as jnp
from jax import lax
from jax.experimental import pallas as pl
from jax.experimental.pallas import tpu as pltpu

D_MODEL = 2048
DEPTH = 4
N_MEM = 256
HEAD_DIM = 128
ATT_W = D_MODEL // 2
N_ATT_HEADS = ATT_W // HEAD_DIM
DIL_PATTERNS = ((128, 1), (512, 4), (2048, 16))
WIN = max(w for w, _ in DIL_PATTERNS)
N_BUCKETS = 32
MAX_DIST = WIN
CONV_CH = D_MODEL // 4
CONV_K = 31
X_W = D_MODEL // 4
N_X_HEADS = X_W // HEAD_DIM
MIX_W = ATT_W + CONV_CH + X_W
IN_W = 4 * ATT_W + 3 * CONV_CH + 2 * X_W
EPS = 1e-6
NEG = -1e30
SCALE = HEAD_DIM ** -0.5

COL_BLK = 512
CB_Q, CB_K, CB_V, CB_GA = 0, 2, 4, 6
CB_UV, CB_UG, CB_GC, CB_QM, CB_GM = 8, 9, 10, 11, 12

Q_BLK = 128
N_KBLK = WIN // Q_BLK + 1
HALO = 32
VMEM_LIMIT = 56 * 1024 * 1024

BF16 = jnp.bfloat16
F32 = jnp.float32


def _t5_bucket(dist):
    dist = np.asarray(dist)
    max_exact = N_BUCKETS // 2
    large = max_exact + (np.log(np.maximum(dist, 1) / max_exact)
                         / np.log(MAX_DIST / max_exact) * (N_BUCKETS - max_exact)).astype(np.int32)
    large = np.minimum(large, N_BUCKETS - 1)
    return np.where(dist < max_exact, dist, large).astype(np.int32)


def _distance_table(rel_bias):
    d = np.arange(WIN + 1)
    cnt = np.zeros(WIN + 1, np.int64)
    for w, dil in DIL_PATTERNS:
        cnt += ((d % dil == 0) & (d <= w)).astype(np.int64)
    logc = jnp.log(jnp.asarray(np.maximum(cnt, 1), F32))
    tab = rel_bias[_t5_bucket(d)].T.astype(F32) + logc[None, :]
    tab = jnp.where(jnp.asarray(cnt > 0)[None, :], tab, NEG)
    return jnp.concatenate([tab, jnp.full((tab.shape[0], 1), NEG, F32)], axis=1)


def _prompt_bias_tiles(tab):
    delta = np.arange(N_KBLK)[:, None, None]
    r = np.arange(Q_BLK)[None, :, None]
    c = np.arange(Q_BLK)[None, None, :]
    dist = delta * Q_BLK + r - c
    idx = np.where((dist >= 0) & (dist <= WIN), dist, WIN + 1)
    return jnp.take(tab, jnp.asarray(idx, jnp.int32), axis=1)


def _sample_bias(tab, t_len, l_buf, n_keys):
    t = np.arange(t_len)[:, None]
    idx = np.arange(n_keys)[None, :]
    dist = l_buf + t - idx
    sel = np.where((dist >= 0) & (dist <= WIN), dist, WIN + 1)
    out = jnp.take(tab, jnp.asarray(sel, jnp.int32), axis=1)
    return out.reshape(tab.shape[0] * t_len, n_keys)


def _proj_kernel(x_ref, g_ref, w_ref, o_ref, h_ref, *, normalize):
    @pl.when(pl.program_id(1) == 0)
    def _():
        x = x_ref[...]
        if normalize:
            x = x * lax.rsqrt(jnp.mean(x * x, axis=-1, keepdims=True) + EPS) * g_ref[...]
        h_ref[...] = x.astype(BF16)

    o_ref[...] = jnp.dot(h_ref[...], w_ref[...], preferred_element_type=F32)


def _proj(x, g, w, *, normalize, bm, bn):
    m, d = x.shape
    n = w.shape[1]
    return pl.pallas_call(
        functools.partial(_proj_kernel, normalize=normalize),
        grid=(m // bm, n // bn),
        in_specs=[
            pl.BlockSpec((bm, d), lambda i, j: (i, 0)),
            pl.BlockSpec((1, d), lambda i, j: (0, 0)),
            pl.BlockSpec((d, bn), lambda i, j: (0, j)),
        ],
        out_specs=pl.BlockSpec((bm, bn), lambda i, j: (i, j)),
        out_shape=jax.ShapeDtypeStruct((m, n), F32),
        scratch_shapes=[pltpu.VMEM((bm, d), BF16)],
        compiler_params=pltpu.CompilerParams(
            dimension_semantics=("parallel", "arbitrary"), vmem_limit_bytes=VMEM_LIMIT),
        name="proj",
    )(x, g.reshape(1, d), w)


def _prompt_attn_kernel(q_ref, k_ref, v_ref, bias_ref, o_ref, kb_ref, vb_ref):
    i = pl.program_id(2)

    @pl.when(i == 0)
    def _():
        kb_ref[...] = k_ref[...].astype(BF16)
        vb_ref[...] = v_ref[...].astype(BF16)

    q = (q_ref[...] * SCALE).astype(BF16)

    def body(j, carry):
        m, l, acc = carry
        k0 = pl.multiple_of(j * Q_BLK, Q_BLK)
        s = lax.dot_general(q, kb_ref[pl.ds(k0, Q_BLK), :], (((1,), (1,)), ((), ())),
                            preferred_element_type=F32)
        s = s + bias_ref[i - j]
        m_new = jnp.maximum(m, jnp.max(s, axis=-1, keepdims=True))
        alpha = jnp.exp(m - m_new)
        p = jnp.exp(s - m_new)
        l = alpha * l + jnp.sum(p, axis=-1, keepdims=True)
        acc = alpha * acc + jnp.dot(p.astype(BF16), vb_ref[pl.ds(k0, Q_BLK), :],
                                    preferred_element_type=F32)
        return m_new, l, acc

    init = (jnp.full((Q_BLK, 1), NEG, F32), jnp.zeros((Q_BLK, 1), F32),
            jnp.zeros((Q_BLK, HEAD_DIM), F32))
    _, l, acc = lax.fori_loop(jnp.maximum(i - (N_KBLK - 1), 0), i + 1, body, init)
    o_ref[...] = acc / l


def _prompt_attn(z, bias_tiles):
    b, s, _ = z.shape
    hpb = COL_BLK // HEAD_DIM
    return pl.pallas_call(
        _prompt_attn_kernel,
        grid=(b, N_ATT_HEADS, s // Q_BLK),
        in_specs=[
            pl.BlockSpec((None, Q_BLK, HEAD_DIM), lambda bi, h, i: (bi, i, CB_Q * hpb + h)),
            pl.BlockSpec((None, s, HEAD_DIM), lambda bi, h, i: (bi, 0, CB_K * hpb + h)),
            pl.BlockSpec((None, s, HEAD_DIM), lambda bi, h, i: (bi, 0, CB_V * hpb + h)),
            pl.BlockSpec((None, N_KBLK, Q_BLK, Q_BLK), lambda bi, h, i: (h, 0, 0, 0)),
        ],
        out_specs=pl.BlockSpec((None, Q_BLK, HEAD_DIM), lambda bi, h, i: (bi, i, h)),
        out_shape=jax.ShapeDtypeStruct((b, s, ATT_W), F32),
        scratch_shapes=[pltpu.VMEM((s, HEAD_DIM), BF16), pltpu.VMEM((s, HEAD_DIM), BF16)],
        compiler_params=pltpu.CompilerParams(
            dimension_semantics=("parallel", "parallel", "arbitrary"), vmem_limit_bytes=VMEM_LIMIT),
        name="prompt_attn",
    )(z, z, z, bias_tiles)


def _sample_attn_kernel(q_ref, kn_ref, vn_ref, kc_ref, vc_ref, bias_ref,
                        o_ref, ko_ref, vo_ref, ks_ref, vs_ref, *, t_len, l_buf, n_keys, n_heads):
    pad = n_keys - l_buf - t_len
    width = n_heads * HEAD_DIM
    for src_new, cache, src, out in ((kn_ref, kc_ref, ks_ref, ko_ref), (vn_ref, vc_ref, vs_ref, vo_ref)):
        new = src_new[...]
        src[0:l_buf, :] = cache[...].astype(BF16)
        src[l_buf:n_keys, :] = jnp.concatenate(
            [new, jnp.zeros((pad, width), F32)], axis=0).astype(BF16)
        out[0:l_buf - t_len, :] = cache[t_len:l_buf, :]
        out[l_buf - t_len:l_buf, :] = new

    rows = n_heads * t_len
    row_head = lax.broadcasted_iota(jnp.int32, (rows, width), 0) // t_len
    col_head = lax.broadcasted_iota(jnp.int32, (rows, width), 1) // HEAD_DIM
    diag = row_head == col_head
    q = q_ref[...] * SCALE
    qbd = jnp.where(diag, jnp.concatenate([q] * n_heads, axis=0), 0.0).astype(BF16)
    s = lax.dot_general(qbd, ks_ref[...], (((1,), (1,)), ((), ())), preferred_element_type=F32)
    s = s + bias_ref[...]
    m = jnp.max(s, axis=-1, keepdims=True)
    p = jnp.exp(s - m)
    l = jnp.sum(p, axis=-1, keepdims=True)
    o = jnp.dot(p.astype(BF16), vs_ref[...], preferred_element_type=F32) / l
    o = jnp.where(diag, o, 0.0)
    acc = o[0:t_len, :]
    for h in range(1, n_heads):
        acc = acc + o[h * t_len:(h + 1) * t_len, :]
    o_ref[...] = acc


def _sample_attn(z, cache_k, cache_v, bias, n_keys):
    b, t_len, _ = z.shape
    l_buf = cache_k.shape[1]
    n_heads = COL_BLK // HEAD_DIM
    n_grp = ATT_W // COL_BLK
    rows = n_heads * t_len
    cache_spec = pl.BlockSpec((None, l_buf, COL_BLK), lambda bi, g: (bi, 0, g))
    return pl.pallas_call(
        functools.partial(_sample_attn_kernel, t_len=t_len, l_buf=l_buf, n_keys=n_keys, n_heads=n_heads),
        grid=(b, n_grp),
        in_specs=[
            pl.BlockSpec((None, t_len, COL_BLK), lambda bi, g: (bi, 0, CB_Q + g)),
            pl.BlockSpec((None, t_len, COL_BLK), lambda bi, g: (bi, 0, CB_K + g)),
            pl.BlockSpec((None, t_len, COL_BLK), lambda bi, g: (bi, 0, CB_V + g)),
            cache_spec, cache_spec,
            pl.BlockSpec((rows, n_keys), lambda bi, g: (g, 0)),
        ],
        out_specs=[
            pl.BlockSpec((None, t_len, COL_BLK), lambda bi, g: (bi, 0, g)),
            cache_spec, cache_spec,
        ],
        out_shape=[
            jax.ShapeDtypeStruct((b, t_len, ATT_W), F32),
            jax.ShapeDtypeStruct(cache_k.shape, F32),
            jax.ShapeDtypeStruct(cache_v.shape, F32),
        ],
        scratch_shapes=[pltpu.VMEM((n_keys, COL_BLK), BF16), pltpu.VMEM((n_keys, COL_BLK), BF16)],
        compiler_params=pltpu.CompilerParams(
            dimension_semantics=("parallel", "parallel"), vmem_limit_bytes=VMEM_LIMIT),
        name="sample_attn",
    )(z, z, z, cache_k, cache_v, bias)


def _silu(x):
    return x * jax.nn.sigmoid(x)


def _mix_kernel(x_ref, a_ref, ga_ref, uv_ref, ug_ref, gc_ref, qm_ref, gm_ref, cinit_ref, mk_ref, mv_ref,
                wdw_ref, bdw_ref, lng_ref, lnb_ref, wpw_ref, wout_ref, gpost_ref,
                xo_ref, cs_ref, uext_ref, mix_ref, *, tile, conv_chunk):
    @pl.when(pl.program_id(1) == 0)
    def _():
        uext_ref[0:HALO, :] = cinit_ref[...]

    uext_ref[HALO:HALO + tile, :] = uv_ref[...] * jax.nn.sigmoid(ug_ref[...])

    first = HALO - (CONV_K - 1)
    chunks = []
    for c0 in range(0, tile, conv_chunk):
        acc = jnp.broadcast_to(bdw_ref[...], (conv_chunk, CONV_CH))
        for k in range(CONV_K):
            acc = acc + uext_ref[c0 + first + k:c0 + first + k + conv_chunk, :] * wdw_ref[k:k + 1, :]
        chunks.append(acc)
    c = jnp.concatenate(chunks, axis=0) if len(chunks) > 1 else chunks[0]

    cs_ref[...] = uext_ref[tile + first:tile + HALO, :]
    tail = uext_ref[tile:tile + HALO, :]
    uext_ref[0:HALO, :] = tail

    mu = jnp.mean(c, axis=-1, keepdims=True)
    var = jnp.mean(jnp.square(c - mu), axis=-1, keepdims=True)
    c = (c - mu) * lax.rsqrt(var + EPS) * lng_ref[...] + lnb_ref[...]
    c = jnp.dot(_silu(c).astype(BF16), wpw_ref[...], preferred_element_type=F32)

    mix_ref[:, 0:ATT_W] = (a_ref[...] * _silu(ga_ref[...])).astype(BF16)
    mix_ref[:, ATT_W:ATT_W + CONV_CH] = (c * _silu(gc_ref[...])).astype(BF16)

    for h in range(N_X_HEADS):
        cols = slice(h * HEAD_DIM, (h + 1) * HEAD_DIM)
        q = (qm_ref[:, cols] * SCALE).astype(BF16)
        s = lax.dot_general(q, mk_ref[:, cols].astype(BF16), (((1,), (1,)), ((), ())),
                            preferred_element_type=F32)
        p = jnp.exp(s - jnp.max(s, axis=-1, keepdims=True))
        o = jnp.dot(p.astype(BF16), mv_ref[:, cols].astype(BF16), preferred_element_type=F32)
        o = o / jnp.sum(p, axis=-1, keepdims=True)
        off = ATT_W + CONV_CH + h * HEAD_DIM
        mix_ref[:, off:off + HEAD_DIM] = (o * _silu(gm_ref[:, cols])).astype(BF16)

    y = jnp.dot(mix_ref[...], wout_ref[...], preferred_element_type=F32)
    y = y * lax.rsqrt(jnp.mean(y * y, axis=-1, keepdims=True) + EPS) * gpost_ref[...]
    xo_ref[...] = x_ref[...] + y


def _mix(x, z, a, cinit, mk_arr, mk_blk, mv_arr, mv_blk, w_dw, b_dw, ln_g, ln_b, w_pw2, w_out, g_post,
         *, tile):
    b, s, _ = x.shape
    conv_chunk = min(tile, 32)

    def zspec(cb, width=COL_BLK):
        return pl.BlockSpec((None, tile, width), lambda bi, t: (bi, t, cb * COL_BLK // width))

    def const(shape):
        return pl.BlockSpec(shape, lambda bi, t: (0,) * len(shape))

    return pl.pallas_call(
        functools.partial(_mix_kernel, tile=tile, conv_chunk=conv_chunk),
        grid=(b, s // tile),
        in_specs=[
            pl.BlockSpec((None, tile, D_MODEL), lambda bi, t: (bi, t, 0)),
            pl.BlockSpec((None, tile, ATT_W), lambda bi, t: (bi, t, 0)),
            zspec(CB_GA, ATT_W), zspec(CB_UV), zspec(CB_UG), zspec(CB_GC), zspec(CB_QM), zspec(CB_GM),
            pl.BlockSpec((None, HALO, CONV_CH), lambda bi, t: (bi, 0, 0)),
            pl.BlockSpec((None, N_MEM, X_W), lambda bi, t: (bi, 0, mk_blk)),
            pl.BlockSpec((None, N_MEM, X_W), lambda bi, t: (bi, 0, mv_blk)),
            const((CONV_K, CONV_CH)), const((1, CONV_CH)), const((1, CONV_CH)), const((1, CONV_CH)),
            const((CONV_CH, CONV_CH)), const((MIX_W, D_MODEL)), const((1, D_MODEL)),
        ],
        out_specs=[
            pl.BlockSpec((None, tile, D_MODEL), lambda bi, t: (bi, t, 0)),
            pl.BlockSpec((None, CONV_K - 1, CONV_CH), lambda bi, t: (bi, 0, 0)),
        ],
        out_shape=[
            jax.ShapeDtypeStruct((b, s, D_MODEL), F32),
            jax.ShapeDtypeStruct((b, CONV_K - 1, CONV_CH), F32),
        ],
        scratch_shapes=[pltpu.VMEM((HALO + tile, CONV_CH), F32), pltpu.VMEM((tile, MIX_W), BF16)],
        compiler_params=pltpu.CompilerParams(
            dimension_semantics=("parallel", "arbitrary"), vmem_limit_bytes=VMEM_LIMIT),
        name="mix",
    )(x, a, z, z, z, z, z, z, cinit, mk_arr, mv_arr,
      w_dw, b_dw.reshape(1, -1), ln_g.reshape(1, -1), ln_b.reshape(1, -1), w_pw2, w_out,
      g_post.reshape(1, -1))


def kernel(x_prompt, x_sample, mem_prompt, cache_attn_k, cache_attn_v, state_conv, cache_mem_k, cache_mem_v,
           rel_bias, norm_pre_g, w_in, w_dw, b_dw, ln_conv_g, ln_conv_b, w_pw2, w_mem_kv, w_out, norm_post_g):
    bp, s_len, _ = x_prompt.shape
    bs, t_len, _ = x_sample.shape
    l_buf = cache_attn_k.shape[2]
    l_prompt = min(WIN, s_len)
    n_keys_s = -(-(l_buf + t_len) // Q_BLK) * Q_BLK

    w_in_b, w_pw2_b = w_in.astype(BF16), w_pw2.astype(BF16)
    w_mem_b, w_out_b = w_mem_kv.astype(BF16), w_out.astype(BF16)

    tab = _distance_table(rel_bias)
    bias_p = _prompt_bias_tiles(tab)
    bias_s = _sample_bias(tab, t_len, l_buf, n_keys_s)

    mem2d = mem_prompt.reshape(bp * N_MEM, D_MODEL)
    ones_g = jnp.ones((D_MODEL,), F32)
    cinit_p = jnp.zeros((bp, HALO, CONV_CH), F32)
    pad_s = jnp.zeros((DEPTH, bs, HALO - (CONV_K - 1), CONV_CH), F32)
    cinit_s = jnp.concatenate([pad_s, state_conv], axis=2)
    ck = cache_attn_k.reshape(DEPTH, bs, l_buf, ATT_W)
    cv = cache_attn_v.reshape(DEPTH, bs, l_buf, ATT_W)
    cmk = cache_mem_k.reshape(DEPTH, bs, N_MEM, X_W)
    cmv = cache_mem_v.reshape(DEPTH, bs, N_MEM, X_W)

    xp, xs = x_prompt, x_sample
    akp, avp, cvp, mkp, mvp, aks, avs, cvs = [], [], [], [], [], [], [], []
    for li in range(DEPTH):
        wargs = (w_dw[li], b_dw[li], ln_conv_g[li], ln_conv_b[li], w_pw2_b[li], w_out_b[li], norm_post_g[li])

        zp = _proj(xp.reshape(bp * s_len, D_MODEL), norm_pre_g[li], w_in_b[li],
                   normalize=True, bm=1024, bn=COL_BLK).reshape(bp, s_len, IN_W)
        a = _prompt_attn(zp, bias_p)
        mkv = _proj(mem2d, ones_g, w_mem_b[li], normalize=False, bm=bp * N_MEM, bn=COL_BLK)
        mkv = mkv.reshape(bp, N_MEM, 2 * X_W)
        xp, cst = _mix(xp, zp, a, cinit_p, mkv, 0, mkv, 1, *wargs, tile=256)
        akp.append(zp[:, s_len - l_prompt:, ATT_W:2 * ATT_W].reshape(bp, l_prompt, N_ATT_HEADS, HEAD_DIM))
        avp.append(zp[:, s_len - l_prompt:, 2 * ATT_W:3 * ATT_W].reshape(bp, l_prompt, N_ATT_HEADS, HEAD_DIM))
        cvp.append(cst)
        mkp.append(mkv[:, :, :X_W].reshape(bp, N_MEM, N_X_HEADS, HEAD_DIM))
        mvp.append(mkv[:, :, X_W:].reshape(bp, N_MEM, N_X_HEADS, HEAD_DIM))

        zs = _proj(xs.reshape(bs * t_len, D_MODEL), norm_pre_g[li], w_in_b[li],
                   normalize=True, bm=bs * t_len, bn=COL_BLK).reshape(bs, t_len, IN_W)
        a, ko, vo = _sample_attn(zs, ck[li], cv[li], bias_s, n_keys_s)
        xs, cst = _mix(xs, zs, a, cinit_s[li], cmk[li], 0, cmv[li], 0, *wargs, tile=t_len)
        aks.append(ko.reshape(bs, l_buf, N_ATT_HEADS, HEAD_DIM))
        avs.append(vo.reshape(bs, l_buf, N_ATT_HEADS, HEAD_DIM))
        cvs.append(cst)

    return (xp, xs, jnp.stack(akp), jnp.stack(avp), jnp.stack(cvp), jnp.stack(mkp), jnp.stack(mvp),
            jnp.stack(aks), jnp.stack(avs), jnp.stack(cvs))
```

```python
import functools

import numpy as np
import jax
import jax.numpy as jnp
from jax import lax
from jax.experimental import pallas as pl
from jax.experimental.pallas import tpu as pltpu

D_MODEL = 2048
DEPTH = 4
N_MEM = 256
HEAD_DIM = 128
ATT_W = D_MODEL // 2
N_ATT_HEADS = ATT_W // HEAD_DIM
DIL_PATTERNS = ((128, 1), (512, 4), (2048, 16))
WIN = max(w for w, _ in DIL_PATTERNS)
N_BUCKETS = 32
MAX_DIST = WIN
CONV_CH = D_MODEL // 4
CONV_K = 31
X_W = D_MODEL // 4
N_X_HEADS = X_W // HEAD_DIM
MIX_W = ATT_W + CONV_CH + X_W
IN_W = 4 * ATT_W + 3 * CONV_CH + 2 * X_W
EPS = 1e-6
NEG = -1e30
SCALE = HEAD_DIM ** -0.5

COL_BLK = 512
CB_Q, CB_K, CB_V, CB_GA = 0, 2, 4, 6
CB_UV, CB_UG, CB_GC, CB_QM, CB_GM = 8, 9, 10, 11, 12

Q_BLK = 128
FAR_W, FAR_D = DIL_PATTERNS[-1]
NEAR_PATTERNS = DIL_PATTERNS[:-1]
NEAR_REACH = max(w for w, _ in NEAR_PATTERNS)
NEAR_KBLK = NEAR_REACH // Q_BLK + 1
FAR_KEYS = FAR_W // FAR_D
assert FAR_KEYS == Q_BLK and NEAR_REACH % Q_BLK == 0
FAR_GROUP = 2
NEAR_GROUP = 4
HALO = 32
VMEM_LIMIT = 56 * 1024 * 1024

BF16 = jnp.bfloat16
F32 = jnp.float32


def _t5_bucket(dist):
    dist = np.asarray(dist)
    max_exact = N_BUCKETS // 2
    large = max_exact + (np.log(np.maximum(dist, 1) / max_exact)
                         / np.log(MAX_DIST / max_exact) * (N_BUCKETS - max_exact)).astype(np.int32)
    large = np.minimum(large, N_BUCKETS - 1)
    return np.where(dist < max_exact, dist, large).astype(np.int32)


def _distance_table(rel_bias, dists, patterns):
    dists = np.asarray(dists)
    cnt = np.zeros(dists.shape, np.int64)
    for w, dil in patterns:
        cnt += ((dists % dil == 0) & (dists <= w)).astype(np.int64)
    logc = jnp.log(jnp.asarray(np.maximum(cnt, 1), F32))
    tab = rel_bias[_t5_bucket(dists)].T.astype(F32) + logc[None, :]
    tab = jnp.where(jnp.asarray(cnt > 0)[None, :], tab, NEG)
    return jnp.concatenate([tab, jnp.full((tab.shape[0], 1), NEG, F32)], axis=1)


def _toeplitz(tab, off, rows, cols):
    n_valid = tab.shape[1] - 1
    lp = rows + cols
    j = np.concatenate([np.arange(cols), np.full(lp - (rows + cols - 1), off + n_valid),
                        np.arange(-(rows - 1), 0)])
    d = off - j
    idx = np.where((d >= 0) & (d < n_valid), d, n_valid)
    v = jnp.take(tab, jnp.asarray(idx, jnp.int32), axis=1)
    flat = jnp.tile(v, (1, rows))[:, :rows * (lp - 1)]
    return flat.reshape(tab.shape[0], rows, lp - 1)[:, :, :cols]


def _sample_bias(tab, t_len, l_buf, n_keys):
    t = np.arange(t_len)[:, None]
    idx = np.arange(n_keys)[None, :]
    dist = l_buf + t - idx
    sel = np.where((dist >= 0) & (dist <= WIN), dist, WIN + 1)
    out = jnp.take(tab, jnp.asarray(sel, jnp.int32), axis=1)
    return out.reshape(tab.shape[0] * t_len, n_keys)


def _proj_kernel(x_ref, g_ref, w_ref, o_ref, h_ref, *, normalize):
    @pl.when(pl.program_id(1) == 0)
    def _():
        x = x_ref[...]
        if normalize:
            x = x * lax.rsqrt(jnp.mean(x * x, axis=-1, keepdims=True) + EPS) * g_ref[...]
        h_ref[...] = x.astype(BF16)

    o_ref[...] = jnp.dot(h_ref[...], w_ref[...], preferred_element_type=F32)


def _proj(x, g, w, *, normalize, bm, bn):
    m, d = x.shape
    n = w.shape[1]
    return pl.pallas_call(
        functools.partial(_proj_kernel, normalize=normalize),
        grid=(m // bm, n // bn),
        in_specs=[
            pl.BlockSpec((bm, d), lambda i, j: (i, 0)),
            pl.BlockSpec((1, d), lambda i, j: (0, 0)),
            pl.BlockSpec((d, bn), lambda i, j: (0, j)),
        ],
        out_specs=pl.BlockSpec((bm, bn), lambda i, j: (i, j)),
        out_shape=jax.ShapeDtypeStruct((m, n), F32),
        scratch_shapes=[pltpu.VMEM((bm, d), BF16)],
        compiler_params=pltpu.CompilerParams(
            dimension_semantics=("parallel", "arbitrary"), vmem_limit_bytes=VMEM_LIMIT),
        name="proj",
    )(x, g.reshape(1, d), w)


def _qk(q, k):
    return lax.dot_general(q, k, (((1,), (1,)), ((), ())), preferred_element_type=F32)


def _aligned(x, m):
    return x if isinstance(x, int) else pl.multiple_of(x, m)


def _prompt_attn_kernel(q_ref, k_ref, v_ref, bn_ref, bf_ref, o_ref,
                        qb_ref, kb_ref, vb_ref, qf_ref, kf_ref, vf_ref, of_ref, lf_ref, *, seq):
    sub = seq // FAR_D
    chunk = 2 * Q_BLK

    def cast_body(c, carry):
        r0 = pl.multiple_of(c * chunk, chunk)
        rows = pl.ds(r0, chunk)
        qb_ref[rows, :] = (q_ref[rows, :] * SCALE).astype(BF16)
        kb_ref[rows, :] = k_ref[rows, :].astype(BF16)
        vb_ref[rows, :] = v_ref[rows, :].astype(BF16)
        return carry

    lax.fori_loop(0, seq // chunk, cast_body, 0)

    for r in range(FAR_D):
        src = pl.ds(r, sub, stride=FAR_D)
        qf_ref[r * sub:(r + 1) * sub, :] = (q_ref[src, :] * SCALE).astype(BF16)
        kf_ref[r * sub:(r + 1) * sub, :] = k_ref[src, :].astype(BF16)
        vf_ref[r * sub:(r + 1) * sub, :] = v_ref[src, :].astype(BF16)

    def far_tile(r, a_blk):
        n_kt = min(a_blk, 1) + 1
        q0 = pl.multiple_of(r * sub + a_blk * Q_BLK, Q_BLK)
        k0 = pl.multiple_of(r * sub + (a_blk + 1 - n_kt) * Q_BLK, Q_BLK)
        s = _qk(qf_ref[pl.ds(q0, Q_BLK), :], kf_ref[pl.ds(k0, n_kt * Q_BLK), :])
        s = s + bf_ref[:, (2 - n_kt) * Q_BLK:]
        m = jnp.max(s, axis=-1, keepdims=True)
        p = jnp.exp(s - m)
        l = jnp.sum(p, axis=-1, keepdims=True)
        o = jnp.dot(p.astype(BF16), vf_ref[pl.ds(k0, n_kt * Q_BLK), :], preferred_element_type=F32)
        dst = pl.ds(a_blk * Q_BLK * FAR_D + r, Q_BLK, stride=FAR_D)
        of_ref[dst, :] = o / l
        lf_ref[dst, :] = jnp.broadcast_to(m + jnp.log(l), (Q_BLK, HEAD_DIM))

    def far_body(g, carry):
        for r_off in range(FAR_GROUP):
            for a_blk in range(sub // Q_BLK):
                far_tile(g * FAR_GROUP + r_off, a_blk)
        return carry

    lax.fori_loop(0, FAR_D // FAR_GROUP, far_body, 0)

    def near_tile(i, n_kt):
        q0 = _aligned(i * Q_BLK, Q_BLK)
        k0 = _aligned((i + 1 - n_kt) * Q_BLK, Q_BLK)
        rows = pl.ds(q0, Q_BLK)
        s = _qk(qb_ref[rows, :], kb_ref[pl.ds(k0, n_kt * Q_BLK), :])
        s = s + bn_ref[:, (NEAR_KBLK - n_kt) * Q_BLK:]
        m_far = lf_ref[rows, 0:1]
        m = jnp.maximum(jnp.max(s, axis=-1, keepdims=True), m_far)
        p = jnp.exp(s - m)
        w_far = jnp.exp(m_far - m)
        l = w_far + jnp.sum(p, axis=-1, keepdims=True)
        acc = w_far * of_ref[rows, :] + jnp.dot(p.astype(BF16), vb_ref[pl.ds(k0, n_kt * Q_BLK), :],
                                                preferred_element_type=F32)
        o_ref[rows, :] = acc / l

    n_blk = seq // Q_BLK
    n_edge = NEAR_KBLK - 1
    assert n_blk > n_edge and (n_blk - n_edge) % NEAR_GROUP == 0
    for i in range(n_edge):
        near_tile(i, i + 1)

    def near_body(g, carry):
        for off in range(NEAR_GROUP):
            near_tile(n_edge + g * NEAR_GROUP + off, NEAR_KBLK)
        return carry

    lax.fori_loop(0, (n_blk - n_edge) // NEAR_GROUP, near_body, 0)


def _prompt_attn(z, bias_near, bias_far):
    b, s, _ = z.shape
    hpb = COL_BLK // HEAD_DIM
    assert s % (FAR_D * Q_BLK) == 0

    def head_cols(cb):
        return pl.BlockSpec((None, s, HEAD_DIM), lambda bi, h: (bi, 0, cb * hpb + h))

    return pl.pallas_call(
        functools.partial(_prompt_attn_kernel, seq=s),
        grid=(b, N_ATT_HEADS),
        in_specs=[
            head_cols(CB_Q), head_cols(CB_K), head_cols(CB_V),
            pl.BlockSpec((None, Q_BLK, NEAR_KBLK * Q_BLK), lambda bi, h: (h, 0, 0)),
            pl.BlockSpec((None, Q_BLK, 2 * Q_BLK), lambda bi, h: (h, 0, 0)),
        ],
        out_specs=pl.BlockSpec((None, s, HEAD_DIM), lambda bi, h: (bi, 0, h)),
        out_shape=jax.ShapeDtypeStruct((b, s, ATT_W), F32),
        scratch_shapes=[pltpu.VMEM((s, HEAD_DIM), BF16)] * 6 + [pltpu.VMEM((s, HEAD_DIM), F32)] * 2,
        compiler_params=pltpu.CompilerParams(
            dimension_semantics=("parallel", "parallel"), vmem_limit_bytes=VMEM_LIMIT),
        name="prompt_attn",
    )(z, z, z, bias_near, bias_far)


def _sample_attn_kernel(q_ref, kn_ref, vn_ref, kc_ref, vc_ref, bias_ref,
                        o_ref, ko_ref, vo_ref, ks_ref, vs_ref, *, t_len, l_buf, n_keys, n_heads):
    pad = n_keys - l_buf - t_len
    width = n_heads * HEAD_DIM
    for src_new, cache, src, out in ((kn_ref, kc_ref, ks_ref, ko_ref), (vn_ref, vc_ref, vs_ref, vo_ref)):
        new = src_new[...]
        src[0:l_buf, :] = cache[...].astype(BF16)
        src[l_buf:n_keys, :] = jnp.concatenate(
            [new, jnp.zeros((pad, width), F32)], axis=0).astype(BF16)
        out[0:l_buf - t_len, :] = cache[t_len:l_buf, :]
        out[l_buf - t_len:l_buf, :] = new

    rows = n_heads * t_len
    row_head = lax.broadcasted_iota(jnp.int32, (rows, width), 0) // t_len
    col_head = lax.broadcasted_iota(jnp.int32, (rows, width), 1) // HEAD_DIM
    diag = row_head == col_head
    q = q_ref[...] * SCALE
    qbd = jnp.where(diag, jnp.concatenate([q] * n_heads, axis=0), 0.0).astype(BF16)
    s = lax.dot_general(qbd, ks_ref[...], (((1,), (1,)), ((), ())), preferred_element_type=F32)
    s = s + bias_ref[...]
    m = jnp.max(s, axis=-1, keepdims=True)
    p = jnp.exp(s - m)
    l = jnp.sum(p, axis=-1, keepdims=True)
    o = jnp.dot(p.astype(BF16), vs_ref[...], preferred_element_type=F32) / l
    o = jnp.where(diag, o, 0.0)
    acc = o[0:t_len, :]
    for h in range(1, n_heads):
        acc = acc + o[h * t_len:(h + 1) * t_len, :]
    o_ref[...] = acc


def _sample_attn(z, cache_k, cache_v, bias, n_keys):
    b, t_len, _ = z.shape
    l_buf = cache_k.shape[1]
    n_heads = COL_BLK // HEAD_DIM
    n_grp = ATT_W // COL_BLK
    rows = n_heads * t_len
    cache_spec = pl.BlockSpec((None, l_buf, COL_BLK), lambda bi, g: (bi, 0, g))
    return pl.pallas_call(
        functools.partial(_sample_attn_kernel, t_len=t_len, l_buf=l_buf, n_keys=n_keys, n_heads=n_heads),
        grid=(b, n_grp),
        in_specs=[
            pl.BlockSpec((None, t_len, COL_BLK), lambda bi, g: (bi, 0, CB_Q + g)),
            pl.BlockSpec((None, t_len, COL_BLK), lambda bi, g: (bi, 0, CB_K + g)),
            pl.BlockSpec((None, t_len, COL_BLK), lambda bi, g: (bi, 0, CB_V + g)),
            cache_spec, cache_spec,
            pl.BlockSpec((rows, n_keys), lambda bi, g: (g, 0)),
        ],
        out_specs=[
            pl.BlockSpec((None, t_len, COL_BLK), lambda bi, g: (bi, 0, g)),
            cache_spec, cache_spec,
        ],
        out_shape=[
            jax.ShapeDtypeStruct((b, t_len, ATT_W), F32),
            jax.ShapeDtypeStruct(cache_k.shape, F32),
            jax.ShapeDtypeStruct(cache_v.shape, F32),
        ],
        scratch_shapes=[pltpu.VMEM((n_keys, COL_BLK), BF16), pltpu.VMEM((n_keys, COL_BLK), BF16)],
        compiler_params=pltpu.CompilerParams(
            dimension_semantics=("parallel", "parallel"), vmem_limit_bytes=VMEM_LIMIT),
        name="sample_attn",
    )(z, z, z, cache_k, cache_v, bias)


def _silu(x):
    return x * jax.nn.sigmoid(x)


def _mix_kernel(x_ref, a_ref, ga_ref, uv_ref, ug_ref, gc_ref, qm_ref, gm_ref, cinit_ref, mk_ref, mv_ref,
                wdw_ref, bdw_ref, lng_ref, lnb_ref, wpw_ref, wout_ref, gpost_ref,
                xo_ref, cs_ref, uext_ref, mix_ref, *, tile, conv_chunk):
    @pl.when(pl.program_id(1) == 0)
    def _():
        uext_ref[0:HALO, :] = cinit_ref[...]

    uext_ref[HALO:HALO + tile, :] = uv_ref[...] * jax.nn.sigmoid(ug_ref[...])

    first = HALO - (CONV_K - 1)
    chunks = []
    for c0 in range(0, tile, conv_chunk):
        acc = jnp.broadcast_to(bdw_ref[...], (conv_chunk, CONV_CH))
        for k in range(CONV_K):
            acc = acc + uext_ref[c0 + first + k:c0 + first + k + conv_chunk, :] * wdw_ref[k:k + 1, :]
        chunks.append(acc)
    c = jnp.concatenate(chunks, axis=0) if len(chunks) > 1 else chunks[0]

    cs_ref[...] = uext_ref[tile + first:tile + HALO, :]
    tail = uext_ref[tile:tile + HALO, :]
    uext_ref[0:HALO, :] = tail

    mu = jnp.mean(c, axis=-1, keepdims=True)
    var = jnp.mean(jnp.square(c - mu), axis=-1, keepdims=True)
    c = (c - mu) * lax.rsqrt(var + EPS) * lng_ref[...] + lnb_ref[...]
    c = jnp.dot(_silu(c).astype(BF16), wpw_ref[...], preferred_element_type=F32)

    mix_ref[:, 0:ATT_W] = (a_ref[...] * _silu(ga_ref[...])).astype(BF16)
    mix_ref[:, ATT_W:ATT_W + CONV_CH] = (c * _silu(gc_ref[...])).astype(BF16)

    for h in range(N_X_HEADS):
        cols = slice(h * HEAD_DIM, (h + 1) * HEAD_DIM)
        q = (qm_ref[:, cols] * SCALE).astype(BF16)
        s = lax.dot_general(q, mk_ref[:, cols].astype(BF16), (((1,), (1,)), ((), ())),
                            preferred_element_type=F32)
        p = jnp.exp(s - jnp.max(s, axis=-1, keepdims=True))
        o = jnp.dot(p.astype(BF16), mv_ref[:, cols].astype(BF16), preferred_element_type=F32)
        o = o / jnp.sum(p, axis=-1, keepdims=True)
        off = ATT_W + CONV_CH + h * HEAD_DIM
        mix_ref[:, off:off + HEAD_DIM] = (o * _silu(gm_ref[:, cols])).astype(BF16)

    y = jnp.dot(mix_ref[...], wout_ref[...], preferred_element_type=F32)
    y = y * lax.rsqrt(jnp.mean(y * y, axis=-1, keepdims=True) + EPS) * gpost_ref[...]
    xo_ref[...] = x_ref[...] + y


def _mix(x, z, a, cinit, mk_arr, mk_blk, mv_arr, mv_blk, w_dw, b_dw, ln_g, ln_b, w_pw2, w_out, g_post,
         *, tile):
    b, s, _ = x.shape
    conv_chunk = min(tile, 32)

    def zspec(cb, width=COL_BLK):
        return pl.BlockSpec((None, tile, width), lambda bi, t: (bi, t, cb * COL_BLK // width))

    def const(shape):
        return pl.BlockSpec(shape, lambda bi, t: (0,) * len(shape))

    return pl.pallas_call(
        functools.partial(_mix_kernel, tile=tile, conv_chunk=conv_chunk),
        grid=(b, s // tile),
        in_specs=[
            pl.BlockSpec((None, tile, D_MODEL), lambda bi, t: (bi, t, 0)),
            pl.BlockSpec((None, tile, ATT_W), lambda bi, t: (bi, t, 0)),
            zspec(CB_GA, ATT_W), zspec(CB_UV), zspec(CB_UG), zspec(CB_GC), zspec(CB_QM), zspec(CB_GM),
            pl.BlockSpec((None, HALO, CONV_CH), lambda bi, t: (bi, 0, 0)),
            pl.BlockSpec((None, N_MEM, X_W), lambda bi, t: (bi, 0, mk_blk)),
            pl.BlockSpec((None, N_MEM, X_W), lambda bi, t: (bi, 0, mv_blk)),
            const((CONV_K, CONV_CH)), const((1, CONV_CH)), const((1, CONV_CH)), const((1, CONV_CH)),
            const((CONV_CH, CONV_CH)), const((MIX_W, D_MODEL)), const((1, D_MODEL)),
        ],
        out_specs=[
            pl.BlockSpec((None, tile, D_MODEL), lambda bi, t: (bi, t, 0)),
            pl.BlockSpec((None, CONV_K - 1, CONV_CH), lambda bi, t: (bi, 0, 0)),
        ],
        out_shape=[
            jax.ShapeDtypeStruct((b, s, D_MODEL), F32),
            jax.ShapeDtypeStruct((b, CONV_K - 1, CONV_CH), F32),
        ],
        scratch_shapes=[pltpu.VMEM((HALO + tile, CONV_CH), F32), pltpu.VMEM((tile, MIX_W), BF16)],
        compiler_params=pltpu.CompilerParams(
            dimension_semantics=("parallel", "arbitrary"), vmem_limit_bytes=VMEM_LIMIT),
        name="mix",
    )(x, a, z, z, z, z, z, z, cinit, mk_arr, mv_arr,
      w_dw, b_dw.reshape(1, -1), ln_g.reshape(1, -1), ln_b.reshape(1, -1), w_pw2, w_out,
      g_post.reshape(1, -1))


def kernel(x_prompt, x_sample, mem_prompt, cache_attn_k, cache_attn_v, state_conv, cache_mem_k, cache_mem_v,
           rel_bias, norm_pre_g, w_in, w_dw, b_dw, ln_conv_g, ln_conv_b, w_pw2, w_mem_kv, w_out, norm_post_g):
    bp, s_len, _ = x_prompt.shape
    bs, t_len, _ = x_sample.shape
    l_buf = cache_attn_k.shape[2]
    l_prompt = min(WIN, s_len)
    n_keys_s = -(-(l_buf + t_len) // Q_BLK) * Q_BLK

    w_in_b, w_pw2_b = w_in.astype(BF16), w_pw2.astype(BF16)
    w_mem_b, w_out_b = w_mem_kv.astype(BF16), w_out.astype(BF16)

    tab_near = _distance_table(rel_bias, np.arange(NEAR_REACH + 1), NEAR_PATTERNS)
    tab_far = _distance_table(rel_bias, FAR_D * np.arange(FAR_KEYS + 1), DIL_PATTERNS[-1:])
    bias_near = _toeplitz(tab_near, NEAR_REACH, Q_BLK, NEAR_KBLK * Q_BLK)
    bias_far = _toeplitz(tab_far, FAR_KEYS, Q_BLK, 2 * Q_BLK)
    tab_all = _distance_table(rel_bias, np.arange(WIN + 1), DIL_PATTERNS)
    bias_s = _sample_bias(tab_all, t_len, l_buf, n_keys_s)

    mem2d = mem_prompt.reshape(bp * N_MEM, D_MODEL)
    ones_g = jnp.ones((D_MODEL,), F32)
    cinit_p = jnp.zeros((bp, HALO, CONV_CH), F32)
    pad_s = jnp.zeros((DEPTH, bs, HALO - (CONV_K - 1), CONV_CH), F32)
    cinit_s = jnp.concatenate([pad_s, state_conv], axis=2)
    ck = cache_attn_k.reshape(DEPTH, bs, l_buf, ATT_W)
    cv = cache_attn_v.reshape(DEPTH, bs, l_buf, ATT_W)
    cmk = cache_mem_k.reshape(DEPTH, bs, N_MEM, X_W)
    cmv = cache_mem_v.reshape(DEPTH, bs, N_MEM, X_W)

    xp, xs = x_prompt, x_sample
    akp, avp, cvp, mkp, mvp, aks, avs, cvs = [], [], [], [], [], [], [], []
    for li in range(DEPTH):
        wargs = (w_dw[li], b_dw[li], ln_conv_g[li], ln_conv_b[li], w_pw2_b[li], w_out_b[li], norm_post_g[li])

        zp = _proj(xp.reshape(bp * s_len, D_MODEL), norm_pre_g[li], w_in_b[li],
                   normalize=True, bm=1024, bn=COL_BLK).reshape(bp, s_len, IN_W)
        a = _prompt_attn(zp, bias_near, bias_far)
        mkv = _proj(mem2d, ones_g, w_mem_b[li], normalize=False, bm=bp * N_MEM, bn=COL_BLK)
        mkv = mkv.reshape(bp, N_MEM, 2 * X_W)
        xp, cst = _mix(xp, zp, a, cinit_p, mkv, 0, mkv, 1, *wargs, tile=256)
        akp.append(zp[:, s_len - l_prompt:, ATT_W:2 * ATT_W].reshape(bp, l_prompt, N_ATT_HEADS, HEAD_DIM))
        avp.append(zp[:, s_len - l_prompt:, 2 * ATT_W:3 * ATT_W].reshape(bp, l_prompt, N_ATT_HEADS, HEAD_DIM))
        cvp.append(cst)
        mkp.append(mkv[:, :, :X_W].reshape(bp, N_MEM, N_X_HEADS, HEAD_DIM))
        mvp.append(mkv[:, :, X_W:].reshape(bp, N_MEM, N_X_HEADS, HEAD_DIM))

        zs = _proj(xs.reshape(bs * t_len, D_MODEL), norm_pre_g[li], w_in_b[li],
                   normalize=True, bm=bs * t_len, bn=COL_BLK).reshape(bs, t_len, IN_W)
        a, ko, vo = _sample_attn(zs, ck[li], cv[li], bias_s, n_keys_s)
        xs, cst = _mix(xs, zs, a, cinit_s[li], cmk[li], 0, cmv[li], 0, *wargs, tile=t_len)
        aks.append(ko.reshape(bs, l_buf, N_ATT_HEADS, HEAD_DIM))
        avs.append(vo.reshape(bs, l_buf, N_ATT_HEADS, HEAD_DIM))
        cvs.append(cst)

    return (xp, xs, jnp.stack(akp), jnp.stack(avp), jnp.stack(cvp), jnp.stack(mkp), jnp.stack(mvp),
            jnp.stack(aks), jnp.stack(avs), jnp.stack(cvs))
```

```python
import functools

import numpy as np
import jax
import jax.numpy as jnp
from jax import lax
from jax.experimental import pallas as pl
from jax.experimental.pallas import tpu as pltpu

D_MODEL = 2048
DEPTH = 4
N_MEM = 256
HEAD_DIM = 128
ATT_W = D_MODEL // 2
N_ATT_HEADS = ATT_W // HEAD_DIM
DIL_PATTERNS = ((128, 1), (512, 4), (2048, 16))
WIN = max(w for w, _ in DIL_PATTERNS)
N_BUCKETS = 32
MAX_DIST = WIN
CONV_CH = D_MODEL // 4
CONV_K = 31
X_W = D_MODEL // 4
N_X_HEADS = X_W // HEAD_DIM
MIX_W = ATT_W + CONV_CH + X_W
IN_W = 4 * ATT_W + 3 * CONV_CH + 2 * X_W
EPS = 1e-6
NEG = -1e30
SCALE = HEAD_DIM ** -0.5

COL_BLK = 512
CB_Q, CB_K, CB_V, CB_GA = 0, 2, 4, 6
CB_UV, CB_UG, CB_GC, CB_QM, CB_GM = 8, 9, 10, 11, 12
HEADS_PER_COL_BLK = COL_BLK // HEAD_DIM

Q_BLK = 128
FAR_W, FAR_D = DIL_PATTERNS[-1]
NEAR_PATTERNS = DIL_PATTERNS[:-1]
NEAR_REACH = max(w for w, _ in NEAR_PATTERNS)
NEAR_KBLK = NEAR_REACH // Q_BLK + 1
FAR_KEYS = FAR_W // FAR_D
assert FAR_KEYS == Q_BLK and NEAR_REACH % Q_BLK == 0
FAR_GROUP = 2
NEAR_GROUP = 4
PROJ_BM = 1024
SAMPLE_RB = 512
HALO = 32
SUBLANES = 8
VMEM_LIMIT = 56 * 1024 * 1024

BF16 = jnp.bfloat16
F32 = jnp.float32


def _t5_bucket(dist):
    dist = np.asarray(dist)
    max_exact = N_BUCKETS // 2
    large = max_exact + (np.log(np.maximum(dist, 1) / max_exact)
                         / np.log(MAX_DIST / max_exact) * (N_BUCKETS - max_exact)).astype(np.int32)
    large = np.minimum(large, N_BUCKETS - 1)
    return np.where(dist < max_exact, dist, large).astype(np.int32)


def _distance_table(rel_bias, dists, patterns):
    dists = np.asarray(dists)
    cnt = np.zeros(dists.shape, np.int64)
    for w, dil in patterns:
        cnt += ((dists % dil == 0) & (dists <= w)).astype(np.int64)
    logc = jnp.log(jnp.asarray(np.maximum(cnt, 1), F32))
    tab = rel_bias[_t5_bucket(dists)].T.astype(F32) + logc[None, :]
    tab = jnp.where(jnp.asarray(cnt > 0)[None, :], tab, NEG)
    return jnp.concatenate([tab, jnp.full((tab.shape[0], 1), NEG, F32)], axis=1)


def _toeplitz(tab, off, rows, cols):
    n_valid = tab.shape[1] - 1
    lp = rows + cols
    j = np.concatenate([np.arange(cols), np.full(lp - (rows + cols - 1), off + n_valid),
                        np.arange(-(rows - 1), 0)])
    d = off - j
    idx = np.where((d >= 0) & (d < n_valid), d, n_valid)
    v = jnp.take(tab, jnp.asarray(idx, jnp.int32), axis=1)
    flat = jnp.tile(v, (1, rows))[:, :rows * (lp - 1)]
    return flat.reshape(tab.shape[0], rows, lp - 1)[:, :, :cols]


def _head_matched(toep):
    n_h, t_len, n_l = toep.shape
    same = jnp.asarray(np.eye(n_h, dtype=bool))[:, None, None, :]
    out = jnp.where(same, toep[:, :, :, None], NEG)
    return out.reshape(n_h * t_len, n_l * n_h)


def _norm_rows(x_ref, g_ref, h_ref, normalize):
    x = x_ref[...]
    if normalize:
        x = x * lax.rsqrt(jnp.mean(x * x, axis=-1, keepdims=True) + EPS) * g_ref[...]
    h_ref[...] = x.astype(BF16)


def _proj_kernel(x_ref, g_ref, w_ref, o_ref, h_ref, *, normalize):
    @pl.when(pl.program_id(1) == 0)
    def _():
        _norm_rows(x_ref, g_ref, h_ref, normalize)

    o_ref[...] = jnp.dot(h_ref[...], w_ref[...], preferred_element_type=F32)


def _proj(x, g, w, *, normalize, bm, bn):
    m, d = x.shape
    n = w.shape[1]
    return pl.pallas_call(
        functools.partial(_proj_kernel, normalize=normalize),
        grid=(m // bm, n // bn),
        in_specs=[
            pl.BlockSpec((bm, d), lambda i, j: (i, 0)),
            pl.BlockSpec((1, d), lambda i, j: (0, 0)),
            pl.BlockSpec((d, bn), lambda i, j: (0, j)),
        ],
        out_specs=pl.BlockSpec((bm, bn), lambda i, j: (i, j)),
        out_shape=jax.ShapeDtypeStruct((m, n), F32),
        scratch_shapes=[pltpu.VMEM((bm, d), BF16)],
        compiler_params=pltpu.CompilerParams(
            dimension_semantics=("parallel", "arbitrary"), vmem_limit_bytes=VMEM_LIMIT),
        name="proj",
    )(x, g.reshape(1, d), w)


def _proj_kv_kernel(*refs, tiles_per_seq, tail_tiles, aliased):
    x_ref, g_ref, w_ref = refs[:3]
    o_ref, ko_ref, vo_ref, h_ref = refs[5:] if aliased else refs[3:]
    i, j = pl.program_id(0), pl.program_id(1)

    @pl.when(j == 0)
    def _():
        _norm_rows(x_ref, g_ref, h_ref, True)

    o_ref[...] = jnp.dot(h_ref[...], w_ref[...], preferred_element_type=F32)

    in_tail = i % tiles_per_seq >= tiles_per_seq - tail_tiles
    for cb0, dst in ((CB_K, ko_ref), (CB_V, vo_ref)):
        for cb in range(cb0, cb0 + ATT_W // COL_BLK):
            @pl.when(in_tail & (j == cb))
            def _(cb=cb, cb0=cb0, dst=dst):
                for hh in range(HEADS_PER_COL_BLK):
                    head = (cb - cb0) * HEADS_PER_COL_BLK + hh
                    dst[:, head, :] = o_ref[:, hh * HEAD_DIM:(hh + 1) * HEAD_DIM]


def _proj_kv(x, g, w, prev_k, prev_v, *, li, seq, l_keep):
    m, d = x.shape
    n = w.shape[1]
    bsz = m // seq
    bm, bn = PROJ_BM, COL_BLK
    tiles_per_seq, tail_tiles = seq // bm, l_keep // bm
    assert seq % bm == 0 and l_keep % bm == 0
    aliased = prev_k is not None
    kv_shape = jax.ShapeDtypeStruct((DEPTH, bsz, l_keep, N_ATT_HEADS, HEAD_DIM), F32)

    def kv_map(i, j):
        return (li, i // tiles_per_seq, jnp.maximum(i % tiles_per_seq - (tiles_per_seq - tail_tiles), 0), 0, 0)

    kv_spec = pl.BlockSpec((None, None, bm, N_ATT_HEADS, HEAD_DIM), kv_map)
    in_specs = [
        pl.BlockSpec((bm, d), lambda i, j: (i, 0)),
        pl.BlockSpec((1, d), lambda i, j: (0, 0)),
        pl.BlockSpec((d, bn), lambda i, j: (0, j)),
    ]
    args = [x, g.reshape(1, d), w]
    if aliased:
        in_specs += [pl.BlockSpec(memory_space=pl.ANY)] * 2
        args += [prev_k, prev_v]
    return pl.pallas_call(
        functools.partial(_proj_kv_kernel, tiles_per_seq=tiles_per_seq, tail_tiles=tail_tiles,
                          aliased=aliased),
        grid=(m // bm, n // bn),
        in_specs=in_specs,
        out_specs=[pl.BlockSpec((bm, bn), lambda i, j: (i, j)), kv_spec, kv_spec],
        out_shape=[jax.ShapeDtypeStruct((m, n), F32), kv_shape, kv_shape],
        scratch_shapes=[pltpu.VMEM((bm, d), BF16)],
        input_output_aliases={3: 1, 4: 2} if aliased else {},
        compiler_params=pltpu.CompilerParams(
            dimension_semantics=("arbitrary", "arbitrary"), vmem_limit_bytes=VMEM_LIMIT),
        name="proj_kv",
    )(*args)


def _qk(q, k):
    return lax.dot_general(q, k, (((1,), (1,)), ((), ())), preferred_element_type=F32)


def _aligned(x, m):
    return x if isinstance(x, int) else pl.multiple_of(x, m)


def _prompt_attn_kernel(q_ref, k_ref, v_ref, bn_ref, bf_ref, o_ref,
                        qb_ref, kb_ref, vb_ref, qf_ref, kf_ref, vf_ref, of_ref, lf_ref, *, seq):
    sub = seq // FAR_D
    chunk = 2 * Q_BLK

    def cast_body(c, carry):
        r0 = pl.multiple_of(c * chunk, chunk)
        rows = pl.ds(r0, chunk)
        qb_ref[rows, :] = (q_ref[rows, :] * SCALE).astype(BF16)
        kb_ref[rows, :] = k_ref[rows, :].astype(BF16)
        vb_ref[rows, :] = v_ref[rows, :].astype(BF16)
        return carry

    lax.fori_loop(0, seq // chunk, cast_body, 0)

    for r in range(FAR_D):
        src = pl.ds(r, sub, stride=FAR_D)
        qf_ref[r * sub:(r + 1) * sub, :] = (q_ref[src, :] * SCALE).astype(BF16)
        kf_ref[r * sub:(r + 1) * sub, :] = k_ref[src, :].astype(BF16)
        vf_ref[r * sub:(r + 1) * sub, :] = v_ref[src, :].astype(BF16)

    def far_tile(r, a_blk):
        n_kt = min(a_blk, 1) + 1
        q0 = pl.multiple_of(r * sub + a_blk * Q_BLK, Q_BLK)
        k0 = pl.multiple_of(r * sub + (a_blk + 1 - n_kt) * Q_BLK, Q_BLK)
        s = _qk(qf_ref[pl.ds(q0, Q_BLK), :], kf_ref[pl.ds(k0, n_kt * Q_BLK), :])
        s = s + bf_ref[:, (2 - n_kt) * Q_BLK:]
        m = jnp.max(s, axis=-1, keepdims=True)
        p = jnp.exp(s - m)
        l = jnp.sum(p, axis=-1, keepdims=True)
        o = jnp.dot(p.astype(BF16), vf_ref[pl.ds(k0, n_kt * Q_BLK), :], preferred_element_type=F32)
        dst = pl.ds(a_blk * Q_BLK * FAR_D + r, Q_BLK, stride=FAR_D)
        of_ref[dst, :] = o / l
        lf_ref[dst, :] = jnp.broadcast_to(m + jnp.log(l), (Q_BLK, HEAD_DIM))

    def far_body(g, carry):
        for r_off in range(FAR_GROUP):
            for a_blk in range(sub // Q_BLK):
                far_tile(g * FAR_GROUP + r_off, a_blk)
        return carry

    lax.fori_loop(0, FAR_D // FAR_GROUP, far_body, 0)

    def near_tile(i, n_kt):
        q0 = _aligned(i * Q_BLK, Q_BLK)
        k0 = _aligned((i + 1 - n_kt) * Q_BLK, Q_BLK)
        rows = pl.ds(q0, Q_BLK)
        s = _qk(qb_ref[rows, :], kb_ref[pl.ds(k0, n_kt * Q_BLK), :])
        s = s + bn_ref[:, (NEAR_KBLK - n_kt) * Q_BLK:]
        m_far = lf_ref[rows, 0:1]
        m = jnp.maximum(jnp.max(s, axis=-1, keepdims=True), m_far)
        p = jnp.exp(s - m)
        w_far = jnp.exp(m_far - m)
        l = w_far + jnp.sum(p, axis=-1, keepdims=True)
        acc = w_far * of_ref[rows, :] + jnp.dot(p.astype(BF16), vb_ref[pl.ds(k0, n_kt * Q_BLK), :],
                                                preferred_element_type=F32)
        o_ref[rows, :] = acc / l

    n_blk = seq // Q_BLK
    n_edge = NEAR_KBLK - 1
    assert n_blk > n_edge and (n_blk - n_edge) % NEAR_GROUP == 0
    for i in range(n_edge):
        near_tile(i, i + 1)

    def near_body(g, carry):
        for off in range(NEAR_GROUP):
            near_tile(n_edge + g * NEAR_GROUP + off, NEAR_KBLK)
        return carry

    lax.fori_loop(0, (n_blk - n_edge) // NEAR_GROUP, near_body, 0)


def _prompt_attn(z, bias_near, bias_far):
    b, s, _ = z.shape
    assert s % (FAR_D * Q_BLK) == 0

    def head_cols(cb):
        return pl.BlockSpec((None, s, HEAD_DIM), lambda bi, h: (bi, 0, cb * HEADS_PER_COL_BLK + h))

    return pl.pallas_call(
        functools.partial(_prompt_attn_kernel, seq=s),
        grid=(b, N_ATT_HEADS),
        in_specs=[
            head_cols(CB_Q), head_cols(CB_K), head_cols(CB_V),
            pl.BlockSpec((None, Q_BLK, NEAR_KBLK * Q_BLK), lambda bi, h: (h, 0, 0)),
            pl.BlockSpec((None, Q_BLK, 2 * Q_BLK), lambda bi, h: (h, 0, 0)),
        ],
        out_specs=pl.BlockSpec((None, s, HEAD_DIM), lambda bi, h: (bi, 0, h)),
        out_shape=jax.ShapeDtypeStruct((b, s, ATT_W), F32),
        scratch_shapes=[pltpu.VMEM((s, HEAD_DIM), BF16)] * 6 + [pltpu.VMEM((s, HEAD_DIM), F32)] * 2,
        compiler_params=pltpu.CompilerParams(
            dimension_semantics=("parallel", "parallel"), vmem_limit_bytes=VMEM_LIMIT),
        name="prompt_attn",
    )(z, z, z, bias_near, bias_far)


def _sample_attn_kernel(*refs, t_len, rb, aliased):
    (q_ref, kn_ref, vn_ref, kc_ref, vc_ref, kx_ref, vx_ref, bias_ref, biasn_ref) = refs[:9]
    (o_ref, ko_ref, vo_ref, qa_ref, kn3_ref, vn3_ref, m_ref, l_ref, acc_ref) = refs[11:] if aliased else refs[9:]
    j = pl.program_id(1)
    last = pl.num_programs(1) - 1
    n_heads = N_ATT_HEADS

    @pl.when(j == 0)
    def _():
        for h in range(n_heads):
            cols = slice(h * HEAD_DIM, (h + 1) * HEAD_DIM)
            qa_ref[h * t_len:(h + 1) * t_len, :] = q_ref[:, cols] * SCALE
            kn3_ref[:, h, :] = kn_ref[:, cols]
            vn3_ref[:, h, :] = vn_ref[:, cols]
        m_ref[...] = jnp.full(m_ref.shape, NEG, F32)
        l_ref[...] = jnp.zeros(l_ref.shape, F32)
        acc_ref[...] = jnp.zeros(acc_ref.shape, F32)

    qa = qa_ref[...].astype(BF16)

    def accumulate(k, v, bias):
        s = _qk(qa, k.astype(BF16)) + bias
        m_old = m_ref[...]
        m_new = jnp.maximum(m_old, jnp.max(s, axis=-1, keepdims=True))
        alpha = jnp.exp(m_old - m_new)
        p = jnp.exp(s - m_new)
        l_ref[...] = alpha * l_ref[...] + jnp.sum(p, axis=-1, keepdims=True)
        acc_ref[...] = alpha * acc_ref[...] + jnp.dot(p.astype(BF16), v.astype(BF16),
                                                      preferred_element_type=F32)
        m_ref[...] = m_new

    accumulate(kc_ref[...].reshape(rb * n_heads, HEAD_DIM), vc_ref[...].reshape(rb * n_heads, HEAD_DIM),
               bias_ref[...])

    for cache, nxt, new, out in ((kc_ref, kx_ref, kn3_ref, ko_ref), (vc_ref, vx_ref, vn3_ref, vo_ref)):
        out[0:rb - t_len] = cache[t_len:rb]

        @pl.when(j < last)
        def _(nxt=nxt, out=out):
            out[rb - t_len:rb] = nxt[...]

        @pl.when(j == last)
        def _(new=new, out=out):
            out[rb - t_len:rb] = new[...]

    @pl.when(j == last)
    def _():
        accumulate(kn3_ref[...].reshape(t_len * n_heads, HEAD_DIM),
                   vn3_ref[...].reshape(t_len * n_heads, HEAD_DIM), biasn_ref[...])
        o = acc_ref[...] / l_ref[...]
        for h in range(n_heads):
            o_ref[:, h * HEAD_DIM:(h + 1) * HEAD_DIM] = o[h * t_len:(h + 1) * t_len, :]


def _sample_attn(z, cache_k, cache_v, bias, bias_new, prev_k, prev_v, *, li):
    b, t_len, _ = z.shape
    l_buf = cache_k.shape[2]
    rb = SAMPLE_RB
    assert t_len == SUBLANES and l_buf % rb == 0
    rows = N_ATT_HEADS * t_len
    aliased = prev_k is not None
    n_next = l_buf // t_len

    def zspec(cb):
        return pl.BlockSpec((None, t_len, ATT_W), lambda bi, j: (bi, 0, cb * COL_BLK // ATT_W))

    blk_spec = pl.BlockSpec((None, None, rb, N_ATT_HEADS, HEAD_DIM), lambda bi, j: (li, bi, j, 0, 0))
    next_spec = pl.BlockSpec(
        (None, None, t_len, N_ATT_HEADS, HEAD_DIM),
        lambda bi, j: (li, bi, jnp.minimum((j + 1) * (rb // t_len), n_next - 1), 0, 0))
    in_specs = [
        zspec(CB_Q), zspec(CB_K), zspec(CB_V), blk_spec, blk_spec, next_spec, next_spec,
        pl.BlockSpec((rows, rb * N_ATT_HEADS), lambda bi, j: (0, j)),
        pl.BlockSpec((rows, rows), lambda bi, j: (0, 0)),
    ]
    args = [z, z, z, cache_k, cache_v, cache_k, cache_v, bias, bias_new]
    if aliased:
        in_specs += [pl.BlockSpec(memory_space=pl.ANY)] * 2
        args += [prev_k, prev_v]
    return pl.pallas_call(
        functools.partial(_sample_attn_kernel, t_len=t_len, rb=rb, aliased=aliased),
        grid=(b, l_buf // rb),
        in_specs=in_specs,
        out_specs=[pl.BlockSpec((None, t_len, ATT_W), lambda bi, j: (bi, 0, 0)), blk_spec, blk_spec],
        out_shape=[
            jax.ShapeDtypeStruct((b, t_len, ATT_W), F32),
            jax.ShapeDtypeStruct(cache_k.shape, F32),
            jax.ShapeDtypeStruct(cache_v.shape, F32),
        ],
        scratch_shapes=[
            pltpu.VMEM((rows, HEAD_DIM), F32),
            pltpu.VMEM((t_len, N_ATT_HEADS, HEAD_DIM), F32), pltpu.VMEM((t_len, N_ATT_HEADS, HEAD_DIM), F32),
            pltpu.VMEM((rows, 1), F32), pltpu.VMEM((rows, 1), F32), pltpu.VMEM((rows, HEAD_DIM), F32),
        ],
        input_output_aliases={9: 1, 10: 2} if aliased else {},
        compiler_params=pltpu.CompilerParams(
            dimension_semantics=("arbitrary", "arbitrary"), vmem_limit_bytes=VMEM_LIMIT),
        name="sample_attn",
    )(*args)


def _silu(x):
    return x * jax.nn.sigmoid(x)


def _mix_kernel(x_ref, a_ref, ga_ref, uv_ref, ug_ref, gc_ref, qm_ref, gm_ref, cinit_ref, mk_ref, mv_ref,
                wdw_ref, bdw_ref, lng_ref, lnb_ref, wpw_ref, wout_ref, gpost_ref,
                xo_ref, cs_ref, uext_ref, mix_ref, *, tile, conv_chunk):
    @pl.when(pl.program_id(1) == 0)
    def _():
        uext_ref[0:HALO, :] = cinit_ref[...]

    uext_ref[HALO:HALO + tile, :] = uv_ref[...] * jax.nn.sigmoid(ug_ref[...])

    first = HALO - (CONV_K - 1)
    chunks = []
    for c0 in range(0, tile, conv_chunk):
        acc = jnp.broadcast_to(bdw_ref[...], (conv_chunk, CONV_CH))
        for k in range(CONV_K):
            acc = acc + uext_ref[c0 + first + k:c0 + first + k + conv_chunk, :] * wdw_ref[k:k + 1, :]
        chunks.append(acc)
    c = jnp.concatenate(chunks, axis=0) if len(chunks) > 1 else chunks[0]

    cs_ref[...] = uext_ref[tile + first:tile + HALO, :]
    tail = uext_ref[tile:tile + HALO, :]
    uext_ref[0:HALO, :] = tail

    mu = jnp.mean(c, axis=-1, keepdims=True)
    var = jnp.mean(jnp.square(c - mu), axis=-1, keepdims=True)
    c = (c - mu) * lax.rsqrt(var + EPS) * lng_ref[...] + lnb_ref[...]
    c = jnp.dot(_silu(c).astype(BF16), wpw_ref[...], preferred_element_type=F32)

    mix_ref[:, 0:ATT_W] = (a_ref[...] * _silu(ga_ref[...])).astype(BF16)
    mix_ref[:, ATT_W:ATT_W + CONV_CH] = (c * _silu(gc_ref[...])).astype(BF16)

    for h in range(N_X_HEADS):
        cols = slice(h * HEAD_DIM, (h + 1) * HEAD_DIM)
        q = (qm_ref[:, cols] * SCALE).astype(BF16)
        s = _qk(q, mk_ref[:, cols].astype(BF16))
        p = jnp.exp(s - jnp.max(s, axis=-1, keepdims=True))
        o = jnp.dot(p.astype(BF16), mv_ref[:, cols].astype(BF16), preferred_element_type=F32)
        o = o / jnp.sum(p, axis=-1, keepdims=True)
        off = ATT_W + CONV_CH + h * HEAD_DIM
        mix_ref[:, off:off + HEAD_DIM] = (o * _silu(gm_ref[:, cols])).astype(BF16)

    y = jnp.dot(mix_ref[...], wout_ref[...], preferred_element_type=F32)
    y = y * lax.rsqrt(jnp.mean(y * y, axis=-1, keepdims=True) + EPS) * gpost_ref[...]
    xo_ref[...] = x_ref[...] + y


def _mix(x, z, a, cinit, mk_arr, mk_blk, mv_arr, mv_blk, w_dw, b_dw, ln_g, ln_b, w_pw2, w_out, g_post,
         *, tile):
    b, s, _ = x.shape
    conv_chunk = min(tile, 32)

    def zspec(cb, width=COL_BLK):
        return pl.BlockSpec((None, tile, width), lambda bi, t: (bi, t, cb * COL_BLK // width))

    def const(shape):
        return pl.BlockSpec(shape, lambda bi, t: (0,) * len(shape))

    return pl.pallas_call(
        functools.partial(_mix_kernel, tile=tile, conv_chunk=conv_chunk),
        grid=(b, s // tile),
        in_specs=[
            pl.BlockSpec((None, tile, D_MODEL), lambda bi, t: (bi, t, 0)),
            pl.BlockSpec((None, tile, ATT_W), lambda bi, t: (bi, t, 0)),
            zspec(CB_GA, ATT_W), zspec(CB_UV), zspec(CB_UG), zspec(CB_GC), zspec(CB_QM), zspec(CB_GM),
            pl.BlockSpec((None, HALO, CONV_CH), lambda bi, t: (bi, 0, 0)),
            pl.BlockSpec((None, N_MEM, X_W), lambda bi, t: (bi, 0, mk_blk)),
            pl.BlockSpec((None, N_MEM, X_W), lambda bi, t: (bi, 0, mv_blk)),
            const((CONV_K, CONV_CH)), const((1, CONV_CH)), const((1, CONV_CH)), const((1, CONV_CH)),
            const((CONV_CH, CONV_CH)), const((MIX_W, D_MODEL)), const((1, D_MODEL)),
        ],
        out_specs=[
            pl.BlockSpec((None, tile, D_MODEL), lambda bi, t: (bi, t, 0)),
            pl.BlockSpec((None, CONV_K - 1, CONV_CH), lambda bi, t: (bi, 0, 0)),
        ],
        out_shape=[
            jax.ShapeDtypeStruct((b, s, D_MODEL), F32),
            jax.ShapeDtypeStruct((b, CONV_K - 1, CONV_CH), F32),
        ],
        scratch_shapes=[pltpu.VMEM((HALO + tile, CONV_CH), F32), pltpu.VMEM((tile, MIX_W), BF16)],
        compiler_params=pltpu.CompilerParams(
            dimension_semantics=("parallel", "arbitrary"), vmem_limit_bytes=VMEM_LIMIT),
        name="mix",
    )(x, a, z, z, z, z, z, z, cinit, mk_arr, mv_arr,
      w_dw, b_dw.reshape(1, -1), ln_g.reshape(1, -1), ln_b.reshape(1, -1), w_pw2, w_out,
      g_post.reshape(1, -1))


def kernel(x_prompt, x_sample, mem_prompt, cache_attn_k, cache_attn_v, state_conv, cache_mem_k, cache_mem_v,
           rel_bias, norm_pre_g, w_in, w_dw, b_dw, ln_conv_g, ln_conv_b, w_pw2, w_mem_kv, w_out, norm_post_g):
    bp, s_len, _ = x_prompt.shape
    bs, t_len, _ = x_sample.shape
    l_buf = cache_attn_k.shape[2]
    l_prompt = min(WIN, s_len)

    w_in_b, w_pw2_b = w_in.astype(BF16), w_pw2.astype(BF16)
    w_mem_b, w_out_b = w_mem_kv.astype(BF16), w_out.astype(BF16)

    tab_near = _distance_table(rel_bias, np.arange(NEAR_REACH + 1), NEAR_PATTERNS)
    tab_far = _distance_table(rel_bias, FAR_D * np.arange(FAR_KEYS + 1), DIL_PATTERNS[-1:])
    bias_near = _toeplitz(tab_near, NEAR_REACH, Q_BLK, NEAR_KBLK * Q_BLK)
    bias_far = _toeplitz(tab_far, FAR_KEYS, Q_BLK, 2 * Q_BLK)
    tab_all = _distance_table(rel_bias, np.arange(WIN + 1), DIL_PATTERNS)
    bias_s = _head_matched(_toeplitz(tab_all, l_buf, t_len, l_buf))
    bias_s_new = _head_matched(_toeplitz(tab_all, 0, t_len, t_len))

    mem2d = mem_prompt.reshape(bp * N_MEM, D_MODEL)
    ones_g = jnp.ones((D_MODEL,), F32)
    cinit_p = jnp.zeros((bp, HALO, CONV_CH), F32)
    pad_s = jnp.zeros((DEPTH, bs, HALO - (CONV_K - 1), CONV_CH), F32)
    cinit_s = jnp.concatenate([pad_s, state_conv], axis=2)
    cmk = cache_mem_k.reshape(DEPTH, bs, N_MEM, X_W)
    cmv = cache_mem_v.reshape(DEPTH, bs, N_MEM, X_W)

    xp, xs = x_prompt, x_sample
    akp = avp = aks = avs = None
    cvp, mkp, mvp, cvs = [], [], [], []
    for li in range(DEPTH):
        wargs = (w_dw[li], b_dw[li], ln_conv_g[li], ln_conv_b[li], w_pw2_b[li], w_out_b[li], norm_post_g[li])

        zp, akp, avp = _proj_kv(xp.reshape(bp * s_len, D_MODEL), norm_pre_g[li], w_in_b[li], akp, avp,
                                li=li, seq=s_len, l_keep=l_prompt)
        zp = zp.reshape(bp, s_len, IN_W)
        a = _prompt_attn(zp, bias_near, bias_far)
        mkv = _proj(mem2d, ones_g, w_mem_b[li], normalize=False, bm=bp * N_MEM, bn=COL_BLK)
        mkv = mkv.reshape(bp, N_MEM, 2 * X_W)
        xp, cst = _mix(xp, zp, a, cinit_p, mkv, 0, mkv, 1, *wargs, tile=256)
        cvp.append(cst)
        mkp.append(mkv[:, :, :X_W].reshape(bp, N_MEM, N_X_HEADS, HEAD_DIM))
        mvp.append(mkv[:, :, X_W:].reshape(bp, N_MEM, N_X_HEADS, HEAD_DIM))

        zs = _proj(xs.reshape(bs * t_len, D_MODEL), norm_pre_g[li], w_in_b[li],
                   normalize=True, bm=bs * t_len, bn=COL_BLK).reshape(bs, t_len, IN_W)
        a, aks, avs = _sample_attn(zs, cache_attn_k, cache_attn_v, bias_s, bias_s_new, aks, avs, li=li)
        xs, cst = _mix(xs, zs, a, cinit_s[li], cmk[li], 0, cmv[li], 0, *wargs, tile=t_len)
        cvs.append(cst)

    return (xp, xs, akp, avp, jnp.stack(cvp), jnp.stack(mkp), jnp.stack(mvp), aks, avs, jnp.stack(cvs))
```

```python
import functools

import numpy as np
import jax
import jax.numpy as jnp
from jax import lax
from jax.experimental import pallas as pl
from jax.experimental.pallas import tpu as pltpu

D_MODEL = 2048
DEPTH = 4
N_MEM = 256
HEAD_DIM = 128
ATT_W = D_MODEL // 2
N_ATT_HEADS = ATT_W // HEAD_DIM
DIL_PATTERNS = ((128, 1), (512, 4), (2048, 16))
WIN = max(w for w, _ in DIL_PATTERNS)
N_BUCKETS = 32
MAX_DIST = WIN
CONV_CH = D_MODEL // 4
CONV_K = 31
X_W = D_MODEL // 4
N_X_HEADS = X_W // HEAD_DIM
MIX_W = ATT_W + CONV_CH + X_W
IN_W = 4 * ATT_W + 3 * CONV_CH + 2 * X_W
EPS = 1e-6
NEG = -1e30
SCALE = HEAD_DIM ** -0.5

COL_BLK = 512
CB_Q, CB_K, CB_V, CB_GA = 0, 2, 4, 6
CB_UV, CB_UG, CB_GC, CB_QM, CB_GM = 8, 9, 10, 11, 12
HEADS_PER_COL_BLK = COL_BLK // HEAD_DIM

Q_BLK = 128
FAR_W, FAR_D = DIL_PATTERNS[-1]
NEAR_PATTERNS = DIL_PATTERNS[:-1]
NEAR_REACH = max(w for w, _ in NEAR_PATTERNS)
NEAR_KBLK = NEAR_REACH // Q_BLK + 1
FAR_KEYS = FAR_W // FAR_D
FAR_STEP = 4
assert FAR_KEYS == Q_BLK and NEAR_REACH % Q_BLK == 0 and FAR_STEP * FAR_STEP == FAR_D
FAR_GROUP = 4
NEAR_GROUP = 7
PROJ_BM = 1024
SAMPLE_RB = 512
HALO = 32
SUBLANES = 8
VMEM_LIMIT = 56 * 1024 * 1024

BF16 = jnp.bfloat16
F32 = jnp.float32


def _t5_bucket(dist):
    dist = np.asarray(dist)
    max_exact = N_BUCKETS // 2
    large = max_exact + (np.log(np.maximum(dist, 1) / max_exact)
                         / np.log(MAX_DIST / max_exact) * (N_BUCKETS - max_exact)).astype(np.int32)
    large = np.minimum(large, N_BUCKETS - 1)
    return np.where(dist < max_exact, dist, large).astype(np.int32)


def _distance_table(rel_bias, dists, patterns):
    dists = np.asarray(dists)
    cnt = np.zeros(dists.shape, np.int64)
    for w, dil in patterns:
        cnt += ((dists % dil == 0) & (dists <= w)).astype(np.int64)
    logc = jnp.log(jnp.asarray(np.maximum(cnt, 1), F32))
    tab = rel_bias[_t5_bucket(dists)].T.astype(F32) + logc[None, :]
    tab = jnp.where(jnp.asarray(cnt > 0)[None, :], tab, NEG)
    return jnp.concatenate([tab, jnp.full((tab.shape[0], 1), NEG, F32)], axis=1)


def _toeplitz(tab, off, rows, cols):
    n_valid = tab.shape[1] - 1
    lp = rows + cols
    j = np.concatenate([np.arange(cols), np.full(lp - (rows + cols - 1), off + n_valid),
                        np.arange(-(rows - 1), 0)])
    d = off - j
    idx = np.where((d >= 0) & (d < n_valid), d, n_valid)
    v = jnp.take(tab, jnp.asarray(idx, jnp.int32), axis=1)
    flat = jnp.tile(v, (1, rows))[:, :rows * (lp - 1)]
    return flat.reshape(tab.shape[0], rows, lp - 1)[:, :, :cols]


def _head_matched(toep):
    n_h, t_len, n_l = toep.shape
    same = jnp.asarray(np.eye(n_h, dtype=bool))[:, None, None, :]
    out = jnp.where(same, toep[:, :, :, None], NEG)
    return out.reshape(n_h * t_len, n_l * n_h)


def _norm_rows(x_ref, g_ref, h_ref, normalize):
    x = x_ref[...]
    if normalize:
        x = x * lax.rsqrt(jnp.mean(x * x, axis=-1, keepdims=True) + EPS) * g_ref[...]
    h_ref[...] = x.astype(BF16)


def _proj_kernel(x_ref, g_ref, w_ref, o_ref, h_ref, *, normalize):
    @pl.when(pl.program_id(1) == 0)
    def _():
        _norm_rows(x_ref, g_ref, h_ref, normalize)

    o_ref[...] = jnp.dot(h_ref[...], w_ref[...], preferred_element_type=F32)


def _proj(x, g, w, *, li, normalize, bm, bn):
    m, d = x.shape
    n = w.shape[2]
    return pl.pallas_call(
        functools.partial(_proj_kernel, normalize=normalize),
        grid=(m // bm, n // bn),
        in_specs=[
            pl.BlockSpec((bm, d), lambda i, j: (i, 0)),
            pl.BlockSpec((1, d), lambda i, j: (0, 0)),
            pl.BlockSpec((None, d, bn), lambda i, j: (li, 0, j)),
        ],
        out_specs=pl.BlockSpec((bm, bn), lambda i, j: (i, j)),
        out_shape=jax.ShapeDtypeStruct((m, n), F32),
        scratch_shapes=[pltpu.VMEM((bm, d), BF16)],
        compiler_params=pltpu.CompilerParams(
            dimension_semantics=("parallel", "arbitrary"), vmem_limit_bytes=VMEM_LIMIT),
        name="proj",
    )(x, g.reshape(1, d), w)


def _proj_kv_kernel(*refs, tiles_per_seq, tail_tiles, aliased):
    x_ref, g_ref, w_ref = refs[:3]
    o_ref, ko_ref, vo_ref, h_ref = refs[5:] if aliased else refs[3:]
    i, j = pl.program_id(0), pl.program_id(1)

    @pl.when(j == 0)
    def _():
        _norm_rows(x_ref, g_ref, h_ref, True)

    o_ref[...] = jnp.dot(h_ref[...], w_ref[...], preferred_element_type=F32)

    in_tail = i % tiles_per_seq >= tiles_per_seq - tail_tiles
    for cb0, dst in ((CB_K, ko_ref), (CB_V, vo_ref)):
        for cb in range(cb0, cb0 + ATT_W // COL_BLK):
            @pl.when(in_tail & (j == cb))
            def _(cb=cb, cb0=cb0, dst=dst):
                for hh in range(HEADS_PER_COL_BLK):
                    head = (cb - cb0) * HEADS_PER_COL_BLK + hh
                    dst[:, head, :] = o_ref[:, hh * HEAD_DIM:(hh + 1) * HEAD_DIM]


def _proj_kv(x, g, w, prev_k, prev_v, *, li, seq, l_keep):
    m, d = x.shape
    n = w.shape[2]
    bsz = m // seq
    bm, bn = PROJ_BM, COL_BLK
    tiles_per_seq, tail_tiles = seq // bm, l_keep // bm
    assert seq % bm == 0 and l_keep % bm == 0
    aliased = prev_k is not None
    kv_shape = jax.ShapeDtypeStruct((DEPTH, bsz, l_keep, N_ATT_HEADS, HEAD_DIM), F32)

    def kv_map(i, j):
        return (li, i // tiles_per_seq, jnp.maximum(i % tiles_per_seq - (tiles_per_seq - tail_tiles), 0), 0, 0)

    kv_spec = pl.BlockSpec((None, None, bm, N_ATT_HEADS, HEAD_DIM), kv_map)
    in_specs = [
        pl.BlockSpec((bm, d), lambda i, j: (i, 0)),
        pl.BlockSpec((1, d), lambda i, j: (0, 0)),
        pl.BlockSpec((None, d, bn), lambda i, j: (li, 0, j)),
    ]
    args = [x, g.reshape(1, d), w]
    if aliased:
        in_specs += [pl.BlockSpec(memory_space=pl.ANY)] * 2
        args += [prev_k, prev_v]
    return pl.pallas_call(
        functools.partial(_proj_kv_kernel, tiles_per_seq=tiles_per_seq, tail_tiles=tail_tiles,
                          aliased=aliased),
        grid=(m // bm, n // bn),
        in_specs=in_specs,
        out_specs=[pl.BlockSpec((bm, bn), lambda i, j: (i, j)), kv_spec, kv_spec],
        out_shape=[jax.ShapeDtypeStruct((m, n), F32), kv_shape, kv_shape],
        scratch_shapes=[pltpu.VMEM((bm, d), BF16)],
        input_output_aliases={3: 1, 4: 2} if aliased else {},
        compiler_params=pltpu.CompilerParams(
            dimension_semantics=("arbitrary", "arbitrary"), vmem_limit_bytes=VMEM_LIMIT),
        name="proj_kv",
    )(*args)


def _qk(q, k):
    return lax.dot_general(q, k, (((1,), (1,)), ((), ())), preferred_element_type=F32)


def _aligned(x, m):
    return x if isinstance(x, int) else pl.multiple_of(x, m)


def _prompt_attn_kernel(q_ref, k_ref, v_ref, bn_ref, bf_ref, o_ref,
                        qb_ref, kb_ref, vb_ref, qf_ref, kf_ref, vf_ref, of_ref, lf_ref, tmp_ref, *, seq):
    sub = seq // FAR_D
    chunk = 2 * Q_BLK

    def cast_body(c, carry):
        r0 = pl.multiple_of(c * chunk, chunk)
        rows = pl.ds(r0, chunk)
        qb_ref[rows, :] = (q_ref[rows, :] * SCALE).astype(BF16)
        kb_ref[rows, :] = k_ref[rows, :].astype(BF16)
        vb_ref[rows, :] = v_ref[rows, :].astype(BF16)
        return carry

    lax.fori_loop(0, seq // chunk, cast_body, 0)

    quarter = seq // FAR_STEP
    for src_ref, dst_ref, scale in ((q_ref, qf_ref, SCALE), (k_ref, kf_ref, None), (v_ref, vf_ref, None)):
        for r1 in range(FAR_STEP):
            tmp_ref[r1 * quarter:(r1 + 1) * quarter, :] = src_ref[pl.ds(r1, quarter, stride=FAR_STEP), :]
        for r1 in range(FAR_STEP):
            for r2 in range(FAR_STEP):
                r = FAR_STEP * r2 + r1
                x = tmp_ref[pl.ds(r1 * quarter + r2, sub, stride=FAR_STEP), :]
                if scale is not None:
                    x = x * scale
                dst_ref[r * sub:(r + 1) * sub, :] = x.astype(BF16)

    def far_tiles(tiles):
        geo = []
        for r, a_blk in tiles:
            n_kt = min(a_blk, 1) + 1
            q0 = pl.multiple_of(r * sub + a_blk * Q_BLK, Q_BLK)
            k0 = pl.multiple_of(r * sub + (a_blk + 1 - n_kt) * Q_BLK, Q_BLK)
            dst = pl.ds(a_blk * Q_BLK * FAR_D + r, Q_BLK, stride=FAR_D)
            geo.append((pl.ds(q0, Q_BLK), pl.ds(k0, n_kt * Q_BLK), n_kt, dst))
        scores = [_qk(qf_ref[rows, :], kf_ref[keys, :]) + bf_ref[:, (2 - n_kt) * Q_BLK:]
                  for rows, keys, n_kt, _ in geo]
        probs = []
        for s in scores:
            m = jnp.max(s, axis=-1, keepdims=True)
            p = jnp.exp(s - m)
            l = jnp.sum(p, axis=-1, keepdims=True)
            probs.append((p.astype(BF16), l, m + jnp.log(l)))
        for (_, keys, _, dst), (p, l, lse) in zip(geo, probs):
            o = jnp.dot(p, vf_ref[keys, :], preferred_element_type=F32)
            of_ref[dst, :] = o / l
            lf_ref[dst, :] = jnp.broadcast_to(lse, (Q_BLK, HEAD_DIM))

    def far_body(g, carry):
        far_tiles([(g * FAR_GROUP + r_off, a_blk)
                   for r_off in range(FAR_GROUP) for a_blk in range(sub // Q_BLK)])
        return carry

    lax.fori_loop(0, FAR_D // FAR_GROUP, far_body, 0)

    def near_tiles(tiles):
        geo = []
        for i, n_kt in tiles:
            q0 = _aligned(i * Q_BLK, Q_BLK)
            k0 = _aligned((i + 1 - n_kt) * Q_BLK, Q_BLK)
            geo.append((pl.ds(q0, Q_BLK), pl.ds(k0, n_kt * Q_BLK), n_kt))
        scores = [_qk(qb_ref[rows, :], kb_ref[keys, :]) + bn_ref[:, (NEAR_KBLK - n_kt) * Q_BLK:]
                  for rows, keys, n_kt in geo]
        probs = []
        for (rows, _, _), s in zip(geo, scores):
            m_far = lf_ref[rows, 0:1]
            m = jnp.maximum(jnp.max(s, axis=-1, keepdims=True), m_far)
            p = jnp.exp(s - m)
            w_far = jnp.exp(m_far - m)
            probs.append((p.astype(BF16), w_far, w_far + jnp.sum(p, axis=-1, keepdims=True)))
        for (rows, keys, _), (p, w_far, l) in zip(geo, probs):
            acc = w_far * of_ref[rows, :] + jnp.dot(p, vb_ref[keys, :], preferred_element_type=F32)
            o_ref[rows, :] = acc / l

    n_blk = seq // Q_BLK
    n_edge = NEAR_KBLK - 1
    assert n_blk > n_edge and (n_blk - n_edge) % NEAR_GROUP == 0
    near_tiles([(i, i + 1) for i in range(n_edge)])

    def near_body(g, carry):
        near_tiles([(n_edge + g * NEAR_GROUP + off, NEAR_KBLK) for off in range(NEAR_GROUP)])
        return carry

    lax.fori_loop(0, (n_blk - n_edge) // NEAR_GROUP, near_body, 0)


def _prompt_attn(z, bias_near, bias_far):
    b, s, _ = z.shape
    assert s % (FAR_D * Q_BLK) == 0

    def head_cols(cb):
        return pl.BlockSpec((None, s, HEAD_DIM), lambda bi, h: (bi, 0, cb * HEADS_PER_COL_BLK + h))

    return pl.pallas_call(
        functools.partial(_prompt_attn_kernel, seq=s),
        grid=(b, N_ATT_HEADS),
        in_specs=[
            head_cols(CB_Q), head_cols(CB_K), head_cols(CB_V),
            pl.BlockSpec((None, Q_BLK, NEAR_KBLK * Q_BLK), lambda bi, h: (h, 0, 0)),
            pl.BlockSpec((None, Q_BLK, 2 * Q_BLK), lambda bi, h: (h, 0, 0)),
        ],
        out_specs=pl.BlockSpec((None, s, HEAD_DIM), lambda bi, h: (bi, 0, h)),
        out_shape=jax.ShapeDtypeStruct((b, s, ATT_W), F32),
        scratch_shapes=[pltpu.VMEM((s, HEAD_DIM), BF16)] * 6 + [pltpu.VMEM((s, HEAD_DIM), F32)] * 3,
        compiler_params=pltpu.CompilerParams(
            dimension_semantics=("parallel", "parallel"), vmem_limit_bytes=VMEM_LIMIT),
        name="prompt_attn",
    )(z, z, z, bias_near, bias_far)


def _sample_attn_kernel(*refs, t_len, rb, aliased):
    (q_ref, kn_ref, vn_ref, kc_ref, vc_ref, kx_ref, vx_ref, bias_ref, biasn_ref) = refs[:9]
    (o_ref, ko_ref, vo_ref, qa_ref, kn3_ref, vn3_ref, m_ref, l_ref, acc_ref) = refs[11:] if aliased else refs[9:]
    j = pl.program_id(1)
    last = pl.num_programs(1) - 1
    n_heads = N_ATT_HEADS

    @pl.when(j == 0)
    def _():
        for h in range(n_heads):
            cols = slice(h * HEAD_DIM, (h + 1) * HEAD_DIM)
            qa_ref[h * t_len:(h + 1) * t_len, :] = q_ref[:, cols] * SCALE
            kn3_ref[:, h, :] = kn_ref[:, cols]
            vn3_ref[:, h, :] = vn_ref[:, cols]
        m_ref[...] = jnp.full(m_ref.shape, NEG, F32)
        l_ref[...] = jnp.zeros(l_ref.shape, F32)
        acc_ref[...] = jnp.zeros(acc_ref.shape, F32)

    qa = qa_ref[...].astype(BF16)

    def accumulate(k, v, bias):
        s = _qk(qa, k.astype(BF16)) + bias
        m_old = m_ref[...]
        m_new = jnp.maximum(m_old, jnp.max(s, axis=-1, keepdims=True))
        alpha = jnp.exp(m_old - m_new)
        p = jnp.exp(s - m_new)
        l_ref[...] = alpha * l_ref[...] + jnp.sum(p, axis=-1, keepdims=True)
        acc_ref[...] = alpha * acc_ref[...] + jnp.dot(p.astype(BF16), v.astype(BF16),
                                                      preferred_element_type=F32)
        m_ref[...] = m_new

    accumulate(kc_ref[...].reshape(rb * n_heads, HEAD_DIM), vc_ref[...].reshape(rb * n_heads, HEAD_DIM),
               bias_ref[...])

    for cache, nxt, new, out in ((kc_ref, kx_ref, kn3_ref, ko_ref), (vc_ref, vx_ref, vn3_ref, vo_ref)):
        out[0:rb - t_len] = cache[t_len:rb]

        @pl.when(j < last)
        def _(nxt=nxt, out=out):
            out[rb - t_len:rb] = nxt[...]

        @pl.when(j == last)
        def _(new=new, out=out):
            out[rb - t_len:rb] = new[...]

    @pl.when(j == last)
    def _():
        accumulate(kn3_ref[...].reshape(t_len * n_heads, HEAD_DIM),
                   vn3_ref[...].reshape(t_len * n_heads, HEAD_DIM), biasn_ref[...])
        o = acc_ref[...] / l_ref[...]
        for h in range(n_heads):
            o_ref[:, h * HEAD_DIM:(h + 1) * HEAD_DIM] = o[h * t_len:(h + 1) * t_len, :]


def _sample_attn(z, cache_k, cache_v, bias, bias_new, prev_k, prev_v, *, li):
    b, t_len, _ = z.shape
    l_buf = cache_k.shape[2]
    rb = SAMPLE_RB
    assert t_len == SUBLANES and l_buf % rb == 0
    rows = N_ATT_HEADS * t_len
    aliased = prev_k is not None
    n_next = l_buf // t_len

    def zspec(cb):
        return pl.BlockSpec((None, t_len, ATT_W), lambda bi, j: (bi, 0, cb * COL_BLK // ATT_W))

    blk_spec = pl.BlockSpec((None, None, rb, N_ATT_HEADS, HEAD_DIM), lambda bi, j: (li, bi, j, 0, 0))
    next_spec = pl.BlockSpec(
        (None, None, t_len, N_ATT_HEADS, HEAD_DIM),
        lambda bi, j: (li, bi, jnp.minimum((j + 1) * (rb // t_len), n_next - 1), 0, 0))
    in_specs = [
        zspec(CB_Q), zspec(CB_K), zspec(CB_V), blk_spec, blk_spec, next_spec, next_spec,
        pl.BlockSpec((rows, rb * N_ATT_HEADS), lambda bi, j: (0, j)),
        pl.BlockSpec((rows, rows), lambda bi, j: (0, 0)),
    ]
    args = [z, z, z, cache_k, cache_v, cache_k, cache_v, bias, bias_new]
    if aliased:
        in_specs += [pl.BlockSpec(memory_space=pl.ANY)] * 2
        args += [prev_k, prev_v]
    return pl.pallas_call(
        functools.partial(_sample_attn_kernel, t_len=t_len, rb=rb, aliased=aliased),
        grid=(b, l_buf // rb),
        in_specs=in_specs,
        out_specs=[pl.BlockSpec((None, t_len, ATT_W), lambda bi, j: (bi, 0, 0)), blk_spec, blk_spec],
        out_shape=[
            jax.ShapeDtypeStruct((b, t_len, ATT_W), F32),
            jax.ShapeDtypeStruct(cache_k.shape, F32),
            jax.ShapeDtypeStruct(cache_v.shape, F32),
        ],
        scratch_shapes=[
            pltpu.VMEM((rows, HEAD_DIM), F32),
            pltpu.VMEM((t_len, N_ATT_HEADS, HEAD_DIM), F32), pltpu.VMEM((t_len, N_ATT_HEADS, HEAD_DIM), F32),
            pltpu.VMEM((rows, 1), F32), pltpu.VMEM((rows, 1), F32), pltpu.VMEM((rows, HEAD_DIM), F32),
        ],
        input_output_aliases={9: 1, 10: 2} if aliased else {},
        compiler_params=pltpu.CompilerParams(
            dimension_semantics=("arbitrary", "arbitrary"), vmem_limit_bytes=VMEM_LIMIT),
        name="sample_attn",
    )(*args)


def _silu(x):
    return x * jax.nn.sigmoid(x)


def _mix_kernel(x_ref, a_ref, ga_ref, uv_ref, ug_ref, gc_ref, qm_ref, gm_ref, cinit_ref, mk_ref, mv_ref,
                wdw_ref, bdw_ref, lng_ref, lnb_ref, wpw_ref, wout_ref, gpost_ref,
                xo_ref, cs_ref, uext_ref, mix_ref, *, tile, conv_chunk):
    @pl.when(pl.program_id(1) == 0)
    def _():
        uext_ref[0:HALO, :] = cinit_ref[...]

    uext_ref[HALO:HALO + tile, :] = uv_ref[...] * jax.nn.sigmoid(ug_ref[...])

    first = HALO - (CONV_K - 1)
    chunks = []
    for c0 in range(0, tile, conv_chunk):
        acc = jnp.broadcast_to(bdw_ref[...], (conv_chunk, CONV_CH))
        for phase in range(SUBLANES):
            rows = conv_chunk + (SUBLANES if phase else 0)
            part = None
            for k in range(CONV_K):
                if (first + k) % SUBLANES == phase:
                    base = c0 + first + k - phase
                    term = uext_ref[base:base + rows, :] * wdw_ref[k:k + 1, :]
                    part = term if part is None else part + term
            if part is not None:
                acc = acc + part[phase:phase + conv_chunk, :]
        chunks.append(acc)
    c = jnp.concatenate(chunks, axis=0) if len(chunks) > 1 else chunks[0]

    cs_ref[...] = uext_ref[tile + first:tile + HALO, :]
    tail = uext_ref[tile:tile + HALO, :]
    uext_ref[0:HALO, :] = tail

    mu = jnp.mean(c, axis=-1, keepdims=True)
    var = jnp.mean(jnp.square(c - mu), axis=-1, keepdims=True)
    c = (c - mu) * lax.rsqrt(var + EPS) * lng_ref[...] + lnb_ref[...]
    c = jnp.dot(_silu(c).astype(BF16), wpw_ref[...], preferred_element_type=F32)

    mix_ref[:, 0:ATT_W] = (a_ref[...] * _silu(ga_ref[...])).astype(BF16)
    mix_ref[:, ATT_W:ATT_W + CONV_CH] = (c * _silu(gc_ref[...])).astype(BF16)

    def mem_head(ref, h):
        if len(ref.shape) == 3:
            return ref[:, h, :].astype(BF16)
        return ref[:, h * HEAD_DIM:(h + 1) * HEAD_DIM].astype(BF16)

    for h in range(N_X_HEADS):
        cols = slice(h * HEAD_DIM, (h + 1) * HEAD_DIM)
        q = (qm_ref[:, cols] * SCALE).astype(BF16)
        s = _qk(q, mem_head(mk_ref, h))
        p = jnp.exp(s - jnp.max(s, axis=-1, keepdims=True))
        o = jnp.dot(p.astype(BF16), mem_head(mv_ref, h), preferred_element_type=F32)
        o = o / jnp.sum(p, axis=-1, keepdims=True)
        off = ATT_W + CONV_CH + h * HEAD_DIM
        mix_ref[:, off:off + HEAD_DIM] = (o * _silu(gm_ref[:, cols])).astype(BF16)

    y = jnp.dot(mix_ref[...], wout_ref[...], preferred_element_type=F32)
    y = y * lax.rsqrt(jnp.mean(y * y, axis=-1, keepdims=True) + EPS) * gpost_ref[...]
    xo_ref[...] = x_ref[...] + y


def _mix(x, z, a, cinit, mk_arr, mk_spec, mv_arr, mv_spec, w_dw, b_dw, ln_g, ln_b, w_pw2, w_out, g_post,
         *, li, tile):
    b, s, _ = x.shape
    conv_chunk = min(tile, 32)

    def zspec(cb, width=COL_BLK):
        return pl.BlockSpec((None, tile, width), lambda bi, t: (bi, t, cb * COL_BLK // width))

    def const(shape):
        return pl.BlockSpec(shape, lambda bi, t: (0,) * len(shape))

    def layer(shape):
        return pl.BlockSpec((None,) + shape, lambda bi, t: (li,) + (0,) * len(shape))

    return pl.pallas_call(
        functools.partial(_mix_kernel, tile=tile, conv_chunk=conv_chunk),
        grid=(b, s // tile),
        in_specs=[
            pl.BlockSpec((None, tile, D_MODEL), lambda bi, t: (bi, t, 0)),
            pl.BlockSpec((None, tile, ATT_W), lambda bi, t: (bi, t, 0)),
            zspec(CB_GA, ATT_W), zspec(CB_UV), zspec(CB_UG), zspec(CB_GC), zspec(CB_QM), zspec(CB_GM),
            pl.BlockSpec((None, HALO, CONV_CH), lambda bi, t: (bi, 0, 0)),
            mk_spec, mv_spec,
            const((CONV_K, CONV_CH)), const((1, CONV_CH)), const((1, CONV_CH)), const((1, CONV_CH)),
            layer((CONV_CH, CONV_CH)), layer((MIX_W, D_MODEL)), const((1, D_MODEL)),
        ],
        out_specs=[
            pl.BlockSpec((None, tile, D_MODEL), lambda bi, t: (bi, t, 0)),
            pl.BlockSpec((None, CONV_K - 1, CONV_CH), lambda bi, t: (bi, 0, 0)),
        ],
        out_shape=[
            jax.ShapeDtypeStruct((b, s, D_MODEL), F32),
            jax.ShapeDtypeStruct((b, CONV_K - 1, CONV_CH), F32),
        ],
        scratch_shapes=[pltpu.VMEM((HALO + tile, CONV_CH), F32), pltpu.VMEM((tile, MIX_W), BF16)],
        compiler_params=pltpu.CompilerParams(
            dimension_semantics=("parallel", "arbitrary"), vmem_limit_bytes=VMEM_LIMIT),
        name="mix",
    )(x, a, z, z, z, z, z, z, cinit, mk_arr, mv_arr,
      w_dw, b_dw.reshape(1, -1), ln_g.reshape(1, -1), ln_b.reshape(1, -1), w_pw2, w_out,
      g_post.reshape(1, -1))


def kernel(x_prompt, x_sample, mem_prompt, cache_attn_k, cache_attn_v, state_conv, cache_mem_k, cache_mem_v,
           rel_bias, norm_pre_g, w_in, w_dw, b_dw, ln_conv_g, ln_conv_b, w_pw2, w_mem_kv, w_out, norm_post_g):
    bp, s_len, _ = x_prompt.shape
    bs, t_len, _ = x_sample.shape
    l_buf = cache_attn_k.shape[2]
    l_prompt = min(WIN, s_len)

    w_in_b, w_pw2_b = w_in.astype(BF16), w_pw2.astype(BF16)
    w_mem_b, w_out_b = w_mem_kv.astype(BF16), w_out.astype(BF16)

    tab_near = _distance_table(rel_bias, np.arange(NEAR_REACH + 1), NEAR_PATTERNS)
    tab_far = _distance_table(rel_bias, FAR_D * np.arange(FAR_KEYS + 1), DIL_PATTERNS[-1:])
    bias_near = _toeplitz(tab_near, NEAR_REACH, Q_BLK, NEAR_KBLK * Q_BLK)
    bias_far = _toeplitz(tab_far, FAR_KEYS, Q_BLK, 2 * Q_BLK)
    tab_all = _distance_table(rel_bias, np.arange(WIN + 1), DIL_PATTERNS)
    bias_s = _head_matched(_toeplitz(tab_all, l_buf, t_len, l_buf))
    bias_s_new = _head_matched(_toeplitz(tab_all, 0, t_len, t_len))

    mem2d = mem_prompt.reshape(bp * N_MEM, D_MODEL)
    ones_g = jnp.ones((D_MODEL,), F32)
    cinit_p = jnp.zeros((bp, HALO, CONV_CH), F32)
    pad_s = jnp.zeros((DEPTH, bs, HALO - (CONV_K - 1), CONV_CH), F32)
    cinit_s = jnp.concatenate([pad_s, state_conv], axis=2)
    xp, xs = x_prompt, x_sample
    akp = avp = aks = avs = None
    cvp, mkp, mvp, cvs = [], [], [], []
    for li in range(DEPTH):
        wargs = (w_dw[li], b_dw[li], ln_conv_g[li], ln_conv_b[li], w_pw2_b, w_out_b, norm_post_g[li])

        zp, akp, avp = _proj_kv(xp.reshape(bp * s_len, D_MODEL), norm_pre_g[li], w_in_b, akp, avp,
                                li=li, seq=s_len, l_keep=l_prompt)
        zp = zp.reshape(bp, s_len, IN_W)
        a = _prompt_attn(zp, bias_near, bias_far)
        mkv = _proj(mem2d, ones_g, w_mem_b, li=li, normalize=False, bm=bp * N_MEM, bn=COL_BLK)
        mkv = mkv.reshape(bp, N_MEM, 2 * X_W)
        mk_spec, mv_spec = (pl.BlockSpec((None, N_MEM, X_W), lambda bi, t, half=half: (bi, 0, half))
                            for half in (0, 1))
        xp, cst = _mix(xp, zp, a, cinit_p, mkv, mk_spec, mkv, mv_spec, *wargs, li=li, tile=256)
        cvp.append(cst)
        mkp.append(mkv[:, :, :X_W].reshape(bp, N_MEM, N_X_HEADS, HEAD_DIM))
        mvp.append(mkv[:, :, X_W:].reshape(bp, N_MEM, N_X_HEADS, HEAD_DIM))

        zs = _proj(xs.reshape(bs * t_len, D_MODEL), norm_pre_g[li], w_in_b, li=li,
                   normalize=True, bm=bs * t_len, bn=COL_BLK).reshape(bs, t_len, IN_W)
        a, aks, avs = _sample_attn(zs, cache_attn_k, cache_attn_v, bias_s, bias_s_new, aks, avs, li=li)
        mem_spec = pl.BlockSpec((None, None, N_MEM, N_X_HEADS, HEAD_DIM),
                                lambda bi, t, li=li: (li, bi, 0, 0, 0))
        xs, cst = _mix(xs, zs, a, cinit_s[li], cache_mem_k, mem_spec, cache_mem_v, mem_spec, *wargs,
                       li=li, tile=t_len)
        cvs.append(cst)

    return (xp, xs, akp, avp, jnp.stack(cvp), jnp.stack(mkp), jnp.stack(mvp), aks, avs, jnp.stack(cvs))
```

```python
import functools

import numpy as np
import jax
import jax.numpy as jnp
from jax import lax
from jax.experimental import pallas as pl
from jax.experimental.pallas import tpu as pltpu

D_MODEL = 2048
DEPTH = 4
N_MEM = 256
HEAD_DIM = 128
ATT_W = D_MODEL // 2
N_ATT_HEADS = ATT_W // HEAD_DIM
DIL_PATTERNS = ((128, 1), (512, 4), (2048, 16))
WIN = max(w for w, _ in DIL_PATTERNS)
N_BUCKETS = 32
MAX_DIST = WIN
CONV_CH = D_MODEL // 4
CONV_K = 31
X_W = D_MODEL // 4
N_X_HEADS = X_W // HEAD_DIM
MIX_W = ATT_W + CONV_CH + X_W
IN_W = 4 * ATT_W + 3 * CONV_CH + 2 * X_W
EPS = 1e-6
NEG = -1e30
SCALE = HEAD_DIM ** -0.5
LOG2E = 1.4426950408889634

COL_BLK = 512
CB_Q, CB_K, CB_V, CB_GA = 0, 2, 4, 6
CB_UV, CB_UG, CB_GC, CB_QM, CB_GM = 8, 9, 10, 11, 12
HEADS_PER_COL_BLK = COL_BLK // HEAD_DIM

Q_BLK = 128
FAR_W, FAR_D = DIL_PATTERNS[-1]
NEAR_PATTERNS = DIL_PATTERNS[:-1]
NEAR_REACH = max(w for w, _ in NEAR_PATTERNS)
NEAR_KBLK = NEAR_REACH // Q_BLK + 1
FAR_KEYS = FAR_W // FAR_D
FAR_STEP = 4
assert FAR_KEYS == Q_BLK and NEAR_REACH % Q_BLK == 0 and FAR_STEP * FAR_STEP == FAR_D
FAR_GROUP = 4
NEAR_GROUP = 7
PROJ_BM = 1024
SAMPLE_RB = 1024
HALO = 32
SUBLANES = 8
VMEM_LIMIT = 56 * 1024 * 1024

BF16 = jnp.bfloat16
F32 = jnp.float32


def _t5_bucket(dist):
    dist = np.asarray(dist)
    max_exact = N_BUCKETS // 2
    large = max_exact + (np.log(np.maximum(dist, 1) / max_exact)
                         / np.log(MAX_DIST / max_exact) * (N_BUCKETS - max_exact)).astype(np.int32)
    large = np.minimum(large, N_BUCKETS - 1)
    return np.where(dist < max_exact, dist, large).astype(np.int32)


def _distance_table(rel_bias, dists, patterns):
    dists = np.asarray(dists)
    cnt = np.zeros(dists.shape, np.int64)
    for w, dil in patterns:
        cnt += ((dists % dil == 0) & (dists <= w)).astype(np.int64)
    logc = jnp.log(jnp.asarray(np.maximum(cnt, 1), F32))
    tab = rel_bias[_t5_bucket(dists)].T.astype(F32) + logc[None, :]
    tab = jnp.where(jnp.asarray(cnt > 0)[None, :], tab, NEG)
    return jnp.concatenate([tab, jnp.full((tab.shape[0], 1), NEG, F32)], axis=1)


def _toeplitz(tab, off, rows, cols):
    n_valid = tab.shape[1] - 1
    lp = rows + cols
    j = np.concatenate([np.arange(cols), np.full(lp - (rows + cols - 1), off + n_valid),
                        np.arange(-(rows - 1), 0)])
    d = off - j
    idx = np.where((d >= 0) & (d < n_valid), d, n_valid)
    v = jnp.take(tab, jnp.asarray(idx, jnp.int32), axis=1)
    flat = jnp.tile(v, (1, rows))[:, :rows * (lp - 1)]
    return flat.reshape(tab.shape[0], rows, lp - 1)[:, :, :cols]


def _head_matched_bias(tab, off, t_len, n_rows):
    n_h, n_valid = tab.shape[0], tab.shape[1] - 1
    n_cols = n_rows * n_h
    gaps = jnp.full((n_h, n_valid, n_h - 1), NEG, F32)
    s = jnp.concatenate([tab[:, :n_valid, None], gaps], axis=2).reshape(n_h, n_valid * n_h)
    k_max = n_h * (off + t_len - 1) + n_h - 1
    pad_l = max(0, n_cols - 1 - n_h * off)
    pad_r = max(0, k_max - (n_valid * n_h - 1))
    s = jnp.concatenate([jnp.full((n_h, pad_l), NEG, F32), s, jnp.full((n_h, pad_r), NEG, F32)], axis=1)
    rev = s[:, ::-1]
    n_s = s.shape[1]
    rows = []
    for h in range(n_h):
        for t in range(t_len):
            start = n_s - 1 - (n_h * (off + t) + h) - pad_l
            rows.append(rev[h, start:start + n_cols])
    return jnp.stack(rows, axis=0)


def _norm_rows(x_ref, g_ref, h_ref, normalize):
    x = x_ref[...]
    if normalize:
        x = x * lax.rsqrt(jnp.mean(x * x, axis=-1, keepdims=True) + EPS) * g_ref[...]
    h_ref[...] = x.astype(BF16)


def _proj_kernel(x_ref, g_ref, w_ref, o_ref, h_ref, *, normalize):
    @pl.when(pl.program_id(1) == 0)
    def _():
        _norm_rows(x_ref, g_ref, h_ref, normalize)

    o_ref[...] = jnp.dot(h_ref[...], w_ref[...], preferred_element_type=F32)


def _proj(x, g, w, *, li, normalize, bm, bn):
    m, d = x.shape
    n = w.shape[2]
    return pl.pallas_call(
        functools.partial(_proj_kernel, normalize=normalize),
        grid=(m // bm, n // bn),
        in_specs=[
            pl.BlockSpec((bm, d), lambda i, j: (i, 0)),
            pl.BlockSpec((1, d), lambda i, j: (0, 0)),
            pl.BlockSpec((None, d, bn), lambda i, j: (li, 0, j)),
        ],
        out_specs=pl.BlockSpec((bm, bn), lambda i, j: (i, j)),
        out_shape=jax.ShapeDtypeStruct((m, n), F32),
        scratch_shapes=[pltpu.VMEM((bm, d), BF16)],
        compiler_params=pltpu.CompilerParams(
            dimension_semantics=("parallel", "arbitrary"), vmem_limit_bytes=VMEM_LIMIT),
        name="proj",
    )(x, g.reshape(1, d), w)


def _proj_kv_kernel(*refs, li, tiles_per_seq, tail_tiles, aliased, shift):
    x_ref, g_ref, w_ref, ck_hbm, cv_hbm = refs[:5]
    o_ref, ko_ref, vo_ref, sk_hbm, sv_hbm, h_ref, sem = refs[9:] if aliased else refs[5:]
    i, j = pl.program_id(0), pl.program_id(1)
    n_i, n_j = pl.num_programs(0), pl.num_programs(1)
    n_seq, l_buf = ck_hbm.shape[1], ck_hbm.shape[2]

    def cache_copies():
        for which, (src, dst) in enumerate(((ck_hbm, sk_hbm), (cv_hbm, sv_hbm))):
            for b in range(n_seq):
                yield pltpu.make_async_copy(src.at[li, b, pl.ds(shift, l_buf - shift)],
                                            dst.at[li, b, pl.ds(0, l_buf - shift)], sem.at[which, b])

    @pl.when((i == 0) & (j == 0))
    def _():
        for cp in cache_copies():
            cp.start()

    @pl.when((i == n_i - 1) & (j == n_j - 1))
    def _():
        for cp in cache_copies():
            cp.wait()

    @pl.when(j == 0)
    def _():
        _norm_rows(x_ref, g_ref, h_ref, True)

    o_ref[...] = jnp.dot(h_ref[...], w_ref[...], preferred_element_type=F32)

    in_tail = i % tiles_per_seq >= tiles_per_seq - tail_tiles
    for cb0, dst in ((CB_K, ko_ref), (CB_V, vo_ref)):
        for cb in range(cb0, cb0 + ATT_W // COL_BLK):
            @pl.when(in_tail & (j == cb))
            def _(cb=cb, cb0=cb0, dst=dst):
                for hh in range(HEADS_PER_COL_BLK):
                    head = (cb - cb0) * HEADS_PER_COL_BLK + hh
                    dst[:, head, :] = o_ref[:, hh * HEAD_DIM:(hh + 1) * HEAD_DIM]


def _proj_kv(x, g, w, cache_k, cache_v, prev, *, li, seq, l_keep, shift):
    m, d = x.shape
    n = w.shape[2]
    bsz = m // seq
    bm, bn = PROJ_BM, COL_BLK
    tiles_per_seq, tail_tiles = seq // bm, l_keep // bm
    assert seq % bm == 0 and l_keep % bm == 0
    aliased = prev is not None
    kv_shape = jax.ShapeDtypeStruct((DEPTH, bsz, l_keep, N_ATT_HEADS, HEAD_DIM), F32)
    cache_shape = jax.ShapeDtypeStruct(cache_k.shape, F32)
    any_spec = pl.BlockSpec(memory_space=pl.ANY)

    def kv_map(i, j):
        return (li, i // tiles_per_seq, jnp.maximum(i % tiles_per_seq - (tiles_per_seq - tail_tiles), 0), 0, 0)

    kv_spec = pl.BlockSpec((None, None, bm, N_ATT_HEADS, HEAD_DIM), kv_map)
    in_specs = [
        pl.BlockSpec((bm, d), lambda i, j: (i, 0)),
        pl.BlockSpec((1, d), lambda i, j: (0, 0)),
        pl.BlockSpec((None, d, bn), lambda i, j: (li, 0, j)),
        any_spec, any_spec,
    ]
    args = [x, g.reshape(1, d), w, cache_k, cache_v]
    if aliased:
        in_specs += [any_spec] * 4
        args += list(prev)
    return pl.pallas_call(
        functools.partial(_proj_kv_kernel, li=li, tiles_per_seq=tiles_per_seq, tail_tiles=tail_tiles,
                          aliased=aliased, shift=shift),
        grid=(m // bm, n // bn),
        in_specs=in_specs,
        out_specs=[pl.BlockSpec((bm, bn), lambda i, j: (i, j)), kv_spec, kv_spec, any_spec, any_spec],
        out_shape=[jax.ShapeDtypeStruct((m, n), F32), kv_shape, kv_shape, cache_shape, cache_shape],
        scratch_shapes=[pltpu.VMEM((bm, d), BF16), pltpu.SemaphoreType.DMA((2, cache_k.shape[1]))],
        input_output_aliases={5: 1, 6: 2, 7: 3, 8: 4} if aliased else {},
        compiler_params=pltpu.CompilerParams(
            dimension_semantics=("arbitrary", "arbitrary"), vmem_limit_bytes=VMEM_LIMIT),
        name="proj_kv",
    )(*args)


def _qk(q, k):
    return lax.dot_general(q, k, (((1,), (1,)), ((), ())), preferred_element_type=F32)


def _aligned(x, m):
    return x if isinstance(x, int) else pl.multiple_of(x, m)


def _prompt_attn_kernel(q_ref, k_ref, v_ref, bn_ref, bf_ref, o_ref,
                        qb_ref, kb_ref, vb_ref, qf_ref, kf_ref, vf_ref, of_ref, lf_ref, tmp_ref, *, seq):
    sub = seq // FAR_D
    chunk = 2 * Q_BLK
    val = slice(0, HEAD_DIM)

    def cast_body(c, carry):
        r0 = pl.multiple_of(c * chunk, chunk)
        rows = pl.ds(r0, chunk)
        qb_ref[rows, :] = (q_ref[rows, :] * (SCALE * LOG2E)).astype(BF16)
        kb_ref[rows, :] = k_ref[rows, :].astype(BF16)
        vb_ref[rows, val] = v_ref[rows, :].astype(BF16)
        vb_ref[rows, HEAD_DIM:] = jnp.ones((chunk, HEAD_DIM), BF16)
        vf_ref[rows, HEAD_DIM:] = jnp.ones((chunk, HEAD_DIM), BF16)
        return carry

    lax.fori_loop(0, seq // chunk, cast_body, 0)

    quarter = seq // FAR_STEP
    for src_ref, dst_ref, scale in ((q_ref, qf_ref, SCALE * LOG2E), (k_ref, kf_ref, None),
                                    (v_ref, vf_ref, None)):
        for r1 in range(FAR_STEP):
            tmp_ref[r1 * quarter:(r1 + 1) * quarter, :] = src_ref[pl.ds(r1, quarter, stride=FAR_STEP), :]
        for r1 in range(FAR_STEP):
            for r2 in range(FAR_STEP):
                r = FAR_STEP * r2 + r1
                x = tmp_ref[pl.ds(r1 * quarter + r2, sub, stride=FAR_STEP), :]
                if scale is not None:
                    x = x * scale
                dst_ref[r * sub:(r + 1) * sub, val] = x.astype(BF16)

    def far_tiles(tiles):
        geo = []
        for r, a_blk in tiles:
            n_kt = min(a_blk, 1) + 1
            q0 = pl.multiple_of(r * sub + a_blk * Q_BLK, Q_BLK)
            k0 = pl.multiple_of(r * sub + (a_blk + 1 - n_kt) * Q_BLK, Q_BLK)
            dst = pl.ds(a_blk * Q_BLK * FAR_D + r, Q_BLK, stride=FAR_D)
            geo.append((pl.ds(q0, Q_BLK), pl.ds(k0, n_kt * Q_BLK), n_kt, dst))
        scores = [_qk(qf_ref[rows, :], kf_ref[keys, :]) + bf_ref[:, (2 - n_kt) * Q_BLK:]
                  for rows, keys, n_kt, _ in geo]
        probs = []
        for s in scores:
            m = jnp.max(s, axis=-1, keepdims=True)
            probs.append((jnp.exp2(s - m).astype(BF16), m))
        for (_, keys, _, dst), (p, m) in zip(geo, probs):
            o = jnp.dot(p, vf_ref[keys, :], preferred_element_type=F32)
            l = o[:, HEAD_DIM:]
            of_ref[dst, :] = o[:, val] / l
            lf_ref[dst, :] = m + jnp.log2(l)

    def far_body(g, carry):
        far_tiles([(g * FAR_GROUP + r_off, a_blk)
                   for r_off in range(FAR_GROUP) for a_blk in range(sub // Q_BLK)])
        return carry

    lax.fori_loop(0, FAR_D // FAR_GROUP, far_body, 0)

    def near_tiles(tiles):
        geo = []
        for i, n_kt in tiles:
            q0 = _aligned(i * Q_BLK, Q_BLK)
            k0 = _aligned((i + 1 - n_kt) * Q_BLK, Q_BLK)
            geo.append((pl.ds(q0, Q_BLK), pl.ds(k0, n_kt * Q_BLK), n_kt))
        scores = [_qk(qb_ref[rows, :], kb_ref[keys, :]) + bn_ref[:, (NEAR_KBLK - n_kt) * Q_BLK:]
                  for rows, keys, n_kt in geo]
        probs = []
        for (rows, _, _), s in zip(geo, scores):
            m_far = lf_ref[rows, 0:1]
            m = jnp.maximum(jnp.max(s, axis=-1, keepdims=True), m_far)
            probs.append((jnp.exp2(s - m).astype(BF16), jnp.exp2(m_far - m)))
        for (rows, keys, _), (p, w_far) in zip(geo, probs):
            o = jnp.dot(p, vb_ref[keys, :], preferred_element_type=F32)
            acc = w_far * of_ref[rows, :] + o[:, val]
            o_ref[rows, :] = acc / (w_far + o[:, HEAD_DIM:])

    n_blk = seq // Q_BLK
    n_edge = NEAR_KBLK - 1
    assert n_blk > n_edge and (n_blk - n_edge) % NEAR_GROUP == 0
    near_tiles([(i, i + 1) for i in range(n_edge)])

    def near_body(g, carry):
        near_tiles([(n_edge + g * NEAR_GROUP + off, NEAR_KBLK) for off in range(NEAR_GROUP)])
        return carry

    lax.fori_loop(0, (n_blk - n_edge) // NEAR_GROUP, near_body, 0)


def _prompt_attn(z, bias_near, bias_far):
    b, s, _ = z.shape
    assert s % (FAR_D * Q_BLK) == 0

    def head_cols(cb):
        return pl.BlockSpec((None, s, HEAD_DIM), lambda bi, h: (bi, 0, cb * HEADS_PER_COL_BLK + h))

    return pl.pallas_call(
        functools.partial(_prompt_attn_kernel, seq=s),
        grid=(b, N_ATT_HEADS),
        in_specs=[
            head_cols(CB_Q), head_cols(CB_K), head_cols(CB_V),
            pl.BlockSpec((None, Q_BLK, NEAR_KBLK * Q_BLK), lambda bi, h: (h, 0, 0)),
            pl.BlockSpec((None, Q_BLK, 2 * Q_BLK), lambda bi, h: (h, 0, 0)),
        ],
        out_specs=pl.BlockSpec((None, s, HEAD_DIM), lambda bi, h: (bi, 0, h)),
        out_shape=jax.ShapeDtypeStruct((b, s, ATT_W), F32),
        scratch_shapes=[pltpu.VMEM((s, HEAD_DIM), BF16), pltpu.VMEM((s, HEAD_DIM), BF16),
                        pltpu.VMEM((s, 2 * HEAD_DIM), BF16)] * 2 + [pltpu.VMEM((s, HEAD_DIM), F32)] * 3,
        compiler_params=pltpu.CompilerParams(
            dimension_semantics=("parallel", "parallel"), vmem_limit_bytes=VMEM_LIMIT),
        name="prompt_attn",
    )(z, z, z, bias_near, bias_far)


def _sample_attn_kernel(q_ref, kn_ref, vn_ref, kc_ref, vc_ref, bias_ref, biasn_ref, sk_hbm, sv_hbm,
                        o_ref, ko_ref, vo_ref, qa_ref, kn3_ref, vn3_ref, m_ref, l_ref, acc_ref, *, t_len, rb):
    del sk_hbm, sv_hbm
    j = pl.program_id(1)
    last = pl.num_programs(1) - 1
    n_heads = N_ATT_HEADS

    @pl.when(j == 0)
    def _():
        for h in range(n_heads):
            cols = slice(h * HEAD_DIM, (h + 1) * HEAD_DIM)
            qa_ref[h * t_len:(h + 1) * t_len, :] = q_ref[:, cols] * SCALE
            kn3_ref[:, h, :] = kn_ref[:, cols]
            vn3_ref[:, h, :] = vn_ref[:, cols]
        m_ref[...] = jnp.full(m_ref.shape, NEG, F32)
        l_ref[...] = jnp.zeros(l_ref.shape, F32)
        acc_ref[...] = jnp.zeros(acc_ref.shape, F32)

    qa = qa_ref[...].astype(BF16)

    def accumulate(k, v, bias):
        s = _qk(qa, k.astype(BF16)) + bias
        m_old = m_ref[...]
        m_new = jnp.maximum(m_old, jnp.max(s, axis=-1, keepdims=True))
        alpha = jnp.exp(m_old - m_new)
        p = jnp.exp(s - m_new)
        l_ref[...] = alpha * l_ref[...] + jnp.sum(p, axis=-1, keepdims=True)
        acc_ref[...] = alpha * acc_ref[...] + jnp.dot(p.astype(BF16), v.astype(BF16),
                                                      preferred_element_type=F32)
        m_ref[...] = m_new

    accumulate(kc_ref[...].reshape(rb * n_heads, HEAD_DIM), vc_ref[...].reshape(rb * n_heads, HEAD_DIM),
               bias_ref[...])

    @pl.when(j == last)
    def _():
        ko_ref[...] = kn3_ref[...]
        vo_ref[...] = vn3_ref[...]
        accumulate(kn3_ref[...].reshape(t_len * n_heads, HEAD_DIM),
                   vn3_ref[...].reshape(t_len * n_heads, HEAD_DIM), biasn_ref[...])
        o = acc_ref[...] / l_ref[...]
        for h in range(n_heads):
            o_ref[:, h * HEAD_DIM:(h + 1) * HEAD_DIM] = o[h * t_len:(h + 1) * t_len, :]


def _sample_attn(z, cache_k, cache_v, bias, bias_new, stack_k, stack_v, *, li):
    b, t_len, _ = z.shape
    l_buf = cache_k.shape[2]
    rb = SAMPLE_RB
    assert t_len == SUBLANES and l_buf % rb == 0
    rows = N_ATT_HEADS * t_len

    def zspec(cb):
        return pl.BlockSpec((None, t_len, ATT_W), lambda bi, j: (bi, 0, cb * COL_BLK // ATT_W))

    blk_spec = pl.BlockSpec((None, None, rb, N_ATT_HEADS, HEAD_DIM), lambda bi, j: (li, bi, j, 0, 0))
    new_spec = pl.BlockSpec((None, None, t_len, N_ATT_HEADS, HEAD_DIM),
                            lambda bi, j: (li, bi, l_buf // t_len - 1, 0, 0))
    any_spec = pl.BlockSpec(memory_space=pl.ANY)
    in_specs = [
        zspec(CB_Q), zspec(CB_K), zspec(CB_V), blk_spec, blk_spec,
        pl.BlockSpec((rows, rb * N_ATT_HEADS), lambda bi, j: (0, j)),
        pl.BlockSpec((rows, rows), lambda bi, j: (0, 0)),
        any_spec, any_spec,
    ]
    args = [z, z, z, cache_k, cache_v, bias, bias_new, stack_k, stack_v]
    return pl.pallas_call(
        functools.partial(_sample_attn_kernel, t_len=t_len, rb=rb),
        grid=(b, l_buf // rb),
        in_specs=in_specs,
        out_specs=[pl.BlockSpec((None, t_len, ATT_W), lambda bi, j: (bi, 0, 0)), new_spec, new_spec],
        out_shape=[
            jax.ShapeDtypeStruct((b, t_len, ATT_W), F32),
            jax.ShapeDtypeStruct(cache_k.shape, F32),
            jax.ShapeDtypeStruct(cache_v.shape, F32),
        ],
        scratch_shapes=[
            pltpu.VMEM((rows, HEAD_DIM), F32),
            pltpu.VMEM((t_len, N_ATT_HEADS, HEAD_DIM), F32), pltpu.VMEM((t_len, N_ATT_HEADS, HEAD_DIM), F32),
            pltpu.VMEM((rows, 1), F32), pltpu.VMEM((rows, 1), F32), pltpu.VMEM((rows, HEAD_DIM), F32),
        ],
        input_output_aliases={7: 1, 8: 2},
        compiler_params=pltpu.CompilerParams(
            dimension_semantics=("arbitrary", "arbitrary"), vmem_limit_bytes=VMEM_LIMIT),
        name="sample_attn",
    )(*args)


def _silu(x):
    return x * jax.nn.sigmoid(x)


def _mix_kernel(x_ref, a_ref, ga_ref, uv_ref, ug_ref, gc_ref, qm_ref, gm_ref, cinit_ref, mk_ref, mv_ref,
                wdw_ref, bdw_ref, lng_ref, lnb_ref, wpw_ref, wout_ref, gpost_ref,
                xo_ref, cs_ref, uext_ref, mix_ref, *, tile, conv_chunk):
    @pl.when(pl.program_id(1) == 0)
    def _():
        uext_ref[0:HALO, :] = cinit_ref[...]

    uext_ref[HALO:HALO + tile, :] = uv_ref[...] * jax.nn.sigmoid(ug_ref[...])

    first = HALO - (CONV_K - 1)
    chunks = []
    for c0 in range(0, tile, conv_chunk):
        acc = jnp.broadcast_to(bdw_ref[...], (conv_chunk, CONV_CH))
        for phase in range(SUBLANES):
            rows = conv_chunk + (SUBLANES if phase else 0)
            part = None
            for k in range(CONV_K):
                if (first + k) % SUBLANES == phase:
                    base = c0 + first + k - phase
                    term = uext_ref[base:base + rows, :] * wdw_ref[k:k + 1, :]
                    part = term if part is None else part + term
            if part is not None:
                acc = acc + part[phase:phase + conv_chunk, :]
        chunks.append(acc)
    c = jnp.concatenate(chunks, axis=0) if len(chunks) > 1 else chunks[0]

    cs_ref[...] = uext_ref[tile + first:tile + HALO, :]
    tail = uext_ref[tile:tile + HALO, :]
    uext_ref[0:HALO, :] = tail

    mu = jnp.mean(c, axis=-1, keepdims=True)
    var = jnp.mean(jnp.square(c - mu), axis=-1, keepdims=True)
    c = (c - mu) * lax.rsqrt(var + EPS) * lng_ref[...] + lnb_ref[...]
    c = jnp.dot(_silu(c).astype(BF16), wpw_ref[...], preferred_element_type=F32)

    mix_ref[:, 0:ATT_W] = (a_ref[...] * _silu(ga_ref[...])).astype(BF16)
    mix_ref[:, ATT_W:ATT_W + CONV_CH] = (c * _silu(gc_ref[...])).astype(BF16)

    def mem_head(ref, h):
        if len(ref.shape) == 3:
            return ref[:, h, :].astype(BF16)
        return ref[:, h * HEAD_DIM:(h + 1) * HEAD_DIM].astype(BF16)

    for h in range(N_X_HEADS):
        cols = slice(h * HEAD_DIM, (h + 1) * HEAD_DIM)
        q = (qm_ref[:, cols] * SCALE).astype(BF16)
        s = _qk(q, mem_head(mk_ref, h))
        p = jnp.exp(s - jnp.max(s, axis=-1, keepdims=True))
        o = jnp.dot(p.astype(BF16), mem_head(mv_ref, h), preferred_element_type=F32)
        o = o / jnp.sum(p, axis=-1, keepdims=True)
        off = ATT_W + CONV_CH + h * HEAD_DIM
        mix_ref[:, off:off + HEAD_DIM] = (o * _silu(gm_ref[:, cols])).astype(BF16)

    y = jnp.dot(mix_ref[...], wout_ref[...], preferred_element_type=F32)
    y = y * lax.rsqrt(jnp.mean(y * y, axis=-1, keepdims=True) + EPS) * gpost_ref[...]
    xo_ref[...] = x_ref[...] + y


def _mix(x, z, a, cinit, mk_arr, mk_block, mv_arr, mv_block, w_dw, b_dw, ln_g, ln_b, w_pw2, w_out, g_post,
         *, li, tile):
    b, s, _ = x.shape
    conv_chunk = min(tile, 32)

    def zspec(cb, width=COL_BLK):
        return pl.BlockSpec((None, tile, width), lambda bi, t: (bi, t, cb * COL_BLK // width))

    def const(shape):
        return pl.BlockSpec(shape, lambda bi, t: (0,) * len(shape))

    def layer(shape):
        return pl.BlockSpec((None,) + shape, lambda bi, t: (li,) + (0,) * len(shape))

    def mem(block):
        shape, index = block
        return pl.BlockSpec(shape, lambda bi, t: index(bi))

    return pl.pallas_call(
        functools.partial(_mix_kernel, tile=tile, conv_chunk=conv_chunk),
        grid=(b, s // tile),
        in_specs=[
            pl.BlockSpec((None, tile, D_MODEL), lambda bi, t: (bi, t, 0)),
            pl.BlockSpec((None, tile, ATT_W), lambda bi, t: (bi, t, 0)),
            zspec(CB_GA, ATT_W), zspec(CB_UV), zspec(CB_UG), zspec(CB_GC), zspec(CB_QM), zspec(CB_GM),
            pl.BlockSpec((None, HALO, CONV_CH), lambda bi, t: (bi, 0, 0)),
            mem(mk_block), mem(mv_block),
            const((CONV_K, CONV_CH)), const((1, CONV_CH)), const((1, CONV_CH)), const((1, CONV_CH)),
            layer((CONV_CH, CONV_CH)), layer((MIX_W, D_MODEL)), const((1, D_MODEL)),
        ],
        out_specs=[
            pl.BlockSpec((None, tile, D_MODEL), lambda bi, t: (bi, t, 0)),
            pl.BlockSpec((None, CONV_K - 1, CONV_CH), lambda bi, t: (bi, 0, 0)),
        ],
        out_shape=[
            jax.ShapeDtypeStruct((b, s, D_MODEL), F32),
            jax.ShapeDtypeStruct((b, CONV_K - 1, CONV_CH), F32),
        ],
        scratch_shapes=[pltpu.VMEM((HALO + tile, CONV_CH), F32), pltpu.VMEM((tile, MIX_W), BF16)],
        compiler_params=pltpu.CompilerParams(
            dimension_semantics=("parallel", "arbitrary"), vmem_limit_bytes=VMEM_LIMIT),
        name="mix",
    )(x, a, z, z, z, z, z, z, cinit, mk_arr, mv_arr,
      w_dw, b_dw.reshape(1, -1), ln_g.reshape(1, -1), ln_b.reshape(1, -1), w_pw2, w_out,
      g_post.reshape(1, -1))


def kernel(x_prompt, x_sample, mem_prompt, cache_attn_k, cache_attn_v, state_conv, cache_mem_k, cache_mem_v,
           rel_bias, norm_pre_g, w_in, w_dw, b_dw, ln_conv_g, ln_conv_b, w_pw2, w_mem_kv, w_out, norm_post_g):
    bp, s_len, _ = x_prompt.shape
    bs, t_len, _ = x_sample.shape
    l_buf = cache_attn_k.shape[2]
    l_prompt = min(WIN, s_len)

    w_in_b, w_pw2_b = w_in.astype(BF16), w_pw2.astype(BF16)
    w_mem_b, w_out_b = w_mem_kv.astype(BF16), w_out.astype(BF16)

    tab_near = _distance_table(rel_bias, np.arange(NEAR_REACH + 1), NEAR_PATTERNS)
    tab_far = _distance_table(rel_bias, FAR_D * np.arange(FAR_KEYS + 1), DIL_PATTERNS[-1:])
    bias_near = _toeplitz(tab_near * LOG2E, NEAR_REACH, Q_BLK, NEAR_KBLK * Q_BLK)
    bias_far = _toeplitz(tab_far * LOG2E, FAR_KEYS, Q_BLK, 2 * Q_BLK)
    tab_all = _distance_table(rel_bias, np.arange(WIN + 1), DIL_PATTERNS)
    bias_s = _head_matched_bias(tab_all, l_buf, t_len, l_buf)
    bias_s_new = _head_matched_bias(tab_all, 0, t_len, t_len)

    mem2d = mem_prompt.reshape(bp * N_MEM, D_MODEL)
    ones_g = jnp.ones((D_MODEL,), F32)
    cinit_p = jnp.zeros((bp, HALO, CONV_CH), F32)
    pad_s = jnp.zeros((DEPTH, bs, HALO - (CONV_K - 1), CONV_CH), F32)
    cinit_s = jnp.concatenate([pad_s, state_conv], axis=2)
    xp, xs = x_prompt, x_sample
    stacks = None
    cvp, mkp, mvp, cvs = [], [], [], []
    for li in range(DEPTH):
        wargs = (w_dw[li], b_dw[li], ln_conv_g[li], ln_conv_b[li], w_pw2_b, w_out_b, norm_post_g[li])

        zp, akp, avp, aks, avs = _proj_kv(
            xp.reshape(bp * s_len, D_MODEL), norm_pre_g[li], w_in_b, cache_attn_k, cache_attn_v, stacks,
            li=li, seq=s_len, l_keep=l_prompt, shift=t_len)
        zp = zp.reshape(bp, s_len, IN_W)
        a = _prompt_attn(zp, bias_near, bias_far)
        mkv = _proj(mem2d, ones_g, w_mem_b, li=li, normalize=False, bm=bp * N_MEM, bn=COL_BLK)
        mkv = mkv.reshape(bp, N_MEM, 2 * X_W)
        mk_block, mv_block = (((None, N_MEM, X_W), lambda bi, half=half: (bi, 0, half)) for half in (0, 1))
        xp, cst = _mix(xp, zp, a, cinit_p, mkv, mk_block, mkv, mv_block, *wargs, li=li, tile=256)
        cvp.append(cst)
        mkp.append(mkv[:, :, :X_W].reshape(bp, N_MEM, N_X_HEADS, HEAD_DIM))
        mvp.append(mkv[:, :, X_W:].reshape(bp, N_MEM, N_X_HEADS, HEAD_DIM))

        zs = _proj(xs.reshape(bs * t_len, D_MODEL), norm_pre_g[li], w_in_b, li=li,
                   normalize=True, bm=bs * t_len, bn=COL_BLK).reshape(bs, t_len, IN_W)
        a, aks, avs = _sample_attn(zs, cache_attn_k, cache_attn_v, bias_s, bias_s_new, aks, avs, li=li)
        stacks = (akp, avp, aks, avs)
        mem_block = ((None, None, N_MEM, N_X_HEADS, HEAD_DIM), lambda bi, li=li: (li, bi, 0, 0, 0))
        xs, cst = _mix(xs, zs, a, cinit_s[li], cache_mem_k, mem_block, cache_mem_v, mem_block, *wargs,
                       li=li, tile=t_len)
        cvs.append(cst)

    return (xp, xs, akp, avp, jnp.stack(cvp), jnp.stack(mkp), jnp.stack(mvp), aks, avs, jnp.stack(cvs))
```

```python
import functools

import numpy as np
import jax
import jax.numpy as jnp
from jax import lax
from jax.experimental import pallas as pl
from jax.experimental.pallas import tpu as pltpu

D_MODEL = 2048
DEPTH = 4
N_MEM = 256
HEAD_DIM = 128
ATT_W = D_MODEL // 2
N_ATT_HEADS = ATT_W // HEAD_DIM
DIL_PATTERNS = ((128, 1), (512, 4), (2048, 16))
WIN = max(w for w, _ in DIL_PATTERNS)
N_BUCKETS = 32
MAX_DIST = WIN
CONV_CH = D_MODEL // 4
CONV_K = 31
X_W = D_MODEL // 4
N_X_HEADS = X_W // HEAD_DIM
MIX_W = ATT_W + CONV_CH + X_W
IN_W = 4 * ATT_W + 3 * CONV_CH + 2 * X_W
EPS = 1e-6
NEG = -1e30
SCALE = HEAD_DIM ** -0.5
LOG2E = 1.4426950408889634

COL_BLK = 512
CB_Q, CB_K, CB_V, CB_GA = 0, 2, 4, 6
CB_UV, CB_UG, CB_GC, CB_QM, CB_GM = 8, 9, 10, 11, 12
HEADS_PER_COL_BLK = COL_BLK // HEAD_DIM

Q_BLK = 128
FAR_W, FAR_D = DIL_PATTERNS[-1]
NEAR_PATTERNS = DIL_PATTERNS[:-1]
NEAR_REACH = max(w for w, _ in NEAR_PATTERNS)
NEAR_KBLK = NEAR_REACH // Q_BLK + 1
FAR_KEYS = FAR_W // FAR_D
FAR_STEP = 4
assert FAR_KEYS == Q_BLK and NEAR_REACH % Q_BLK == 0 and FAR_STEP * FAR_STEP == FAR_D
FAR_GROUP = 4
NEAR_GROUP = 7
PROJ_BM = 1024
COPY_RB = 256
SAMPLE_RB = 1024
HALO = 32
SUBLANES = 8
VMEM_LIMIT = 56 * 1024 * 1024

BF16 = jnp.bfloat16
F32 = jnp.float32


def _t5_bucket(dist):
    dist = np.asarray(dist)
    max_exact = N_BUCKETS // 2
    large = max_exact + (np.log(np.maximum(dist, 1) / max_exact)
                         / np.log(MAX_DIST / max_exact) * (N_BUCKETS - max_exact)).astype(np.int32)
    large = np.minimum(large, N_BUCKETS - 1)
    return np.where(dist < max_exact, dist, large).astype(np.int32)


def _distance_table(rel_bias, dists, patterns):
    dists = np.asarray(dists)
    cnt = np.zeros(dists.shape, np.int64)
    for w, dil in patterns:
        cnt += ((dists % dil == 0) & (dists <= w)).astype(np.int64)
    logc = jnp.log(jnp.asarray(np.maximum(cnt, 1), F32))
    tab = rel_bias[_t5_bucket(dists)].T.astype(F32) + logc[None, :]
    tab = jnp.where(jnp.asarray(cnt > 0)[None, :], tab, NEG)
    return jnp.concatenate([tab, jnp.full((tab.shape[0], 1), NEG, F32)], axis=1)


def _toeplitz(tab, off, rows, cols):
    n_valid = tab.shape[1] - 1
    lp = rows + cols
    j = np.concatenate([np.arange(cols), np.full(lp - (rows + cols - 1), off + n_valid),
                        np.arange(-(rows - 1), 0)])
    d = off - j
    idx = np.where((d >= 0) & (d < n_valid), d, n_valid)
    v = jnp.take(tab, jnp.asarray(idx, jnp.int32), axis=1)
    flat = jnp.tile(v, (1, rows))[:, :rows * (lp - 1)]
    return flat.reshape(tab.shape[0], rows, lp - 1)[:, :, :cols]


def _head_matched_bias(tab, off, t_len, n_rows):
    n_h, n_valid = tab.shape[0], tab.shape[1] - 1
    n_cols = n_rows * n_h
    gaps = jnp.full((n_h, n_valid, n_h - 1), NEG, F32)
    s = jnp.concatenate([tab[:, :n_valid, None], gaps], axis=2).reshape(n_h, n_valid * n_h)
    k_max = n_h * (off + t_len - 1) + n_h - 1
    pad_l = max(0, n_cols - 1 - n_h * off)
    pad_r = max(0, k_max - (n_valid * n_h - 1))
    s = jnp.concatenate([jnp.full((n_h, pad_l), NEG, F32), s, jnp.full((n_h, pad_r), NEG, F32)], axis=1)
    rev = s[:, ::-1]
    n_s = s.shape[1]
    rows = []
    for h in range(n_h):
        for t in range(t_len):
            start = n_s - 1 - (n_h * (off + t) + h) - pad_l
            rows.append(rev[h, start:start + n_cols])
    return jnp.stack(rows, axis=0)


def _norm_rows(x_ref, g_ref, h_ref, normalize):
    x = x_ref[...]
    if normalize:
        x = x * lax.rsqrt(jnp.mean(x * x, axis=-1, keepdims=True) + EPS) * g_ref[...]
    h_ref[...] = x.astype(BF16)


def _proj_kernel(x_ref, g_ref, w_ref, o_ref, h_ref, *, normalize):
    @pl.when(pl.program_id(1) == 0)
    def _():
        _norm_rows(x_ref, g_ref, h_ref, normalize)

    o_ref[...] = jnp.dot(h_ref[...], w_ref[...], preferred_element_type=F32)


def _proj(x, g, w, *, li, normalize, bm, bn):
    m, d = x.shape
    n = w.shape[2]
    return pl.pallas_call(
        functools.partial(_proj_kernel, normalize=normalize),
        grid=(m // bm, n // bn),
        in_specs=[
            pl.BlockSpec((bm, d), lambda i, j: (i, 0)),
            pl.BlockSpec((1, d), lambda i, j: (0, 0)),
            pl.BlockSpec((None, d, bn), lambda i, j: (li, 0, j)),
        ],
        out_specs=pl.BlockSpec((bm, bn), lambda i, j: (i, j)),
        out_shape=jax.ShapeDtypeStruct((m, n), F32),
        scratch_shapes=[pltpu.VMEM((bm, d), BF16)],
        compiler_params=pltpu.CompilerParams(
            dimension_semantics=("parallel", "arbitrary"), vmem_limit_bytes=VMEM_LIMIT),
        name="proj",
    )(x, g.reshape(1, d), w)


def _proj_kv_kernel(*refs, tiles_per_seq, tail_tiles, aliased, shift, n_copy, v_start):
    x_ref, g_ref, w_ref, kc_ref, kx_ref, vc_ref, vx_ref = refs[:7]
    o_ref, ko_ref, vo_ref, sk_ref, sv_ref, h_ref = refs[11:] if aliased else refs[7:]
    i, j = pl.program_id(0), pl.program_id(1)
    step = i * pl.num_programs(1) + j
    rb = kc_ref.shape[0]

    for first, cache, nxt, out in ((0, kc_ref, kx_ref, sk_ref), (v_start, vc_ref, vx_ref, sv_ref)):
        @pl.when((step >= first) & (step < first + n_copy))
        def _(cache=cache, nxt=nxt, out=out):
            out[0:rb - shift] = cache[shift:rb]
            out[rb - shift:rb] = nxt[...]

    @pl.when(j == 0)
    def _():
        _norm_rows(x_ref, g_ref, h_ref, True)

    o_ref[...] = jnp.dot(h_ref[...], w_ref[...], preferred_element_type=F32)

    in_tail = i % tiles_per_seq >= tiles_per_seq - tail_tiles
    for cb0, dst in ((CB_K, ko_ref), (CB_V, vo_ref)):
        for cb in range(cb0, cb0 + ATT_W // COL_BLK):
            @pl.when(in_tail & (j == cb))
            def _(cb=cb, cb0=cb0, dst=dst):
                for hh in range(HEADS_PER_COL_BLK):
                    head = (cb - cb0) * HEADS_PER_COL_BLK + hh
                    dst[:, head, :] = o_ref[:, hh * HEAD_DIM:(hh + 1) * HEAD_DIM]


def _proj_kv(x, g, w, cache_k, cache_v, prev, *, li, seq, l_keep, shift):
    m, d = x.shape
    n = w.shape[2]
    bsz = m // seq
    bm, bn = PROJ_BM, COL_BLK
    tiles_per_seq, tail_tiles = seq // bm, l_keep // bm
    assert seq % bm == 0 and l_keep % bm == 0
    aliased = prev is not None
    kv_shape = jax.ShapeDtypeStruct((DEPTH, bsz, l_keep, N_ATT_HEADS, HEAD_DIM), F32)
    cache_shape = jax.ShapeDtypeStruct(cache_k.shape, F32)
    any_spec = pl.BlockSpec(memory_space=pl.ANY)
    n_steps = (m // bm) * (n // bn)
    n_seq, l_buf = cache_k.shape[1], cache_k.shape[2]
    rb = COPY_RB
    blks_per_seq = l_buf // rb
    n_copy = n_seq * blks_per_seq
    v_start = n_steps - n_copy
    assert l_buf % rb == 0 and rb % shift == 0 and 0 <= v_start

    def kv_map(i, j):
        return (li, i // tiles_per_seq, jnp.maximum(i % tiles_per_seq - (tiles_per_seq - tail_tiles), 0), 0, 0)

    def copy_specs(first):
        def blk(i, j):
            return jnp.clip(i * (n // bn) + j - first, 0, n_copy - 1)

        def main(i, j):
            c = blk(i, j)
            return (li, c // blks_per_seq, c % blks_per_seq, 0, 0)

        def nxt(i, j):
            c = jnp.minimum(blk(i, j) + 1, n_copy - 1)
            return (li, c // blks_per_seq, (c % blks_per_seq) * (rb // shift), 0, 0)

        return (pl.BlockSpec((None, None, rb, N_ATT_HEADS, HEAD_DIM), main),
                pl.BlockSpec((None, None, shift, N_ATT_HEADS, HEAD_DIM), nxt))

    k_main, k_next = copy_specs(0)
    v_main, v_next = copy_specs(v_start)
    kv_spec = pl.BlockSpec((None, None, bm, N_ATT_HEADS, HEAD_DIM), kv_map)
    in_specs = [
        pl.BlockSpec((bm, d), lambda i, j: (i, 0)),
        pl.BlockSpec((1, d), lambda i, j: (0, 0)),
        pl.BlockSpec((None, d, bn), lambda i, j: (li, 0, j)),
        k_main, k_next, v_main, v_next,
    ]
    args = [x, g.reshape(1, d), w, cache_k, cache_k, cache_v, cache_v]
    if aliased:
        in_specs += [any_spec] * 4
        args += list(prev)
    return pl.pallas_call(
        functools.partial(_proj_kv_kernel, tiles_per_seq=tiles_per_seq, tail_tiles=tail_tiles,
                          aliased=aliased, shift=shift, n_copy=n_copy, v_start=v_start),
        grid=(m // bm, n // bn),
        in_specs=in_specs,
        out_specs=[pl.BlockSpec((bm, bn), lambda i, j: (i, j)), kv_spec, kv_spec, k_main, v_main],
        out_shape=[jax.ShapeDtypeStruct((m, n), F32), kv_shape, kv_shape, cache_shape, cache_shape],
        scratch_shapes=[pltpu.VMEM((bm, d), BF16)],
        input_output_aliases={7: 1, 8: 2, 9: 3, 10: 4} if aliased else {},
        compiler_params=pltpu.CompilerParams(
            dimension_semantics=("arbitrary", "arbitrary"), vmem_limit_bytes=VMEM_LIMIT),
        name="proj_kv",
    )(*args)


def _qk(q, k):
    return lax.dot_general(q, k, (((1,), (1,)), ((), ())), preferred_element_type=F32)


def _aligned(x, m):
    return x if isinstance(x, int) else pl.multiple_of(x, m)


def _prompt_attn_kernel(q_ref, k_ref, v_ref, bn_ref, bf_ref, o_ref,
                        qb_ref, kb_ref, vb_ref, qf_ref, kf_ref, vf_ref, of_ref, lf_ref, tmp_ref, *, seq):
    sub = seq // FAR_D
    chunk = 2 * Q_BLK
    val = slice(0, HEAD_DIM)

    def cast_body(c, carry):
        r0 = pl.multiple_of(c * chunk, chunk)
        rows = pl.ds(r0, chunk)
        qb_ref[rows, :] = (q_ref[rows, :] * (SCALE * LOG2E)).astype(BF16)
        kb_ref[rows, :] = k_ref[rows, :].astype(BF16)
        vb_ref[rows, val] = v_ref[rows, :].astype(BF16)
        vb_ref[rows, HEAD_DIM:] = jnp.ones((chunk, HEAD_DIM), BF16)
        vf_ref[rows, HEAD_DIM:] = jnp.ones((chunk, HEAD_DIM), BF16)
        return carry

    lax.fori_loop(0, seq // chunk, cast_body, 0)

    quarter = seq // FAR_STEP
    for src_ref, dst_ref, scale in ((q_ref, qf_ref, SCALE * LOG2E), (k_ref, kf_ref, None),
                                    (v_ref, vf_ref, None)):
        for r1 in range(FAR_STEP):
            tmp_ref[r1 * quarter:(r1 + 1) * quarter, :] = src_ref[pl.ds(r1, quarter, stride=FAR_STEP), :]
        for r1 in range(FAR_STEP):
            for r2 in range(FAR_STEP):
                r = FAR_STEP * r2 + r1
                x = tmp_ref[pl.ds(r1 * quarter + r2, sub, stride=FAR_STEP), :]
                if scale is not None:
                    x = x * scale
                dst_ref[r * sub:(r + 1) * sub, val] = x.astype(BF16)

    def far_tiles(tiles):
        geo = []
        for r, a_blk in tiles:
            n_kt = min(a_blk, 1) + 1
            q0 = pl.multiple_of(r * sub + a_blk * Q_BLK, Q_BLK)
            k0 = pl.multiple_of(r * sub + (a_blk + 1 - n_kt) * Q_BLK, Q_BLK)
            dst = pl.ds(a_blk * Q_BLK * FAR_D + r, Q_BLK, stride=FAR_D)
            geo.append((pl.ds(q0, Q_BLK), pl.ds(k0, n_kt * Q_BLK), n_kt, dst))
        scores = [_qk(qf_ref[rows, :], kf_ref[keys, :]) + bf_ref[:, (2 - n_kt) * Q_BLK:]
                  for rows, keys, n_kt, _ in geo]
        probs = []
        for s in scores:
            m = jnp.max(s, axis=-1, keepdims=True)
            probs.append((jnp.exp2(s - m).astype(BF16), m))
        for (_, keys, _, dst), (p, m) in zip(geo, probs):
            o = jnp.dot(p, vf_ref[keys, :], preferred_element_type=F32)
            l = o[:, HEAD_DIM:]
            of_ref[dst, :] = o[:, val] / l
            lf_ref[dst, :] = m + jnp.log2(l)

    def far_body(g, carry):
        far_tiles([(g * FAR_GROUP + r_off, a_blk)
                   for r_off in range(FAR_GROUP) for a_blk in range(sub // Q_BLK)])
        return carry

    lax.fori_loop(0, FAR_D // FAR_GROUP, far_body, 0)

    def near_tiles(tiles):
        geo = []
        for i, n_kt in tiles:
            q0 = _aligned(i * Q_BLK, Q_BLK)
            k0 = _aligned((i + 1 - n_kt) * Q_BLK, Q_BLK)
            geo.append((pl.ds(q0, Q_BLK), pl.ds(k0, n_kt * Q_BLK), n_kt))
        scores = [_qk(qb_ref[rows, :], kb_ref[keys, :]) + bn_ref[:, (NEAR_KBLK - n_kt) * Q_BLK:]
                  for rows, keys, n_kt in geo]
        probs = []
        for (rows, _, _), s in zip(geo, scores):
            m_far = lf_ref[rows, 0:1]
            m = jnp.maximum(jnp.max(s, axis=-1, keepdims=True), m_far)
            probs.append((jnp.exp2(s - m).astype(BF16), jnp.exp2(m_far - m)))
        for (rows, keys, _), (p, w_far) in zip(geo, probs):
            o = jnp.dot(p, vb_ref[keys, :], preferred_element_type=F32)
            acc = w_far * of_ref[rows, :] + o[:, val]
            o_ref[rows, :] = acc / (w_far + o[:, HEAD_DIM:])

    n_blk = seq // Q_BLK
    n_edge = NEAR_KBLK - 1
    assert n_blk > n_edge and (n_blk - n_edge) % NEAR_GROUP == 0
    near_tiles([(i, i + 1) for i in range(n_edge)])

    def near_body(g, carry):
        near_tiles([(n_edge + g * NEAR_GROUP + off, NEAR_KBLK) for off in range(NEAR_GROUP)])
        return carry

    lax.fori_loop(0, (n_blk - n_edge) // NEAR_GROUP, near_body, 0)


def _prompt_attn(z, bias_near, bias_far):
    b, s, _ = z.shape
    assert s % (FAR_D * Q_BLK) == 0

    def head_cols(cb):
        return pl.BlockSpec((None, s, HEAD_DIM), lambda bi, h: (bi, 0, cb * HEADS_PER_COL_BLK + h))

    return pl.pallas_call(
        functools.partial(_prompt_attn_kernel, seq=s),
        grid=(b, N_ATT_HEADS),
        in_specs=[
            head_cols(CB_Q), head_cols(CB_K), head_cols(CB_V),
            pl.BlockSpec((None, Q_BLK, NEAR_KBLK * Q_BLK), lambda bi, h: (h, 0, 0)),
            pl.BlockSpec((None, Q_BLK, 2 * Q_BLK), lambda bi, h: (h, 0, 0)),
        ],
        out_specs=pl.BlockSpec((None, s, HEAD_DIM), lambda bi, h: (bi, 0, h)),
        out_shape=jax.ShapeDtypeStruct((b, s, ATT_W), F32),
        scratch_shapes=[pltpu.VMEM((s, HEAD_DIM), BF16), pltpu.VMEM((s, HEAD_DIM), BF16),
                        pltpu.VMEM((s, 2 * HEAD_DIM), BF16)] * 2 + [pltpu.VMEM((s, HEAD_DIM), F32)] * 3,
        compiler_params=pltpu.CompilerParams(
            dimension_semantics=("parallel", "parallel"), vmem_limit_bytes=VMEM_LIMIT),
        name="prompt_attn",
    )(z, z, z, bias_near, bias_far)


def _sample_attn_kernel(q_ref, kn_ref, vn_ref, kc_ref, vc_ref, bias_ref, biasn_ref, sk_hbm, sv_hbm,
                        o_ref, ko_ref, vo_ref, qa_ref, kn3_ref, vn3_ref, m_ref, l_ref, acc_ref, *, t_len, rb):
    del sk_hbm, sv_hbm
    j = pl.program_id(1)
    last = pl.num_programs(1) - 1
    n_heads = N_ATT_HEADS

    @pl.when(j == 0)
    def _():
        for h in range(n_heads):
            cols = slice(h * HEAD_DIM, (h + 1) * HEAD_DIM)
            qa_ref[h * t_len:(h + 1) * t_len, :] = q_ref[:, cols] * SCALE
            kn3_ref[:, h, :] = kn_ref[:, cols]
            vn3_ref[:, h, :] = vn_ref[:, cols]
        m_ref[...] = jnp.full(m_ref.shape, NEG, F32)
        l_ref[...] = jnp.zeros(l_ref.shape, F32)
        acc_ref[...] = jnp.zeros(acc_ref.shape, F32)

    qa = qa_ref[...].astype(BF16)

    def accumulate(k, v, bias):
        s = _qk(qa, k.astype(BF16)) + bias
        m_old = m_ref[...]
        m_new = jnp.maximum(m_old, jnp.max(s, axis=-1, keepdims=True))
        alpha = jnp.exp(m_old - m_new)
        p = jnp.exp(s - m_new)
        l_ref[...] = alpha * l_ref[...] + jnp.sum(p, axis=-1, keepdims=True)
        acc_ref[...] = alpha * acc_ref[...] + jnp.dot(p.astype(BF16), v.astype(BF16),
                                                      preferred_element_type=F32)
        m_ref[...] = m_new

    accumulate(kc_ref[...].reshape(rb * n_heads, HEAD_DIM), vc_ref[...].reshape(rb * n_heads, HEAD_DIM),
               bias_ref[...])

    @pl.when(j == last)
    def _():
        ko_ref[...] = kn3_ref[...]
        vo_ref[...] = vn3_ref[...]
        accumulate(kn3_ref[...].reshape(t_len * n_heads, HEAD_DIM),
                   vn3_ref[...].reshape(t_len * n_heads, HEAD_DIM), biasn_ref[...])
        o = acc_ref[...] / l_ref[...]
        for h in range(n_heads):
            o_ref[:, h * HEAD_DIM:(h + 1) * HEAD_DIM] = o[h * t_len:(h + 1) * t_len, :]


def _sample_attn(z, cache_k, cache_v, bias, bias_new, stack_k, stack_v, *, li):
    b, t_len, _ = z.shape
    l_buf = cache_k.shape[2]
    rb = SAMPLE_RB
    assert t_len == SUBLANES and l_buf % rb == 0
    rows = N_ATT_HEADS * t_len

    def zspec(cb):
        return pl.BlockSpec((None, t_len, ATT_W), lambda bi, j: (bi, 0, cb * COL_BLK // ATT_W))

    blk_spec = pl.BlockSpec((None, None, rb, N_ATT_HEADS, HEAD_DIM), lambda bi, j: (li, bi, j, 0, 0))
    new_spec = pl.BlockSpec((None, None, t_len, N_ATT_HEADS, HEAD_DIM),
                            lambda bi, j: (li, bi, l_buf // t_len - 1, 0, 0))
    any_spec = pl.BlockSpec(memory_space=pl.ANY)
    in_specs = [
        zspec(CB_Q), zspec(CB_K), zspec(CB_V), blk_spec, blk_spec,
        pl.BlockSpec((rows, rb * N_ATT_HEADS), lambda bi, j: (0, j)),
        pl.BlockSpec((rows, rows), lambda bi, j: (0, 0)),
        any_spec, any_spec,
    ]
    args = [z, z, z, cache_k, cache_v, bias, bias_new, stack_k, stack_v]
    return pl.pallas_call(
        functools.partial(_sample_attn_kernel, t_len=t_len, rb=rb),
        grid=(b, l_buf // rb),
        in_specs=in_specs,
        out_specs=[pl.BlockSpec((None, t_len, ATT_W), lambda bi, j: (bi, 0, 0)), new_spec, new_spec],
        out_shape=[
            jax.ShapeDtypeStruct((b, t_len, ATT_W), F32),
            jax.ShapeDtypeStruct(cache_k.shape, F32),
            jax.ShapeDtypeStruct(cache_v.shape, F32),
        ],
        scratch_shapes=[
            pltpu.VMEM((rows, HEAD_DIM), F32),
            pltpu.VMEM((t_len, N_ATT_HEADS, HEAD_DIM), F32), pltpu.VMEM((t_len, N_ATT_HEADS, HEAD_DIM), F32),
            pltpu.VMEM((rows, 1), F32), pltpu.VMEM((rows, 1), F32), pltpu.VMEM((rows, HEAD_DIM), F32),
        ],
        input_output_aliases={7: 1, 8: 2},
        compiler_params=pltpu.CompilerParams(
            dimension_semantics=("arbitrary", "arbitrary"), vmem_limit_bytes=VMEM_LIMIT),
        name="sample_attn",
    )(*args)


def _silu(x):
    return x * jax.nn.sigmoid(x)


def _mix_kernel(x_ref, a_ref, ga_ref, uv_ref, ug_ref, gc_ref, qm_ref, gm_ref, cinit_ref, mk_ref, mv_ref,
                wdw_ref, bdw_ref, lng_ref, lnb_ref, wpw_ref, wout_ref, gpost_ref,
                xo_ref, cs_ref, uext_ref, mix_ref, *, tile, conv_chunk):
    @pl.when(pl.program_id(1) == 0)
    def _():
        uext_ref[0:HALO, :] = cinit_ref[...]

    uext_ref[HALO:HALO + tile, :] = uv_ref[...] * jax.nn.sigmoid(ug_ref[...])

    first = HALO - (CONV_K - 1)
    chunks = []
    for c0 in range(0, tile, conv_chunk):
        acc = jnp.broadcast_to(bdw_ref[...], (conv_chunk, CONV_CH))
        for phase in range(SUBLANES):
            rows = conv_chunk + (SUBLANES if phase else 0)
            part = None
            for k in range(CONV_K):
                if (first + k) % SUBLANES == phase:
                    base = c0 + first + k - phase
                    term = uext_ref[base:base + rows, :] * wdw_ref[k:k + 1, :]
                    part = term if part is None else part + term
            if part is not None:
                acc = acc + part[phase:phase + conv_chunk, :]
        chunks.append(acc)
    c = jnp.concatenate(chunks, axis=0) if len(chunks) > 1 else chunks[0]

    cs_ref[...] = uext_ref[tile + first:tile + HALO, :]
    tail = uext_ref[tile:tile + HALO, :]
    uext_ref[0:HALO, :] = tail

    mu = jnp.mean(c, axis=-1, keepdims=True)
    var = jnp.mean(jnp.square(c - mu), axis=-1, keepdims=True)
    c = (c - mu) * lax.rsqrt(var + EPS) * lng_ref[...] + lnb_ref[...]
    c = jnp.dot(_silu(c).astype(BF16), wpw_ref[...], preferred_element_type=F32)

    mix_ref[:, 0:ATT_W] = (a_ref[...] * _silu(ga_ref[...])).astype(BF16)
    mix_ref[:, ATT_W:ATT_W + CONV_CH] = (c * _silu(gc_ref[...])).astype(BF16)

    def mem_head(ref, h):
        if len(ref.shape) == 3:
            return ref[:, h, :].astype(BF16)
        return ref[:, h * HEAD_DIM:(h + 1) * HEAD_DIM].astype(BF16)

    for h in range(N_X_HEADS):
        cols = slice(h * HEAD_DIM, (h + 1) * HEAD_DIM)
        q = (qm_ref[:, cols] * SCALE).astype(BF16)
        s = _qk(q, mem_head(mk_ref, h))
        p = jnp.exp(s - jnp.max(s, axis=-1, keepdims=True))
        o = jnp.dot(p.astype(BF16), mem_head(mv_ref, h), preferred_element_type=F32)
        o = o / jnp.sum(p, axis=-1, keepdims=True)
        off = ATT_W + CONV_CH + h * HEAD_DIM
        mix_ref[:, off:off + HEAD_DIM] = (o * _silu(gm_ref[:, cols])).astype(BF16)

    y = jnp.dot(mix_ref[...], wout_ref[...], preferred_element_type=F32)
    y = y * lax.rsqrt(jnp.mean(y * y, axis=-1, keepdims=True) + EPS) * gpost_ref[...]
    xo_ref[...] = x_ref[...] + y


def _mix(x, z, a, cinit, mk_arr, mk_block, mv_arr, mv_block, w_dw, b_dw, ln_g, ln_b, w_pw2, w_out, g_post,
         *, li, tile):
    b, s, _ = x.shape
    conv_chunk = min(tile, 32)

    def zspec(cb, width=COL_BLK):
        return pl.BlockSpec((None, tile, width), lambda bi, t: (bi, t, cb * COL_BLK // width))

    def const(shape):
        return pl.BlockSpec(shape, lambda bi, t: (0,) * len(shape))

    def layer(shape):
        return pl.BlockSpec((None,) + shape, lambda bi, t: (li,) + (0,) * len(shape))

    def mem(block):
        shape, index = block
        return pl.BlockSpec(shape, lambda bi, t: index(bi))

    return pl.pallas_call(
        functools.partial(_mix_kernel, tile=tile, conv_chunk=conv_chunk),
        grid=(b, s // tile),
        in_specs=[
            pl.BlockSpec((None, tile, D_MODEL), lambda bi, t: (bi, t, 0)),
            pl.BlockSpec((None, tile, ATT_W), lambda bi, t: (bi, t, 0)),
            zspec(CB_GA, ATT_W), zspec(CB_UV), zspec(CB_UG), zspec(CB_GC), zspec(CB_QM), zspec(CB_GM),
            pl.BlockSpec((None, HALO, CONV_CH), lambda bi, t: (bi, 0, 0)),
            mem(mk_block), mem(mv_block),
            const((CONV_K, CONV_CH)), const((1, CONV_CH)), const((1, CONV_CH)), const((1, CONV_CH)),
            layer((CONV_CH, CONV_CH)), layer((MIX_W, D_MODEL)), const((1, D_MODEL)),
        ],
        out_specs=[
            pl.BlockSpec((None, tile, D_MODEL), lambda bi, t: (bi, t, 0)),
            pl.BlockSpec((None, CONV_K - 1, CONV_CH), lambda bi, t: (bi, 0, 0)),
        ],
        out_shape=[
            jax.ShapeDtypeStruct((b, s, D_MODEL), F32),
            jax.ShapeDtypeStruct((b, CONV_K - 1, CONV_CH), F32),
        ],
        scratch_shapes=[pltpu.VMEM((HALO + tile, CONV_CH), F32), pltpu.VMEM((tile, MIX_W), BF16)],
        compiler_params=pltpu.CompilerParams(
            dimension_semantics=("parallel", "arbitrary"), vmem_limit_bytes=VMEM_LIMIT),
        name="mix",
    )(x, a, z, z, z, z, z, z, cinit, mk_arr, mv_arr,
      w_dw, b_dw.reshape(1, -1), ln_g.reshape(1, -1), ln_b.reshape(1, -1), w_pw2, w_out,
      g_post.reshape(1, -1))


def kernel(x_prompt, x_sample, mem_prompt, cache_attn_k, cache_attn_v, state_conv, cache_mem_k, cache_mem_v,
           rel_bias, norm_pre_g, w_in, w_dw, b_dw, ln_conv_g, ln_conv_b, w_pw2, w_mem_kv, w_out, norm_post_g):
    bp, s_len, _ = x_prompt.shape
    bs, t_len, _ = x_sample.shape
    l_buf = cache_attn_k.shape[2]
    l_prompt = min(WIN, s_len)

    w_in_b, w_pw2_b = w_in.astype(BF16), w_pw2.astype(BF16)
    w_mem_b, w_out_b = w_mem_kv.astype(BF16), w_out.astype(BF16)

    tab_near = _distance_table(rel_bias, np.arange(NEAR_REACH + 1), NEAR_PATTERNS)
    tab_far = _distance_table(rel_bias, FAR_D * np.arange(FAR_KEYS + 1), DIL_PATTERNS[-1:])
    bias_near = _toeplitz(tab_near * LOG2E, NEAR_REACH, Q_BLK, NEAR_KBLK * Q_BLK)
    bias_far = _toeplitz(tab_far * LOG2E, FAR_KEYS, Q_BLK, 2 * Q_BLK)
    tab_all = _distance_table(rel_bias, np.arange(WIN + 1), DIL_PATTERNS)
    bias_s = _head_matched_bias(tab_all, l_buf, t_len, l_buf)
    bias_s_new = _head_matched_bias(tab_all, 0, t_len, t_len)

    mem2d = mem_prompt.reshape(bp * N_MEM, D_MODEL)
    ones_g = jnp.ones((D_MODEL,), F32)
    cinit_p = jnp.zeros((bp, HALO, CONV_CH), F32)
    pad_s = jnp.zeros((DEPTH, bs, HALO - (CONV_K - 1), CONV_CH), F32)
    cinit_s = jnp.concatenate([pad_s, state_conv], axis=2)
    xp, xs = x_prompt, x_sample
    stacks = None
    cvp, mkp, mvp, cvs = [], [], [], []
    for li in range(DEPTH):
        wargs = (w_dw[li], b_dw[li], ln_conv_g[li], ln_conv_b[li], w_pw2_b, w_out_b, norm_post_g[li])

        zp, akp, avp, aks, avs = _proj_kv(
            xp.reshape(bp * s_len, D_MODEL), norm_pre_g[li], w_in_b, cache_attn_k, cache_attn_v, stacks,
            li=li, seq=s_len, l_keep=l_prompt, shift=t_len)
        zp = zp.reshape(bp, s_len, IN_W)
        a = _prompt_attn(zp, bias_near, bias_far)
        mkv = _proj(mem2d, ones_g, w_mem_b, li=li, normalize=False, bm=bp * N_MEM, bn=COL_BLK)
        mkv = mkv.reshape(bp, N_MEM, 2 * X_W)
        mk_block, mv_block = (((None, N_MEM, X_W), lambda bi, half=half: (bi, 0, half)) for half in (0, 1))
        xp, cst = _mix(xp, zp, a, cinit_p, mkv, mk_block, mkv, mv_block, *wargs, li=li, tile=256)
        cvp.append(cst)
        mkp.append(mkv[:, :, :X_W].reshape(bp, N_MEM, N_X_HEADS, HEAD_DIM))
        mvp.append(mkv[:, :, X_W:].reshape(bp, N_MEM, N_X_HEADS, HEAD_DIM))

        zs = _proj(xs.reshape(bs * t_len, D_MODEL), norm_pre_g[li], w_in_b, li=li,
                   normalize=True, bm=bs * t_len, bn=COL_BLK).reshape(bs, t_len, IN_W)
        a, aks, avs = _sample_attn(zs, cache_attn_k, cache_attn_v, bias_s, bias_s_new, aks, avs, li=li)
        stacks = (akp, avp, aks, avs)
        mem_block = ((None, None, N_MEM, N_X_HEADS, HEAD_DIM), lambda bi, li=li: (li, bi, 0, 0, 0))
        xs, cst = _mix(xs, zs, a, cinit_s[li], cache_mem_k, mem_block, cache_mem_v, mem_block, *wargs,
                       li=li, tile=t_len)
        cvs.append(cst)

    return (xp, xs, akp, avp, jnp.stack(cvp), jnp.stack(mkp), jnp.stack(mvp), aks, avs, jnp.stack(cvs))
```

```python
import functools

import numpy as np
import jax
import jax.numpy as jnp
from jax import lax
from jax.experimental import pallas as pl
from jax.experimental.pallas import tpu as pltpu

D_MODEL = 2048
DEPTH = 4
N_MEM = 256
HEAD_DIM = 128
ATT_W = D_MODEL // 2
N_ATT_HEADS = ATT_W // HEAD_DIM
DIL_PATTERNS = ((128, 1), (512, 4), (2048, 16))
WIN = max(w for w, _ in DIL_PATTERNS)
N_BUCKETS = 32
MAX_DIST = WIN
CONV_CH = D_MODEL // 4
CONV_K = 31
X_W = D_MODEL // 4
N_X_HEADS = X_W // HEAD_DIM
MIX_W = ATT_W + CONV_CH + X_W
IN_W = 4 * ATT_W + 3 * CONV_CH + 2 * X_W
EPS = 1e-6
NEG = -1e30
SCALE = HEAD_DIM ** -0.5
LOG2E = 1.4426950408889634

COL_BLK = 512
CB_Q, CB_K, CB_V, CB_GA = 0, 2, 4, 6
CB_UV, CB_UG, CB_GC, CB_QM, CB_GM = 8, 9, 10, 11, 12
HEADS_PER_COL_BLK = COL_BLK // HEAD_DIM

Q_BLK = 128
FAR_W, FAR_D = DIL_PATTERNS[-1]
NEAR_PATTERNS = DIL_PATTERNS[:-1]
NEAR_REACH = max(w for w, _ in NEAR_PATTERNS)
NEAR_KBLK = NEAR_REACH // Q_BLK + 1
FAR_KEYS = FAR_W // FAR_D
FAR_STEP = 4
assert FAR_KEYS == Q_BLK and NEAR_REACH % Q_BLK == 0 and FAR_STEP * FAR_STEP == FAR_D
FAR_GROUP = 4
NEAR_GROUP = 7
PROJ_BM = 1024
COPY_RB = 256
SAMPLE_RB = 1024
HALO = 32
SUBLANES = 8
LANES = 128
VMEM_LIMIT = 56 * 1024 * 1024

BF16 = jnp.bfloat16
F32 = jnp.float32


def _t5_bucket(dist):
    dist = np.asarray(dist)
    max_exact = N_BUCKETS // 2
    large = max_exact + (np.log(np.maximum(dist, 1) / max_exact)
                         / np.log(MAX_DIST / max_exact) * (N_BUCKETS - max_exact)).astype(np.int32)
    large = np.minimum(large, N_BUCKETS - 1)
    return np.where(dist < max_exact, dist, large).astype(np.int32)


def _distance_table(rel_bias, dists, patterns):
    dists = np.asarray(dists)
    cnt = np.zeros(dists.shape, np.int64)
    for w, dil in patterns:
        cnt += ((dists % dil == 0) & (dists <= w)).astype(np.int64)
    logc = jnp.log(jnp.asarray(np.maximum(cnt, 1), F32))
    tab = rel_bias[_t5_bucket(dists)].T.astype(F32) + logc[None, :]
    tab = jnp.where(jnp.asarray(cnt > 0)[None, :], tab, NEG)
    return jnp.concatenate([tab, jnp.full((tab.shape[0], 1), NEG, F32)], axis=1)


def _toeplitz(tab, off, rows, cols):
    n_valid = tab.shape[1] - 1
    lp = rows + cols
    j = np.concatenate([np.arange(cols), np.full(lp - (rows + cols - 1), off + n_valid),
                        np.arange(-(rows - 1), 0)])
    d = off - j
    idx = np.where((d >= 0) & (d < n_valid), d, n_valid)
    v = jnp.take(tab, jnp.asarray(idx, jnp.int32), axis=1)
    flat = jnp.tile(v, (1, rows))[:, :rows * (lp - 1)]
    return flat.reshape(tab.shape[0], rows, lp - 1)[:, :, :cols]


def _head_matched_bias(tab, off, t_len, n_rows):
    n_h, n_valid = tab.shape[0], tab.shape[1] - 1
    n_cols = n_rows * n_h
    gaps = jnp.full((n_h, n_valid, n_h - 1), NEG, F32)
    s = jnp.concatenate([tab[:, :n_valid, None], gaps], axis=2).reshape(n_h, n_valid * n_h)
    k_max = n_h * (off + t_len - 1) + n_h - 1
    pad_l = max(0, n_cols - 1 - n_h * off)
    pad_r = max(0, k_max - (n_valid * n_h - 1))
    s = jnp.concatenate([jnp.full((n_h, pad_l), NEG, F32), s, jnp.full((n_h, pad_r), NEG, F32)], axis=1)
    n_s = s.shape[1]
    lane_pad = -n_s % LANES
    rev = jnp.concatenate([s[:, ::-1], jnp.full((n_h, lane_pad), NEG, F32)], axis=1)
    starts = tuple((h, n_s - 1 - (n_h * (off + t) + h) - pad_l) for h in range(n_h) for t in range(t_len))
    return pl.pallas_call(
        functools.partial(_slice_rows_kernel, starts=starts, n_cols=n_cols),
        out_shape=jax.ShapeDtypeStruct((len(starts), n_cols), F32),
        name="bias_rows",
    )(rev)


def _slice_rows_kernel(src_ref, out_ref, *, starts, n_cols):
    for r, (h, start) in enumerate(starts):
        lo = start // LANES * LANES
        hi = -(-(start + n_cols) // LANES) * LANES
        window = src_ref[h:h + 1, lo:hi]
        out_ref[r:r + 1, :] = window[:, start - lo:start - lo + n_cols]


def _norm_rows(x_ref, g_ref, h_ref, normalize):
    x = x_ref[...]
    if normalize:
        x = x * lax.rsqrt(jnp.mean(x * x, axis=-1, keepdims=True) + EPS) * g_ref[...]
    h_ref[...] = x.astype(BF16)


def _proj_kernel(x_ref, g_ref, w_ref, o_ref, h_ref, *, normalize):
    @pl.when(pl.program_id(1) == 0)
    def _():
        _norm_rows(x_ref, g_ref, h_ref, normalize)

    o_ref[...] = jnp.dot(h_ref[...], w_ref[...], preferred_element_type=F32)


def _proj(x, g, w, *, li, normalize, bm, bn):
    m, d = x.shape
    n = w.shape[2]
    return pl.pallas_call(
        functools.partial(_proj_kernel, normalize=normalize),
        grid=(m // bm, n // bn),
        in_specs=[
            pl.BlockSpec((bm, d), lambda i, j: (i, 0)),
            pl.BlockSpec((1, d), lambda i, j: (0, 0)),
            pl.BlockSpec((None, d, bn), lambda i, j: (li, 0, j)),
        ],
        out_specs=pl.BlockSpec((bm, bn), lambda i, j: (i, j)),
        out_shape=jax.ShapeDtypeStruct((m, n), F32),
        scratch_shapes=[pltpu.VMEM((bm, d), BF16)],
        compiler_params=pltpu.CompilerParams(
            dimension_semantics=("parallel", "arbitrary"), vmem_limit_bytes=VMEM_LIMIT),
        name="proj",
    )(x, g.reshape(1, d), w)


def _mem_proj(x, w):
    m, d = x.shape
    depth, _, n = w.shape
    bn = COL_BLK
    return pl.pallas_call(
        functools.partial(_proj_kernel, normalize=False),
        grid=(depth, n // bn),
        in_specs=[
            pl.BlockSpec((m, d), lambda li, j: (0, 0)),
            pl.BlockSpec((1, d), lambda li, j: (0, 0)),
            pl.BlockSpec((None, d, bn), lambda li, j: (li, 0, j)),
        ],
        out_specs=pl.BlockSpec((None, m, bn), lambda li, j: (li, 0, j)),
        out_shape=jax.ShapeDtypeStruct((depth, m, n), F32),
        scratch_shapes=[pltpu.VMEM((m, d), BF16)],
        compiler_params=pltpu.CompilerParams(
            dimension_semantics=("arbitrary", "arbitrary"), vmem_limit_bytes=VMEM_LIMIT),
        name="mem_proj",
    )(x, jnp.ones((1, d), F32), w)


def _proj_kv_kernel(*refs,tiles_per_seq, tail_tiles, aliased, shift, n_copy, v_start):
    x_ref, g_ref, w_ref, kc_ref, kx_ref, vc_ref, vx_ref = refs[:7]
    o_ref, ko_ref, vo_ref, sk_ref, sv_ref, h_ref = refs[11:] if aliased else refs[7:]
    i, j = pl.program_id(0), pl.program_id(1)
    step = i * pl.num_programs(1) + j
    rb = kc_ref.shape[0]

    for first, cache, nxt, out in ((0, kc_ref, kx_ref, sk_ref), (v_start, vc_ref, vx_ref, sv_ref)):
        @pl.when((step >= first) & (step < first + n_copy))
        def _(cache=cache, nxt=nxt, out=out):
            out[0:rb - shift] = cache[shift:rb]
            out[rb - shift:rb] = nxt[...]

    @pl.when(j == 0)
    def _():
        _norm_rows(x_ref, g_ref, h_ref, True)

    o_ref[...] = jnp.dot(h_ref[...], w_ref[...], preferred_element_type=F32)

    in_tail = i % tiles_per_seq >= tiles_per_seq - tail_tiles
    for cb0, dst in ((CB_K, ko_ref), (CB_V, vo_ref)):
        for cb in range(cb0, cb0 + ATT_W // COL_BLK):
            @pl.when(in_tail & (j == cb))
            def _(cb=cb, cb0=cb0, dst=dst):
                for hh in range(HEADS_PER_COL_BLK):
                    head = (cb - cb0) * HEADS_PER_COL_BLK + hh
                    dst[pl.ds(head, o_ref.shape[0], stride=N_ATT_HEADS), :] = (
                        o_ref[:, hh * HEAD_DIM:(hh + 1) * HEAD_DIM])


def _proj_kv(x, g, w, cache_k, cache_v, prev, *, li, seq, l_keep, shift):
    m, d = x.shape
    n = w.shape[2]
    bsz = m // seq
    bm, bn = PROJ_BM, COL_BLK
    tiles_per_seq, tail_tiles = seq // bm, l_keep // bm
    assert seq % bm == 0 and l_keep % bm == 0
    aliased = prev is not None
    kv_shape = jax.ShapeDtypeStruct((DEPTH, bsz, l_keep * N_ATT_HEADS, HEAD_DIM), F32)
    cache_shape = jax.ShapeDtypeStruct(cache_k.shape, F32)
    any_spec = pl.BlockSpec(memory_space=pl.ANY)
    n_steps = (m // bm) * (n // bn)
    n_seq, l_buf = cache_k.shape[1], cache_k.shape[2]
    rb = COPY_RB
    blks_per_seq = l_buf // rb
    n_copy = n_seq * blks_per_seq
    v_start = n_steps - n_copy
    assert l_buf % rb == 0 and rb % shift == 0 and 0 <= v_start

    def kv_map(i, j):
        return (li, i // tiles_per_seq, jnp.maximum(i % tiles_per_seq - (tiles_per_seq - tail_tiles), 0), 0)

    def copy_specs(first):
        def blk(i, j):
            return jnp.clip(i * (n // bn) + j - first, 0, n_copy - 1)

        def main(i, j):
            c = blk(i, j)
            return (li, c // blks_per_seq, c % blks_per_seq, 0, 0)

        def nxt(i, j):
            c = jnp.minimum(blk(i, j) + 1, n_copy - 1)
            return (li, c // blks_per_seq, (c % blks_per_seq) * (rb // shift), 0, 0)

        return (pl.BlockSpec((None, None, rb, N_ATT_HEADS, HEAD_DIM), main),
                pl.BlockSpec((None, None, shift, N_ATT_HEADS, HEAD_DIM), nxt))

    k_main, k_next = copy_specs(0)
    v_main, v_next = copy_specs(v_start)
    kv_spec = pl.BlockSpec((None, None, bm * N_ATT_HEADS, HEAD_DIM), kv_map)
    in_specs = [
        pl.BlockSpec((bm, d), lambda i, j: (i, 0)),
        pl.BlockSpec((1, d), lambda i, j: (0, 0)),
        pl.BlockSpec((None, d, bn), lambda i, j: (li, 0, j)),
        k_main, k_next, v_main, v_next,
    ]
    args = [x, g.reshape(1, d), w, cache_k, cache_k, cache_v, cache_v]
    if aliased:
        in_specs += [any_spec] * 4
        args += list(prev)
    return pl.pallas_call(
        functools.partial(_proj_kv_kernel, tiles_per_seq=tiles_per_seq, tail_tiles=tail_tiles,
                          aliased=aliased, shift=shift, n_copy=n_copy, v_start=v_start),
        grid=(m // bm, n // bn),
        in_specs=in_specs,
        out_specs=[pl.BlockSpec((bm, bn), lambda i, j: (i, j)), kv_spec, kv_spec, k_main, v_main],
        out_shape=[jax.ShapeDtypeStruct((m, n), F32), kv_shape, kv_shape, cache_shape, cache_shape],
        scratch_shapes=[pltpu.VMEM((bm, d), BF16)],
        input_output_aliases={7: 1, 8: 2, 9: 3, 10: 4} if aliased else {},
        compiler_params=pltpu.CompilerParams(
            dimension_semantics=("arbitrary", "arbitrary"), vmem_limit_bytes=VMEM_LIMIT),
        name="proj_kv",
    )(*args)


def _qk(q, k):
    return lax.dot_general(q, k, (((1,), (1,)), ((), ())), preferred_element_type=F32)


def _aligned(x, m):
    return x if isinstance(x, int) else pl.multiple_of(x, m)


def _prompt_attn_kernel(q_ref, k_ref, v_ref, bn_ref, bf_ref, o_ref,
                        qb_ref, kb_ref, vb_ref, qf_ref, kf_ref, vf_ref, of_ref, lf_ref, tmp_ref, *, seq):
    sub = seq // FAR_D
    chunk = 2 * Q_BLK
    val = slice(0, HEAD_DIM)

    def cast_body(c, carry):
        r0 = pl.multiple_of(c * chunk, chunk)
        rows = pl.ds(r0, chunk)
        qb_ref[rows, :] = (q_ref[rows, :] * (SCALE * LOG2E)).astype(BF16)
        kb_ref[rows, :] = k_ref[rows, :].astype(BF16)
        vb_ref[rows, val] = v_ref[rows, :].astype(BF16)
        vb_ref[rows, HEAD_DIM:] = jnp.ones((chunk, HEAD_DIM), BF16)
        vf_ref[rows, HEAD_DIM:] = jnp.ones((chunk, HEAD_DIM), BF16)
        return carry

    lax.fori_loop(0, seq // chunk, cast_body, 0)

    quarter = seq // FAR_STEP
    for src_ref, dst_ref, scale in ((q_ref, qf_ref, SCALE * LOG2E), (k_ref, kf_ref, None),
                                    (v_ref, vf_ref, None)):
        for r1 in range(FAR_STEP):
            tmp_ref[r1 * quarter:(r1 + 1) * quarter, :] = src_ref[pl.ds(r1, quarter, stride=FAR_STEP), :]
        for r1 in range(FAR_STEP):
            for r2 in range(FAR_STEP):
                r = FAR_STEP * r2 + r1
                x = tmp_ref[pl.ds(r1 * quarter + r2, sub, stride=FAR_STEP), :]
                if scale is not None:
                    x = x * scale
                dst_ref[r * sub:(r + 1) * sub, val] = x.astype(BF16)

    def far_tiles(tiles):
        geo = []
        for r, a_blk in tiles:
            n_kt = min(a_blk, 1) + 1
            q0 = pl.multiple_of(r * sub + a_blk * Q_BLK, Q_BLK)
            k0 = pl.multiple_of(r * sub + (a_blk + 1 - n_kt) * Q_BLK, Q_BLK)
            dst = pl.ds(a_blk * Q_BLK * FAR_D + r, Q_BLK, stride=FAR_D)
            geo.append((pl.ds(q0, Q_BLK), pl.ds(k0, n_kt * Q_BLK), n_kt, dst))
        scores = [_qk(qf_ref[rows, :], kf_ref[keys, :]) + bf_ref[:, (2 - n_kt) * Q_BLK:]
                  for rows, keys, n_kt, _ in geo]
        probs = []
        for s in scores:
            m = jnp.max(s, axis=-1, keepdims=True)
            probs.append((jnp.exp2(s - m).astype(BF16), m))
        for (_, keys, _, dst), (p, m) in zip(geo, probs):
            o = jnp.dot(p, vf_ref[keys, :], preferred_element_type=F32)
            l = o[:, HEAD_DIM:]
            of_ref[dst, :] = o[:, val] / l
            lf_ref[dst, :] = m + jnp.log2(l)

    def far_body(g, carry):
        far_tiles([(g * FAR_GROUP + r_off, a_blk)
                   for r_off in range(FAR_GROUP) for a_blk in range(sub // Q_BLK)])
        return carry

    lax.fori_loop(0, FAR_D // FAR_GROUP, far_body, 0)

    def near_tiles(tiles):
        geo = []
        for i, n_kt in tiles:
            q0 = _aligned(i * Q_BLK, Q_BLK)
            k0 = _aligned((i + 1 - n_kt) * Q_BLK, Q_BLK)
            geo.append((pl.ds(q0, Q_BLK), pl.ds(k0, n_kt * Q_BLK), n_kt))
        scores = [_qk(qb_ref[rows, :], kb_ref[keys, :]) + bn_ref[:, (NEAR_KBLK - n_kt) * Q_BLK:]
                  for rows, keys, n_kt in geo]
        probs = []
        for (rows, _, _), s in zip(geo, scores):
            m_far = lf_ref[rows, 0:1]
            m = jnp.maximum(jnp.max(s, axis=-1, keepdims=True), m_far)
            probs.append((jnp.exp2(s - m).astype(BF16), jnp.exp2(m_far - m)))
        for (rows, keys, _), (p, w_far) in zip(geo, probs):
            o = jnp.dot(p, vb_ref[keys, :], preferred_element_type=F32)
            acc = w_far * of_ref[rows, :] + o[:, val]
            o_ref[rows, :] = acc / (w_far + o[:, HEAD_DIM:])

    n_blk = seq // Q_BLK
    n_edge = NEAR_KBLK - 1
    assert n_blk > n_edge and (n_blk - n_edge) % NEAR_GROUP == 0
    near_tiles([(i, i + 1) for i in range(n_edge)])

    def near_body(g, carry):
        near_tiles([(n_edge + g * NEAR_GROUP + off, NEAR_KBLK) for off in range(NEAR_GROUP)])
        return carry

    lax.fori_loop(0, (n_blk - n_edge) // NEAR_GROUP, near_body, 0)


def _prompt_attn(z, bias_near, bias_far):
    b, s, _ = z.shape
    assert s % (FAR_D * Q_BLK) == 0

    def head_cols(cb):
        return pl.BlockSpec((None, s, HEAD_DIM), lambda bi, h: (bi, 0, cb * HEADS_PER_COL_BLK + h))

    return pl.pallas_call(
        functools.partial(_prompt_attn_kernel, seq=s),
        grid=(b, N_ATT_HEADS),
        in_specs=[
            head_cols(CB_Q), head_cols(CB_K), head_cols(CB_V),
            pl.BlockSpec((None, Q_BLK, NEAR_KBLK * Q_BLK), lambda bi, h: (h, 0, 0)),
            pl.BlockSpec((None, Q_BLK, 2 * Q_BLK), lambda bi, h: (h, 0, 0)),
        ],
        out_specs=pl.BlockSpec((None, s, HEAD_DIM), lambda bi, h: (bi, 0, h)),
        out_shape=jax.ShapeDtypeStruct((b, s, ATT_W), F32),
        scratch_shapes=[pltpu.VMEM((s, HEAD_DIM), BF16), pltpu.VMEM((s, HEAD_DIM), BF16),
                        pltpu.VMEM((s, 2 * HEAD_DIM), BF16)] * 2 + [pltpu.VMEM((s, HEAD_DIM), F32)] * 3,
        compiler_params=pltpu.CompilerParams(
            dimension_semantics=("parallel", "parallel"), vmem_limit_bytes=VMEM_LIMIT),
        name="prompt_attn",
    )(z, z, z, bias_near, bias_far)


def _sample_attn_kernel(q_ref, kn_ref, vn_ref, kc_ref, vc_ref, bias_ref, biasn_ref, sk_hbm, sv_hbm,
                        o_ref, ko_ref, vo_ref, qa_ref, kn3_ref, vn3_ref, m_ref, l_ref, acc_ref, *, t_len, rb):
    del sk_hbm, sv_hbm
    j = pl.program_id(1)
    last = pl.num_programs(1) - 1
    n_heads = N_ATT_HEADS

    @pl.when(j == 0)
    def _():
        for h in range(n_heads):
            cols = slice(h * HEAD_DIM, (h + 1) * HEAD_DIM)
            qa_ref[h * t_len:(h + 1) * t_len, :] = q_ref[:, cols] * SCALE
            kn3_ref[:, h, :] = kn_ref[:, cols]
            vn3_ref[:, h, :] = vn_ref[:, cols]
        m_ref[...] = jnp.full(m_ref.shape, NEG, F32)
        l_ref[...] = jnp.zeros(l_ref.shape, F32)
        acc_ref[...] = jnp.zeros(acc_ref.shape, F32)

    qa = qa_ref[...].astype(BF16)

    def accumulate(k, v, bias):
        s = _qk(qa, k.astype(BF16)) + bias
        m_old = m_ref[...]
        m_new = jnp.maximum(m_old, jnp.max(s, axis=-1, keepdims=True))
        alpha = jnp.exp(m_old - m_new)
        p = jnp.exp(s - m_new)
        l_ref[...] = alpha * l_ref[...] + jnp.sum(p, axis=-1, keepdims=True)
        acc_ref[...] = alpha * acc_ref[...] + jnp.dot(p.astype(BF16), v.astype(BF16),
                                                      preferred_element_type=F32)
        m_ref[...] = m_new

    accumulate(kc_ref[...].reshape(rb * n_heads, HEAD_DIM), vc_ref[...].reshape(rb * n_heads, HEAD_DIM),
               bias_ref[...])

    @pl.when(j == last)
    def _():
        ko_ref[...] = kn3_ref[...]
        vo_ref[...] = vn3_ref[...]
        accumulate(kn3_ref[...].reshape(t_len * n_heads, HEAD_DIM),
                   vn3_ref[...].reshape(t_len * n_heads, HEAD_DIM), biasn_ref[...])
        o = acc_ref[...] / l_ref[...]
        for h in range(n_heads):
            o_ref[:, h * HEAD_DIM:(h + 1) * HEAD_DIM] = o[h * t_len:(h + 1) * t_len, :]


def _sample_attn(z, cache_k, cache_v, bias, bias_new, stack_k, stack_v, *, li):
    b, t_len, _ = z.shape
    l_buf = cache_k.shape[2]
    rb = SAMPLE_RB
    assert t_len == SUBLANES and l_buf % rb == 0
    rows = N_ATT_HEADS * t_len

    def zspec(cb):
        return pl.BlockSpec((None, t_len, ATT_W), lambda bi, j: (bi, 0, cb * COL_BLK // ATT_W))

    blk_spec = pl.BlockSpec((None, None, rb, N_ATT_HEADS, HEAD_DIM), lambda bi, j: (li, bi, j, 0, 0))
    new_spec = pl.BlockSpec((None, None, t_len, N_ATT_HEADS, HEAD_DIM),
                            lambda bi, j: (li, bi, l_buf // t_len - 1, 0, 0))
    any_spec = pl.BlockSpec(memory_space=pl.ANY)
    in_specs = [
        zspec(CB_Q), zspec(CB_K), zspec(CB_V), blk_spec, blk_spec,
        pl.BlockSpec((rows, rb * N_ATT_HEADS), lambda bi, j: (0, j)),
        pl.BlockSpec((rows, rows), lambda bi, j: (0, 0)),
        any_spec, any_spec,
    ]
    args = [z, z, z, cache_k, cache_v, bias, bias_new, stack_k, stack_v]
    return pl.pallas_call(
        functools.partial(_sample_attn_kernel, t_len=t_len, rb=rb),
        grid=(b, l_buf // rb),
        in_specs=in_specs,
        out_specs=[pl.BlockSpec((None, t_len, ATT_W), lambda bi, j: (bi, 0, 0)), new_spec, new_spec],
        out_shape=[
            jax.ShapeDtypeStruct((b, t_len, ATT_W), F32),
            jax.ShapeDtypeStruct(cache_k.shape, F32),
            jax.ShapeDtypeStruct(cache_v.shape, F32),
        ],
        scratch_shapes=[
            pltpu.VMEM((rows, HEAD_DIM), F32),
            pltpu.VMEM((t_len, N_ATT_HEADS, HEAD_DIM), F32), pltpu.VMEM((t_len, N_ATT_HEADS, HEAD_DIM), F32),
            pltpu.VMEM((rows, 1), F32), pltpu.VMEM((rows, 1), F32), pltpu.VMEM((rows, HEAD_DIM), F32),
        ],
        input_output_aliases={7: 1, 8: 2},
        compiler_params=pltpu.CompilerParams(
            dimension_semantics=("arbitrary", "arbitrary"), vmem_limit_bytes=VMEM_LIMIT),
        name="sample_attn",
    )(*args)


def _silu(x):
    return x * jax.nn.sigmoid(x)


def _mix_kernel(x_ref, a_ref, ga_ref, uv_ref, ug_ref, gc_ref, qm_ref, gm_ref, cinit_ref, mk_ref, mv_ref,
                wdw_ref, bdw_ref, lng_ref, lnb_ref, wpw_ref, wout_ref, gpost_ref,
                xo_ref, cs_ref, uext_ref, mix_ref, *, nb, tile, conv_chunk):
    @pl.when(pl.program_id(1) == 0)
    def _():
        uext_ref[:, 0:HALO, :] = cinit_ref[...]

    first = HALO - (CONV_K - 1)
    chunks = []
    for b in range(nb):
        ub_ref = uext_ref.at[b]
        ub_ref[HALO:HALO + tile, :] = uv_ref[b] * jax.nn.sigmoid(ug_ref[b])
        for c0 in range(0, tile, conv_chunk):
            acc = jnp.broadcast_to(bdw_ref[...], (conv_chunk, CONV_CH))
            for phase in range(SUBLANES):
                rows = conv_chunk + (SUBLANES if phase else 0)
                part = None
                for k in range(CONV_K):
                    if (first + k) % SUBLANES == phase:
                        base = c0 + first + k - phase
                        term = ub_ref[base:base + rows, :] * wdw_ref[k:k + 1, :]
                        part = term if part is None else part + term
                if part is not None:
                    acc = acc + part[phase:phase + conv_chunk, :]
            chunks.append(acc)
        cs_ref[b] = ub_ref[tile + first:tile + HALO, :]
        tail = ub_ref[tile:tile + HALO, :]
        ub_ref[0:HALO, :] = tail
    c = jnp.concatenate(chunks, axis=0) if len(chunks) > 1 else chunks[0]

    mu = jnp.mean(c, axis=-1, keepdims=True)
    var = jnp.mean(jnp.square(c - mu), axis=-1, keepdims=True)
    c = (c - mu) * lax.rsqrt(var + EPS) * lng_ref[...] + lnb_ref[...]
    c = jnp.dot(_silu(c).astype(BF16), wpw_ref[...], preferred_element_type=F32)

    def mem_head(ref, h):
        if len(ref.shape) == 3:
            return ref[:, h, :].astype(BF16)
        return ref[:, h * HEAD_DIM:(h + 1) * HEAD_DIM].astype(BF16)

    for b in range(nb):
        rows = slice(b * tile, (b + 1) * tile)
        mix_ref[rows, 0:ATT_W] = (a_ref[b] * _silu(ga_ref[b])).astype(BF16)
        mix_ref[rows, ATT_W:ATT_W + CONV_CH] = (c[rows] * _silu(gc_ref[b])).astype(BF16)
        for h in range(N_X_HEADS):
            cols = slice(h * HEAD_DIM, (h + 1) * HEAD_DIM)
            q = (qm_ref[b, :, cols] * SCALE).astype(BF16)
            s = _qk(q, mem_head(mk_ref.at[b], h))
            p = jnp.exp(s - jnp.max(s, axis=-1, keepdims=True))
            o = jnp.dot(p.astype(BF16), mem_head(mv_ref.at[b], h), preferred_element_type=F32)
            o = o / jnp.sum(p, axis=-1, keepdims=True)
            off = ATT_W + CONV_CH + h * HEAD_DIM
            mix_ref[rows, off:off + HEAD_DIM] = (o * _silu(gm_ref[b, :, cols])).astype(BF16)

    y = jnp.dot(mix_ref[...], wout_ref[...], preferred_element_type=F32)
    y = y * lax.rsqrt(jnp.mean(y * y, axis=-1, keepdims=True) + EPS) * gpost_ref[...]
    for b in range(nb):
        xo_ref[b] = x_ref[b] + y[b * tile:(b + 1) * tile]


def _mix(x, z, a, cinit, mk_arr, mk_block, mv_arr, mv_block, w_dw, b_dw, ln_g, ln_b, w_pw2, w_out, g_post,
         *, li, nb, tile):
    b, s, _ = x.shape
    conv_chunk = min(tile, 32)
    assert b % nb == 0 and s % tile == 0

    def rows(width, cb=0):
        return pl.BlockSpec((nb, tile, width), lambda bi, t: (bi, t, cb * COL_BLK // width))

    def const(shape):
        return pl.BlockSpec(shape, lambda bi, t: (0,) * len(shape))

    def layer(shape):
        return pl.BlockSpec((None,) + shape, lambda bi, t: (li,) + (0,) * len(shape))

    def mem(block):
        shape, index = block
        return pl.BlockSpec(shape, lambda bi, t: index(bi))

    return pl.pallas_call(
        functools.partial(_mix_kernel, nb=nb, tile=tile, conv_chunk=conv_chunk),
        grid=(b // nb, s // tile),
        in_specs=[
            rows(D_MODEL), rows(ATT_W),
            rows(ATT_W, CB_GA), rows(COL_BLK, CB_UV), rows(COL_BLK, CB_UG), rows(COL_BLK, CB_GC),
            rows(COL_BLK, CB_QM), rows(COL_BLK, CB_GM),
            pl.BlockSpec((nb, HALO, CONV_CH), lambda bi, t: (bi, 0, 0)),
            mem(mk_block), mem(mv_block),
            const((CONV_K, CONV_CH)), const((1, CONV_CH)), const((1, CONV_CH)), const((1, CONV_CH)),
            layer((CONV_CH, CONV_CH)), layer((MIX_W, D_MODEL)), const((1, D_MODEL)),
        ],
        out_specs=[
            rows(D_MODEL),
            pl.BlockSpec((nb, CONV_K - 1, CONV_CH), lambda bi, t: (bi, 0, 0)),
        ],
        out_shape=[
            jax.ShapeDtypeStruct((b, s, D_MODEL), F32),
            jax.ShapeDtypeStruct((b, CONV_K - 1, CONV_CH), F32),
        ],
        scratch_shapes=[pltpu.VMEM((nb, HALO + tile, CONV_CH), F32), pltpu.VMEM((nb * tile, MIX_W), BF16)],
        compiler_params=pltpu.CompilerParams(
            dimension_semantics=("parallel", "arbitrary"), vmem_limit_bytes=VMEM_LIMIT),
        name="mix",
    )(x, a, z, z, z, z, z, z, cinit, mk_arr, mv_arr,
      w_dw, b_dw.reshape(1, -1), ln_g.reshape(1, -1), ln_b.reshape(1, -1), w_pw2, w_out,
      g_post.reshape(1, -1))


def kernel(x_prompt, x_sample, mem_prompt, cache_attn_k, cache_attn_v, state_conv, cache_mem_k, cache_mem_v,
           rel_bias, norm_pre_g, w_in, w_dw, b_dw, ln_conv_g, ln_conv_b, w_pw2, w_mem_kv, w_out, norm_post_g):
    bp, s_len, _ = x_prompt.shape
    bs, t_len, _ = x_sample.shape
    l_buf = cache_attn_k.shape[2]
    l_prompt = min(WIN, s_len)

    w_in_b, w_pw2_b = w_in.astype(BF16), w_pw2.astype(BF16)
    w_mem_b, w_out_b = w_mem_kv.astype(BF16), w_out.astype(BF16)

    tab_near = _distance_table(rel_bias, np.arange(NEAR_REACH + 1), NEAR_PATTERNS)
    tab_far = _distance_table(rel_bias, FAR_D * np.arange(FAR_KEYS + 1), DIL_PATTERNS[-1:])
    bias_near = _toeplitz(tab_near * LOG2E, NEAR_REACH, Q_BLK, NEAR_KBLK * Q_BLK)
    bias_far = _toeplitz(tab_far * LOG2E, FAR_KEYS, Q_BLK, 2 * Q_BLK)
    tab_all = _distance_table(rel_bias, np.arange(WIN + 1), DIL_PATTERNS)
    bias_s = _head_matched_bias(tab_all, l_buf, t_len, l_buf)
    bias_s_new = _head_matched_bias(tab_all, 0, t_len, t_len)

    mkv = _mem_proj(mem_prompt.reshape(bp * N_MEM, D_MODEL), w_mem_b).reshape(DEPTH, bp, N_MEM, 2 * X_W)
    cinit_p = jnp.zeros((bp, HALO, CONV_CH), F32)
    pad_s = jnp.zeros((DEPTH, bs, HALO - (CONV_K - 1), CONV_CH), F32)
    cinit_s = jnp.concatenate([pad_s, state_conv], axis=2)
    xp, xs = x_prompt, x_sample
    stacks = None
    cvp, cvs = [], []
    for li in range(DEPTH):
        wargs = (w_dw[li], b_dw[li], ln_conv_g[li], ln_conv_b[li], w_pw2_b, w_out_b, norm_post_g[li])

        zp, akp, avp, aks, avs = _proj_kv(
            xp.reshape(bp * s_len, D_MODEL), norm_pre_g[li], w_in_b, cache_attn_k, cache_attn_v, stacks,
            li=li, seq=s_len, l_keep=l_prompt, shift=t_len)
        zp = zp.reshape(bp, s_len, IN_W)
        a = _prompt_attn(zp, bias_near, bias_far)
        mk_block, mv_block = (((None, 1, N_MEM, X_W), lambda bi, li=li, half=half: (li, bi, 0, half))
                              for half in (0, 1))
        xp, cst = _mix(xp, zp, a, cinit_p, mkv, mk_block, mkv, mv_block, *wargs, li=li, nb=1, tile=256)
        cvp.append(cst)

        zs = _proj(xs.reshape(bs * t_len, D_MODEL), norm_pre_g[li], w_in_b, li=li,
                   normalize=True, bm=bs * t_len, bn=COL_BLK).reshape(bs, t_len, IN_W)
        a, aks, avs = _sample_attn(zs, cache_attn_k, cache_attn_v, bias_s, bias_s_new, aks, avs, li=li)
        stacks = (akp, avp, aks, avs)
        mem_block = ((None, bs, N_MEM, N_X_HEADS, HEAD_DIM), lambda bi, li=li: (li, bi, 0, 0, 0))
        xs, cst = _mix(xs, zs, a, cinit_s[li], cache_mem_k, mem_block, cache_mem_v, mem_block, *wargs,
                       li=li, nb=bs, tile=t_len)
        cvs.append(cst)

    kv5 = (DEPTH, bp, l_prompt, N_ATT_HEADS, HEAD_DIM)
    mem5 = (DEPTH, bp, N_MEM, N_X_HEADS, HEAD_DIM)
    return (xp, xs, akp.reshape(kv5), avp.reshape(kv5), jnp.stack(cvp),
            mkv[..., :X_W].reshape(mem5), mkv[..., X_W:].reshape(mem5), aks, avs, jnp.stack(cvs))
```

```python
import functools

import numpy as np
import jax
import jax.numpy as jnp
from jax import lax
from jax.experimental import pallas as pl
from jax.experimental.pallas import tpu as pltpu

D_MODEL = 2048
DEPTH = 4
N_MEM = 256
HEAD_DIM = 128
ATT_W = D_MODEL // 2
N_ATT_HEADS = ATT_W // HEAD_DIM
DIL_PATTERNS = ((128, 1), (512, 4), (2048, 16))
WIN = max(w for w, _ in DIL_PATTERNS)
N_BUCKETS = 32
MAX_DIST = WIN
CONV_CH = D_MODEL // 4
CONV_K = 31
X_W = D_MODEL // 4
N_X_HEADS = X_W // HEAD_DIM
MIX_W = ATT_W + CONV_CH + X_W
IN_W = 4 * ATT_W + 3 * CONV_CH + 2 * X_W
EPS = 1e-6
NEG = -1e30
SCALE = HEAD_DIM ** -0.5
LOG2E = 1.4426950408889634

COL_BLK = 512
CB_Q, CB_K, CB_V, CB_GA = 0, 2, 4, 6
CB_UV, CB_UG, CB_GC, CB_QM, CB_GM = 8, 9, 10, 11, 12
N_QKV_BLK = CB_GA
QKV_W = N_QKV_BLK * COL_BLK
REST_W = IN_W - QKV_W
HEADS_PER_COL_BLK = COL_BLK // HEAD_DIM

Q_BLK = 128
FAR_W, FAR_D = DIL_PATTERNS[-1]
NEAR_PATTERNS = DIL_PATTERNS[:-1]
NEAR_REACH = max(w for w, _ in NEAR_PATTERNS)
NEAR_KBLK = NEAR_REACH // Q_BLK + 1
FAR_KEYS = FAR_W // FAR_D
FAR_STEP = 4
assert FAR_KEYS == Q_BLK and NEAR_REACH % Q_BLK == 0 and FAR_STEP * FAR_STEP == FAR_D
FAR_GROUP = 4
NEAR_GROUP = 7
PROJ_BM = 1024
COPY_RB = 256
SAMPLE_RB = 1024
HALO = 32
SUBLANES = 8
LANES = 128
VMEM_LIMIT = 60 * 1024 * 1024

BF16 = jnp.bfloat16
F32 = jnp.float32


def _t5_bucket(dist):
    dist = np.asarray(dist)
    max_exact = N_BUCKETS // 2
    large = max_exact + (np.log(np.maximum(dist, 1) / max_exact)
                         / np.log(MAX_DIST / max_exact) * (N_BUCKETS - max_exact)).astype(np.int32)
    large = np.minimum(large, N_BUCKETS - 1)
    return np.where(dist < max_exact, dist, large).astype(np.int32)


def _distance_table(rel_bias, dists, patterns):
    dists = np.asarray(dists)
    cnt = np.zeros(dists.shape, np.int64)
    for w, dil in patterns:
        cnt += ((dists % dil == 0) & (dists <= w)).astype(np.int64)
    logc = jnp.log(jnp.asarray(np.maximum(cnt, 1), F32))
    tab = rel_bias[_t5_bucket(dists)].T.astype(F32) + logc[None, :]
    tab = jnp.where(jnp.asarray(cnt > 0)[None, :], tab, NEG)
    return jnp.concatenate([tab, jnp.full((tab.shape[0], 1), NEG, F32)], axis=1)


def _toeplitz(tab, off, rows, cols):
    n_valid = tab.shape[1] - 1
    lp = rows + cols
    j = np.concatenate([np.arange(cols), np.full(lp - (rows + cols - 1), off + n_valid),
                        np.arange(-(rows - 1), 0)])
    d = off - j
    idx = np.where((d >= 0) & (d < n_valid), d, n_valid)
    v = jnp.take(tab, jnp.asarray(idx, jnp.int32), axis=1)
    flat = jnp.tile(v, (1, rows))[:, :rows * (lp - 1)]
    return flat.reshape(tab.shape[0], rows, lp - 1)[:, :, :cols]


def _head_matched_bias(tab, off, t_len, n_rows):
    n_h, n_valid = tab.shape[0], tab.shape[1] - 1
    n_cols = n_rows * n_h
    gaps = jnp.full((n_h, n_valid, n_h - 1), NEG, F32)
    s = jnp.concatenate([tab[:, :n_valid, None], gaps], axis=2).reshape(n_h, n_valid * n_h)
    k_max = n_h * (off + t_len - 1) + n_h - 1
    pad_l = max(0, n_cols - 1 - n_h * off)
    pad_r = max(0, k_max - (n_valid * n_h - 1))
    s = jnp.concatenate([jnp.full((n_h, pad_l), NEG, F32), s, jnp.full((n_h, pad_r), NEG, F32)], axis=1)
    n_s = s.shape[1]
    lane_pad = -n_s % LANES
    rev = jnp.concatenate([s[:, ::-1], jnp.full((n_h, lane_pad), NEG, F32)], axis=1)
    starts = tuple((h, n_s - 1 - (n_h * (off + t) + h) - pad_l) for h in range(n_h) for t in range(t_len))
    return pl.pallas_call(
        functools.partial(_slice_rows_kernel, starts=starts, n_cols=n_cols),
        out_shape=jax.ShapeDtypeStruct((len(starts), n_cols), F32),
        name="bias_rows",
    )(rev)


def _slice_rows_kernel(src_ref, out_ref, *, starts, n_cols):
    for r, (h, start) in enumerate(starts):
        lo = start // LANES * LANES
        hi = -(-(start + n_cols) // LANES) * LANES
        window = src_ref[h:h + 1, lo:hi]
        out_ref[r:r + 1, :] = window[:, start - lo:start - lo + n_cols]


def _norm_rows(x_ref, g_ref, h_ref, normalize):
    x = x_ref[...]
    if normalize:
        x = x * lax.rsqrt(jnp.mean(x * x, axis=-1, keepdims=True) + EPS) * g_ref[...]
    h_ref[...] = x.astype(BF16)


def _proj_kernel(x_ref, g_ref, w_ref, o_ref, h_ref, *, normalize):
    @pl.when(pl.program_id(1) == 0)
    def _():
        _norm_rows(x_ref, g_ref, h_ref, normalize)

    o_ref[...] = jnp.dot(h_ref[...], w_ref[...], preferred_element_type=F32)


def _store_split(res, j, zq_ref, zr_ref):
    @pl.when(j < N_QKV_BLK)
    def _():
        zq_ref[...] = res

    @pl.when(j >= N_QKV_BLK)
    def _():
        zr_ref[...] = res.astype(BF16)


def _split_out(m, bm, bn):
    assert bn == COL_BLK
    specs = [pl.BlockSpec((bm, bn), lambda i, j: (i, jnp.minimum(j, N_QKV_BLK - 1))),
             pl.BlockSpec((bm, bn), lambda i, j: (i, jnp.maximum(j - N_QKV_BLK, 0)))]
    shapes = [jax.ShapeDtypeStruct((m, QKV_W), F32), jax.ShapeDtypeStruct((m, REST_W), BF16)]
    return specs, shapes


def _proj_split_kernel(x_ref, g_ref, w_ref, zq_ref, zr_ref, h_ref):
    j = pl.program_id(1)

    @pl.when(j == 0)
    def _():
        _norm_rows(x_ref, g_ref, h_ref, True)

    _store_split(jnp.dot(h_ref[...], w_ref[...], preferred_element_type=F32), j, zq_ref, zr_ref)


def _proj(x, g, w, *, li, bm, bn):
    m, d = x.shape
    n = w.shape[2]
    out_specs, out_shape = _split_out(m, bm, bn)
    return pl.pallas_call(
        _proj_split_kernel,
        grid=(m // bm, n // bn),
        in_specs=[
            pl.BlockSpec((bm, d), lambda i, j: (i, 0)),
            pl.BlockSpec((1, d), lambda i, j: (0, 0)),
            pl.BlockSpec((None, d, bn), lambda i, j: (li, 0, j)),
        ],
        out_specs=out_specs,
        out_shape=out_shape,
        scratch_shapes=[pltpu.VMEM((bm, d), BF16)],
        compiler_params=pltpu.CompilerParams(
            dimension_semantics=("arbitrary", "arbitrary"), vmem_limit_bytes=VMEM_LIMIT),
        name="proj",
    )(x, g.reshape(1, d), w)


def _mem_proj(x, w):
    m, d = x.shape
    depth, _, n = w.shape
    bn = COL_BLK
    return pl.pallas_call(
        functools.partial(_proj_kernel, normalize=False),
        grid=(depth, n // bn),
        in_specs=[
            pl.BlockSpec((m, d), lambda li, j: (0, 0)),
            pl.BlockSpec((1, d), lambda li, j: (0, 0)),
            pl.BlockSpec((None, d, bn), lambda li, j: (li, 0, j)),
        ],
        out_specs=pl.BlockSpec((None, m, bn), lambda li, j: (li, 0, j)),
        out_shape=jax.ShapeDtypeStruct((depth, m, n), F32),
        scratch_shapes=[pltpu.VMEM((m, d), BF16)],
        compiler_params=pltpu.CompilerParams(
            dimension_semantics=("arbitrary", "arbitrary"), vmem_limit_bytes=VMEM_LIMIT),
        name="mem_proj",
    )(x, jnp.ones((1, d), F32), w)


def _proj_kv_kernel(*refs,tiles_per_seq, tail_tiles, aliased, shift, n_copy, v_start):
    x_ref, g_ref, w_ref, kc_ref, kx_ref, vc_ref, vx_ref = refs[:7]
    o_ref, zr_ref, ko_ref, vo_ref, sk_ref, sv_ref, h_ref = refs[11:] if aliased else refs[7:]
    i, j = pl.program_id(0), pl.program_id(1)
    step = i * pl.num_programs(1) + j
    rb = kc_ref.shape[0]

    for first, cache, nxt, out in ((0, kc_ref, kx_ref, sk_ref), (v_start, vc_ref, vx_ref, sv_ref)):
        @pl.when((step >= first) & (step < first + n_copy))
        def _(cache=cache, nxt=nxt, out=out):
            out[0:rb - shift] = cache[shift:rb]
            out[rb - shift:rb] = nxt[...]

    @pl.when(j == 0)
    def _():
        _norm_rows(x_ref, g_ref, h_ref, True)

    _store_split(jnp.dot(h_ref[...], w_ref[...], preferred_element_type=F32), j, o_ref, zr_ref)

    in_tail =i % tiles_per_seq >= tiles_per_seq - tail_tiles
    for cb0, dst in ((CB_K, ko_ref), (CB_V, vo_ref)):
        for cb in range(cb0, cb0 + ATT_W // COL_BLK):
            @pl.when(in_tail & (j == cb))
            def _(cb=cb, cb0=cb0, dst=dst):
                for hh in range(HEADS_PER_COL_BLK):
                    head = (cb - cb0) * HEADS_PER_COL_BLK + hh
                    dst[pl.ds(head, o_ref.shape[0], stride=N_ATT_HEADS), :] = (
                        o_ref[:, hh * HEAD_DIM:(hh + 1) * HEAD_DIM])


def _proj_kv(x, g, w, cache_k, cache_v, prev, *, li, seq, l_keep, shift):
    m, d = x.shape
    n = w.shape[2]
    bsz = m // seq
    bm, bn = PROJ_BM, COL_BLK
    tiles_per_seq, tail_tiles = seq // bm, l_keep // bm
    assert seq % bm == 0 and l_keep % bm == 0
    aliased = prev is not None
    kv_shape = jax.ShapeDtypeStruct((DEPTH, bsz, l_keep * N_ATT_HEADS, HEAD_DIM), F32)
    cache_shape = jax.ShapeDtypeStruct(cache_k.shape, F32)
    any_spec = pl.BlockSpec(memory_space=pl.ANY)
    n_steps = (m // bm) * (n // bn)
    n_seq, l_buf = cache_k.shape[1], cache_k.shape[2]
    rb = COPY_RB
    blks_per_seq = l_buf // rb
    n_copy = n_seq * blks_per_seq
    v_start = n_steps - n_copy
    assert l_buf % rb == 0 and rb % shift == 0 and 0 <= v_start

    def kv_map(i, j):
        return (li, i // tiles_per_seq, jnp.maximum(i % tiles_per_seq - (tiles_per_seq - tail_tiles), 0), 0)

    def copy_specs(first):
        def blk(i, j):
            return jnp.clip(i * (n // bn) + j - first, 0, n_copy - 1)

        def main(i, j):
            c = blk(i, j)
            return (li, c // blks_per_seq, c % blks_per_seq, 0, 0)

        def nxt(i, j):
            c = jnp.minimum(blk(i, j) + 1, n_copy - 1)
            return (li, c // blks_per_seq, (c % blks_per_seq) * (rb // shift), 0, 0)

        return (pl.BlockSpec((None, None, rb, N_ATT_HEADS, HEAD_DIM), main),
                pl.BlockSpec((None, None, shift, N_ATT_HEADS, HEAD_DIM), nxt))

    k_main, k_next = copy_specs(0)
    v_main, v_next = copy_specs(v_start)
    kv_spec = pl.BlockSpec((None, None, bm * N_ATT_HEADS, HEAD_DIM), kv_map)
    in_specs = [
        pl.BlockSpec((bm, d), lambda i, j: (i, 0)),
        pl.BlockSpec((1, d), lambda i, j: (0, 0)),
        pl.BlockSpec((None, d, bn), lambda i, j: (li, 0, j)),
        k_main, k_next, v_main, v_next,
    ]
    args = [x, g.reshape(1, d), w, cache_k, cache_k, cache_v, cache_v]
    if aliased:
        in_specs += [any_spec] * 4
        args += list(prev)
    z_specs, z_shapes = _split_out(m, bm, bn)
    return pl.pallas_call(
        functools.partial(_proj_kv_kernel, tiles_per_seq=tiles_per_seq, tail_tiles=tail_tiles,
                          aliased=aliased, shift=shift, n_copy=n_copy, v_start=v_start),
        grid=(m // bm, n // bn),
        in_specs=in_specs,
        out_specs=z_specs + [kv_spec, kv_spec, k_main, v_main],
        out_shape=z_shapes + [kv_shape, kv_shape, cache_shape, cache_shape],
        scratch_shapes=[pltpu.VMEM((bm, d), BF16)],
        input_output_aliases={7: 2, 8: 3, 9: 4, 10: 5} if aliased else {},
        compiler_params=pltpu.CompilerParams(
            dimension_semantics=("arbitrary", "arbitrary"), vmem_limit_bytes=VMEM_LIMIT),
        name="proj_kv",
    )(*args)


def _qk(q, k):
    return lax.dot_general(q, k, (((1,), (1,)), ((), ())), preferred_element_type=F32)


def _aligned(x, m):
    return x if isinstance(x, int) else pl.multiple_of(x, m)


def _prompt_attn_kernel(q_ref, k_ref, v_ref, bn_ref, bf_ref, o_ref,
                        qb_ref, kb_ref, vb_ref, qf_ref, kf_ref, vf_ref, of_ref, lf_ref, tmp_ref, *, seq):
    sub = seq // FAR_D
    chunk = 2 * Q_BLK
    val = slice(0, HEAD_DIM)

    def cast_body(c, carry):
        r0 = pl.multiple_of(c * chunk, chunk)
        rows = pl.ds(r0, chunk)
        qb_ref[rows, :] = (q_ref[rows, :] * (SCALE * LOG2E)).astype(BF16)
        kb_ref[rows, :] = k_ref[rows, :].astype(BF16)
        vb_ref[rows, val] = v_ref[rows, :].astype(BF16)
        vb_ref[rows, HEAD_DIM:] = jnp.ones((chunk, HEAD_DIM), BF16)
        vf_ref[rows, HEAD_DIM:] = jnp.ones((chunk, HEAD_DIM), BF16)
        return carry

    lax.fori_loop(0, seq // chunk, cast_body, 0)

    quarter = seq // FAR_STEP
    for src_ref, dst_ref, scale in ((q_ref, qf_ref, SCALE * LOG2E), (k_ref, kf_ref, None),
                                    (v_ref, vf_ref, None)):
        for r1 in range(FAR_STEP):
            tmp_ref[r1 * quarter:(r1 + 1) * quarter, :] = src_ref[pl.ds(r1, quarter, stride=FAR_STEP), :]
        for r1 in range(FAR_STEP):
            for r2 in range(FAR_STEP):
                r = FAR_STEP * r2 + r1
                x = tmp_ref[pl.ds(r1 * quarter + r2, sub, stride=FAR_STEP), :]
                if scale is not None:
                    x = x * scale
                dst_ref[r * sub:(r + 1) * sub, val] = x.astype(BF16)

    def far_tiles(tiles):
        geo = []
        for r, a_blk in tiles:
            n_kt = min(a_blk, 1) + 1
            q0 = pl.multiple_of(r * sub + a_blk * Q_BLK, Q_BLK)
            k0 = pl.multiple_of(r * sub + (a_blk + 1 - n_kt) * Q_BLK, Q_BLK)
            dst = pl.ds(a_blk * Q_BLK * FAR_D + r, Q_BLK, stride=FAR_D)
            geo.append((pl.ds(q0, Q_BLK), pl.ds(k0, n_kt * Q_BLK), n_kt, dst))
        scores = [_qk(qf_ref[rows, :], kf_ref[keys, :]) + bf_ref[:, (2 - n_kt) * Q_BLK:]
                  for rows, keys, n_kt, _ in geo]
        probs = []
        for s in scores:
            m = jnp.max(s, axis=-1, keepdims=True)
            probs.append((jnp.exp2(s - m).astype(BF16), m))
        for (_, keys, _, dst), (p, m) in zip(geo, probs):
            o = jnp.dot(p, vf_ref[keys, :], preferred_element_type=F32)
            l = o[:, HEAD_DIM:]
            of_ref[dst, :] = o[:, val] / l
            lf_ref[dst, :] = m + jnp.log2(l)

    def far_body(g, carry):
        far_tiles([(g * FAR_GROUP + r_off, a_blk)
                   for r_off in range(FAR_GROUP) for a_blk in range(sub // Q_BLK)])
        return carry

    lax.fori_loop(0, FAR_D // FAR_GROUP, far_body, 0)

    def near_tiles(tiles):
        geo = []
        for i, n_kt in tiles:
            q0 = _aligned(i * Q_BLK, Q_BLK)
            k0 = _aligned((i + 1 - n_kt) * Q_BLK, Q_BLK)
            geo.append((pl.ds(q0, Q_BLK), pl.ds(k0, n_kt * Q_BLK), n_kt))
        scores = [_qk(qb_ref[rows, :], kb_ref[keys, :]) + bn_ref[:, (NEAR_KBLK - n_kt) * Q_BLK:]
                  for rows, keys, n_kt in geo]
        probs = []
        for (rows, _, _), s in zip(geo, scores):
            m_far = lf_ref[rows, 0:1]
            m = jnp.maximum(jnp.max(s, axis=-1, keepdims=True), m_far)
            probs.append((jnp.exp2(s - m).astype(BF16), jnp.exp2(m_far - m)))
        for (rows, keys, _), (p, w_far) in zip(geo, probs):
            o = jnp.dot(p, vb_ref[keys, :], preferred_element_type=F32)
            acc = w_far * of_ref[rows, :] + o[:, val]
            o_ref[rows, :] = acc / (w_far + o[:, HEAD_DIM:])

    n_blk = seq // Q_BLK
    n_edge = NEAR_KBLK - 1
    assert n_blk > n_edge and (n_blk - n_edge) % NEAR_GROUP == 0
    near_tiles([(i, i + 1) for i in range(n_edge)])

    def near_body(g, carry):
        near_tiles([(n_edge + g * NEAR_GROUP + off, NEAR_KBLK) for off in range(NEAR_GROUP)])
        return carry

    lax.fori_loop(0, (n_blk - n_edge) // NEAR_GROUP, near_body, 0)


def _prompt_attn(z, bias_near, bias_far):
    b, s, _ = z.shape
    assert s % (FAR_D * Q_BLK) == 0

    def head_cols(cb):
        return pl.BlockSpec((None, s, HEAD_DIM), lambda bi, h: (bi, 0, cb * HEADS_PER_COL_BLK + h))

    return pl.pallas_call(
        functools.partial(_prompt_attn_kernel, seq=s),
        grid=(b, N_ATT_HEADS),
        in_specs=[
            head_cols(CB_Q), head_cols(CB_K), head_cols(CB_V),
            pl.BlockSpec((None, Q_BLK, NEAR_KBLK * Q_BLK), lambda bi, h: (h, 0, 0)),
            pl.BlockSpec((None, Q_BLK, 2 * Q_BLK), lambda bi, h: (h, 0, 0)),
        ],
        out_specs=pl.BlockSpec((None, s, HEAD_DIM), lambda bi, h: (bi, 0, h)),
        out_shape=jax.ShapeDtypeStruct((b, s, ATT_W), F32),
        scratch_shapes=[pltpu.VMEM((s, HEAD_DIM), BF16), pltpu.VMEM((s, HEAD_DIM), BF16),
                        pltpu.VMEM((s, 2 * HEAD_DIM), BF16)] * 2 + [pltpu.VMEM((s, HEAD_DIM), F32)] * 3,
        compiler_params=pltpu.CompilerParams(
            dimension_semantics=("parallel", "parallel"), vmem_limit_bytes=VMEM_LIMIT),
        name="prompt_attn",
    )(z, z, z, bias_near, bias_far)


def _sample_attn_kernel(q_ref, kn_ref, vn_ref, kc_ref, vc_ref, bias_ref, biasn_ref, sk_hbm, sv_hbm,
                        o_ref, ko_ref, vo_ref, qa_ref, kn3_ref, vn3_ref, m_ref, l_ref, acc_ref, *, t_len, rb):
    del sk_hbm, sv_hbm
    j = pl.program_id(1)
    last = pl.num_programs(1) - 1
    n_heads = N_ATT_HEADS

    @pl.when(j == 0)
    def _():
        for h in range(n_heads):
            cols = slice(h * HEAD_DIM, (h + 1) * HEAD_DIM)
            qa_ref[h * t_len:(h + 1) * t_len, :] = q_ref[:, cols] * SCALE
            kn3_ref[:, h, :] = kn_ref[:, cols]
            vn3_ref[:, h, :] = vn_ref[:, cols]
        m_ref[...] = jnp.full(m_ref.shape, NEG, F32)
        l_ref[...] = jnp.zeros(l_ref.shape, F32)
        acc_ref[...] = jnp.zeros(acc_ref.shape, F32)

    qa = qa_ref[...].astype(BF16)

    def accumulate(k, v, bias):
        s = _qk(qa, k.astype(BF16)) + bias
        m_old = m_ref[...]
        m_new = jnp.maximum(m_old, jnp.max(s, axis=-1, keepdims=True))
        alpha = jnp.exp(m_old - m_new)
        p = jnp.exp(s - m_new)
        l_ref[...] = alpha * l_ref[...] + jnp.sum(p, axis=-1, keepdims=True)
        acc_ref[...] = alpha * acc_ref[...] + jnp.dot(p.astype(BF16), v.astype(BF16),
                                                      preferred_element_type=F32)
        m_ref[...] = m_new

    accumulate(kc_ref[...].reshape(rb * n_heads, HEAD_DIM), vc_ref[...].reshape(rb * n_heads, HEAD_DIM),
               bias_ref[...])

    @pl.when(j == last)
    def _():
        ko_ref[...] = kn3_ref[...]
        vo_ref[...] = vn3_ref[...]
        accumulate(kn3_ref[...].reshape(t_len * n_heads, HEAD_DIM),
                   vn3_ref[...].reshape(t_len * n_heads, HEAD_DIM), biasn_ref[...])
        o = acc_ref[...] / l_ref[...]
        for h in range(n_heads):
            o_ref[:, h * HEAD_DIM:(h + 1) * HEAD_DIM] = o[h * t_len:(h + 1) * t_len, :]


def _sample_attn(z, cache_k, cache_v, bias, bias_new, stack_k, stack_v, *, li):
    b, t_len, _ = z.shape
    l_buf = cache_k.shape[2]
    rb = SAMPLE_RB
    assert t_len == SUBLANES and l_buf % rb == 0
    rows = N_ATT_HEADS * t_len

    def zspec(cb):
        return pl.BlockSpec((None, t_len, ATT_W), lambda bi, j: (bi, 0, cb * COL_BLK // ATT_W))

    blk_spec = pl.BlockSpec((None, None, rb, N_ATT_HEADS, HEAD_DIM), lambda bi, j: (li, bi, j, 0, 0))
    new_spec = pl.BlockSpec((None, None, t_len, N_ATT_HEADS, HEAD_DIM),
                            lambda bi, j: (li, bi, l_buf // t_len - 1, 0, 0))
    any_spec = pl.BlockSpec(memory_space=pl.ANY)
    in_specs = [
        zspec(CB_Q), zspec(CB_K), zspec(CB_V), blk_spec, blk_spec,
        pl.BlockSpec((rows, rb * N_ATT_HEADS), lambda bi, j: (0, j)),
        pl.BlockSpec((rows, rows), lambda bi, j: (0, 0)),
        any_spec, any_spec,
    ]
    args = [z, z, z, cache_k, cache_v, bias, bias_new, stack_k, stack_v]
    return pl.pallas_call(
        functools.partial(_sample_attn_kernel, t_len=t_len, rb=rb),
        grid=(b, l_buf // rb),
        in_specs=in_specs,
        out_specs=[pl.BlockSpec((None, t_len, ATT_W), lambda bi, j: (bi, 0, 0)), new_spec, new_spec],
        out_shape=[
            jax.ShapeDtypeStruct((b, t_len, ATT_W), F32),
            jax.ShapeDtypeStruct(cache_k.shape, F32),
            jax.ShapeDtypeStruct(cache_v.shape, F32),
        ],
        scratch_shapes=[
            pltpu.VMEM((rows, HEAD_DIM), F32),
            pltpu.VMEM((t_len, N_ATT_HEADS, HEAD_DIM), F32), pltpu.VMEM((t_len, N_ATT_HEADS, HEAD_DIM), F32),
            pltpu.VMEM((rows, 1), F32), pltpu.VMEM((rows, 1), F32), pltpu.VMEM((rows, HEAD_DIM), F32),
        ],
        input_output_aliases={7: 1, 8: 2},
        compiler_params=pltpu.CompilerParams(
            dimension_semantics=("arbitrary", "arbitrary"), vmem_limit_bytes=VMEM_LIMIT),
        name="sample_attn",
    )(*args)


def _silu(x):
    return x * jax.nn.sigmoid(x)


def _mix_kernel(x_ref, a_ref, ga_ref, uv_ref, ug_ref, gc_ref, qm_ref, gm_ref, cinit_ref, mk_ref, mv_ref,
                wdw_ref, bdw_ref, lng_ref, lnb_ref, wpw_ref, wout_ref, gpost_ref,
                xo_ref, cs_ref, uext_ref, mix_ref, *, nb, tile, conv_chunk):
    @pl.when(pl.program_id(1) == 0)
    def _():
        uext_ref[:, 0:HALO, :] = cinit_ref[...]

    first = HALO - (CONV_K - 1)
    chunks = []
    for b in range(nb):
        ub_ref = uext_ref.at[b]
        ub_ref[HALO:HALO + tile, :] = uv_ref[b].astype(F32) * jax.nn.sigmoid(ug_ref[b].astype(F32))
        for c0 in range(0, tile, conv_chunk):
            acc = jnp.broadcast_to(bdw_ref[...], (conv_chunk, CONV_CH))
            for phase in range(SUBLANES):
                rows = conv_chunk + (SUBLANES if phase else 0)
                part = None
                for k in range(CONV_K):
                    if (first + k) % SUBLANES == phase:
                        base = c0 + first + k - phase
                        term = ub_ref[base:base + rows, :] * wdw_ref[k:k + 1, :]
                        part = term if part is None else part + term
                if part is not None:
                    acc = acc + part[phase:phase + conv_chunk, :]
            chunks.append(acc)
        cs_ref[b] = ub_ref[tile + first:tile + HALO, :]
        tail = ub_ref[tile:tile + HALO, :]
        ub_ref[0:HALO, :] = tail
    c = jnp.concatenate(chunks, axis=0) if len(chunks) > 1 else chunks[0]

    mu = jnp.mean(c, axis=-1, keepdims=True)
    var = jnp.mean(jnp.square(c - mu), axis=-1, keepdims=True)
    c = (c - mu) * lax.rsqrt(var + EPS) * lng_ref[...] + lnb_ref[...]
    c = jnp.dot(_silu(c).astype(BF16), wpw_ref[...], preferred_element_type=F32)

    def mem_head(ref, h):
        if len(ref.shape) == 3:
            return ref[:, h, :].astype(BF16)
        return ref[:, h * HEAD_DIM:(h + 1) * HEAD_DIM].astype(BF16)

    for b in range(nb):
        rows = slice(b * tile, (b + 1) * tile)
        mix_ref[rows, 0:ATT_W] = (a_ref[b] * _silu(ga_ref[b].astype(F32))).astype(BF16)
        mix_ref[rows, ATT_W:ATT_W + CONV_CH] = (c[rows] * _silu(gc_ref[b].astype(F32))).astype(BF16)
        for h in range(N_X_HEADS):
            cols = slice(h * HEAD_DIM, (h + 1) * HEAD_DIM)
            q = (qm_ref[b, :, cols].astype(F32) * SCALE).astype(BF16)
            s = _qk(q, mem_head(mk_ref.at[b], h))
            p = jnp.exp(s - jnp.max(s, axis=-1, keepdims=True))
            o = jnp.dot(p.astype(BF16), mem_head(mv_ref.at[b], h), preferred_element_type=F32)
            o = o / jnp.sum(p, axis=-1, keepdims=True)
            off = ATT_W + CONV_CH + h * HEAD_DIM
            mix_ref[rows, off:off + HEAD_DIM] = (o * _silu(gm_ref[b, :, cols].astype(F32))).astype(BF16)

    y = jnp.dot(mix_ref[...], wout_ref[...], preferred_element_type=F32)
    y = y * lax.rsqrt(jnp.mean(y * y, axis=-1, keepdims=True) + EPS) * gpost_ref[...]
    for b in range(nb):
        xo_ref[b] = x_ref[b] + y[b * tile:(b + 1) * tile]


def _mix(x, z, a, cinit, mk_arr, mk_block, mv_arr, mv_block, w_dw, b_dw, ln_g, ln_b, w_pw2, w_out, g_post,
         *, li, nb, tile):
    b, s, _ = x.shape
    conv_chunk = min(tile, 32)
    assert b % nb == 0 and s % tile == 0

    def rows(width, cb=0):
        return pl.BlockSpec((nb, tile, width), lambda bi, t: (bi, t, cb * COL_BLK // width))

    def rest(width, cb):
        return rows(width, cb - N_QKV_BLK)

    def const(shape):
        return pl.BlockSpec(shape, lambda bi, t: (0,) * len(shape))

    def layer(shape):
        return pl.BlockSpec((None,) + shape, lambda bi, t: (li,) + (0,) * len(shape))

    def mem(block):
        shape, index = block
        return pl.BlockSpec(shape, lambda bi, t: index(bi))

    return pl.pallas_call(
        functools.partial(_mix_kernel, nb=nb, tile=tile, conv_chunk=conv_chunk),
        grid=(b // nb, s // tile),
        in_specs=[
            rows(D_MODEL), rows(ATT_W),
            rest(ATT_W, CB_GA), rest(COL_BLK, CB_UV), rest(COL_BLK, CB_UG), rest(COL_BLK, CB_GC),
            rest(COL_BLK, CB_QM), rest(COL_BLK, CB_GM),
            pl.BlockSpec((nb, HALO, CONV_CH), lambda bi, t: (bi, 0, 0)),
            mem(mk_block), mem(mv_block),
            const((CONV_K, CONV_CH)), const((1, CONV_CH)), const((1, CONV_CH)), const((1, CONV_CH)),
            layer((CONV_CH, CONV_CH)), layer((MIX_W, D_MODEL)), const((1, D_MODEL)),
        ],
        out_specs=[
            rows(D_MODEL),
            pl.BlockSpec((nb, CONV_K - 1, CONV_CH), lambda bi, t: (bi, 0, 0)),
        ],
        out_shape=[
            jax.ShapeDtypeStruct((b, s, D_MODEL), F32),
            jax.ShapeDtypeStruct((b, CONV_K - 1, CONV_CH), F32),
        ],
        scratch_shapes=[pltpu.VMEM((nb, HALO + tile, CONV_CH), F32), pltpu.VMEM((nb * tile, MIX_W), BF16)],
        compiler_params=pltpu.CompilerParams(
            dimension_semantics=("parallel", "arbitrary"), vmem_limit_bytes=VMEM_LIMIT),
        name="mix",
    )(x, a, z, z, z, z, z, z, cinit, mk_arr, mv_arr,
      w_dw, b_dw.reshape(1, -1), ln_g.reshape(1, -1), ln_b.reshape(1, -1), w_pw2, w_out,
      g_post.reshape(1, -1))


def kernel(x_prompt, x_sample, mem_prompt, cache_attn_k, cache_attn_v, state_conv, cache_mem_k, cache_mem_v,
           rel_bias, norm_pre_g, w_in, w_dw, b_dw, ln_conv_g, ln_conv_b, w_pw2, w_mem_kv, w_out, norm_post_g):
    bp, s_len, _ = x_prompt.shape
    bs, t_len, _ = x_sample.shape
    l_buf = cache_attn_k.shape[2]
    l_prompt = min(WIN, s_len)

    w_in_b, w_pw2_b = w_in.astype(BF16), w_pw2.astype(BF16)
    w_mem_b, w_out_b = w_mem_kv.astype(BF16), w_out.astype(BF16)

    tab_near = _distance_table(rel_bias, np.arange(NEAR_REACH + 1), NEAR_PATTERNS)
    tab_far = _distance_table(rel_bias, FAR_D * np.arange(FAR_KEYS + 1), DIL_PATTERNS[-1:])
    bias_near = _toeplitz(tab_near * LOG2E, NEAR_REACH, Q_BLK, NEAR_KBLK * Q_BLK)
    bias_far = _toeplitz(tab_far * LOG2E, FAR_KEYS, Q_BLK, 2 * Q_BLK)
    tab_all = _distance_table(rel_bias, np.arange(WIN + 1), DIL_PATTERNS)
    bias_s = _head_matched_bias(tab_all, l_buf, t_len, l_buf)
    bias_s_new = _head_matched_bias(tab_all, 0, t_len, t_len)

    mkv = _mem_proj(mem_prompt.reshape(bp * N_MEM, D_MODEL), w_mem_b).reshape(DEPTH, bp, N_MEM, 2 * X_W)
    cinit_p = jnp.zeros((bp, HALO, CONV_CH), F32)
    pad_s = jnp.zeros((DEPTH, bs, HALO - (CONV_K - 1), CONV_CH), F32)
    cinit_s = jnp.concatenate([pad_s, state_conv], axis=2)
    xp, xs = x_prompt, x_sample
    stacks = None
    cvp, cvs = [], []
    for li in range(DEPTH):
        wargs = (w_dw[li], b_dw[li], ln_conv_g[li], ln_conv_b[li], w_pw2_b, w_out_b, norm_post_g[li])

        zq, zr, akp, avp, aks, avs = _proj_kv(
            xp.reshape(bp * s_len, D_MODEL), norm_pre_g[li], w_in_b, cache_attn_k, cache_attn_v, stacks,
            li=li, seq=s_len, l_keep=l_prompt, shift=t_len)
        a = _prompt_attn(zq.reshape(bp, s_len, QKV_W), bias_near, bias_far)
        mk_block, mv_block = (((None, 1, N_MEM, X_W), lambda bi, li=li, half=half: (li, bi, 0, half))
                              for half in (0, 1))
        xp, cst = _mix(xp, zr.reshape(bp, s_len, REST_W), a, cinit_p, mkv, mk_block, mkv, mv_block, *wargs,
                       li=li, nb=1, tile=256)
        cvp.append(cst)

        zq, zr = _proj(xs.reshape(bs * t_len, D_MODEL), norm_pre_g[li], w_in_b, li=li,
                       bm=bs * t_len, bn=COL_BLK)
        a, aks, avs = _sample_attn(zq.reshape(bs, t_len, QKV_W), cache_attn_k, cache_attn_v,
                                   bias_s, bias_s_new, aks, avs, li=li)
        stacks = (akp, avp, aks, avs)
        mem_block = ((None, bs, N_MEM, N_X_HEADS, HEAD_DIM), lambda bi, li=li: (li, bi, 0, 0, 0))
        xs, cst = _mix(xs, zr.reshape(bs, t_len, REST_W), a, cinit_s[li], cache_mem_k, mem_block,
                       cache_mem_v, mem_block, *wargs, li=li, nb=bs, tile=t_len)
        cvs.append(cst)

    kv5 = (DEPTH, bp, l_prompt, N_ATT_HEADS, HEAD_DIM)
    mem5 = (DEPTH, bp, N_MEM, N_X_HEADS, HEAD_DIM)
    return (xp, xs, akp.reshape(kv5), avp.reshape(kv5), jnp.stack(cvp),
            mkv[..., :X_W].reshape(mem5), mkv[..., X_W:].reshape(mem5), aks, avs, jnp.stack(cvs))
```

```python
import functools

import numpy as np
import jax
import jax.numpy as jnp
from jax import lax
from jax.experimental import pallas as pl
from jax.experimental.pallas import tpu as pltpu

D_MODEL = 2048
DEPTH = 4
N_MEM = 256
HEAD_DIM = 128
ATT_W = D_MODEL // 2
N_ATT_HEADS = ATT_W // HEAD_DIM
DIL_PATTERNS = ((128, 1), (512, 4), (2048, 16))
WIN = max(w for w, _ in DIL_PATTERNS)
N_BUCKETS = 32
MAX_DIST = WIN
CONV_CH = D_MODEL // 4
CONV_K = 31
X_W = D_MODEL // 4
N_X_HEADS = X_W // HEAD_DIM
MIX_W = ATT_W + CONV_CH + X_W
IN_W = 4 * ATT_W + 3 * CONV_CH + 2 * X_W
EPS = 1e-6
NEG = -1e30
SCALE = HEAD_DIM ** -0.5
LOG2E = 1.4426950408889634

COL_BLK = 512
CB_Q, CB_K, CB_V, CB_GA = 0, 2, 4, 6
CB_UV, CB_UG, CB_GC, CB_QM, CB_GM = 8, 9, 10, 11, 12
HEADS_PER_COL_BLK = COL_BLK // HEAD_DIM

Q_BLK = 128
FAR_W, FAR_D = DIL_PATTERNS[-1]
NEAR_PATTERNS = DIL_PATTERNS[:-1]
NEAR_REACH = max(w for w, _ in NEAR_PATTERNS)
NEAR_KBLK = NEAR_REACH // Q_BLK + 1
FAR_KEYS = FAR_W // FAR_D
FAR_STEP = 4
assert FAR_KEYS == Q_BLK and NEAR_REACH % Q_BLK == 0 and FAR_STEP * FAR_STEP == FAR_D
FAR_GROUP = 16
NEAR_GROUP = 28
PROJ_BM = 1024
COPY_RB = 256
SAMPLE_RB = 1024
HALO = 32
SUBLANES = 8
LANES = 128
VMEM_LIMIT = 56 * 1024 * 1024

BF16 = jnp.bfloat16
F32 = jnp.float32


def _t5_bucket(dist):
    dist = np.asarray(dist)
    max_exact = N_BUCKETS // 2
    large = max_exact + (np.log(np.maximum(dist, 1) / max_exact)
                         / np.log(MAX_DIST / max_exact) * (N_BUCKETS - max_exact)).astype(np.int32)
    large = np.minimum(large, N_BUCKETS - 1)
    return np.where(dist < max_exact, dist, large).astype(np.int32)


def _distance_table(rel_bias, dists, patterns):
    dists = np.asarray(dists)
    cnt = np.zeros(dists.shape, np.int64)
    for w, dil in patterns:
        cnt += ((dists % dil == 0) & (dists <= w)).astype(np.int64)
    logc = jnp.log(jnp.asarray(np.maximum(cnt, 1), F32))
    tab = rel_bias[_t5_bucket(dists)].T.astype(F32) + logc[None, :]
    tab = jnp.where(jnp.asarray(cnt > 0)[None, :], tab, NEG)
    return jnp.concatenate([tab, jnp.full((tab.shape[0], 1), NEG, F32)], axis=1)


def _toeplitz(tab, off, rows, cols):
    n_valid = tab.shape[1] - 1
    lp = rows + cols
    j = np.concatenate([np.arange(cols), np.full(lp - (rows + cols - 1), off + n_valid),
                        np.arange(-(rows - 1), 0)])
    d = off - j
    idx = np.where((d >= 0) & (d < n_valid), d, n_valid)
    v = jnp.take(tab, jnp.asarray(idx, jnp.int32), axis=1)
    flat = jnp.tile(v, (1, rows))[:, :rows * (lp - 1)]
    return flat.reshape(tab.shape[0], rows, lp - 1)[:, :, :cols]


def _head_matched_bias(tab, off, t_len, n_rows):
    n_h, n_valid = tab.shape[0], tab.shape[1] - 1
    n_cols = n_rows * n_h
    gaps = jnp.full((n_h, n_valid, n_h - 1), NEG, F32)
    s = jnp.concatenate([tab[:, :n_valid, None], gaps], axis=2).reshape(n_h, n_valid * n_h)
    k_max = n_h * (off + t_len - 1) + n_h - 1
    pad_l = max(0, n_cols - 1 - n_h * off)
    pad_r = max(0, k_max - (n_valid * n_h - 1))
    s = jnp.concatenate([jnp.full((n_h, pad_l), NEG, F32), s, jnp.full((n_h, pad_r), NEG, F32)], axis=1)
    n_s = s.shape[1]
    lane_pad = -n_s % LANES
    rev = jnp.concatenate([s[:, ::-1], jnp.full((n_h, lane_pad), NEG, F32)], axis=1)
    starts = tuple((h, n_s - 1 - (n_h * (off + t) + h) - pad_l) for h in range(n_h) for t in range(t_len))
    return pl.pallas_call(
        functools.partial(_slice_rows_kernel, starts=starts, n_cols=n_cols),
        out_shape=jax.ShapeDtypeStruct((len(starts), n_cols), F32),
        name="bias_rows",
    )(rev)


def _slice_rows_kernel(src_ref, out_ref, *, starts, n_cols):
    for r, (h, start) in enumerate(starts):
        lo = start // LANES * LANES
        hi = -(-(start + n_cols) // LANES) * LANES
        window = src_ref[h:h + 1, lo:hi]
        out_ref[r:r + 1, :] = window[:, start - lo:start - lo + n_cols]


def _norm_rows(x_ref, g_ref, h_ref, normalize):
    x = x_ref[...]
    if normalize:
        x = x * lax.rsqrt(jnp.mean(x * x, axis=-1, keepdims=True) + EPS) * g_ref[...]
    h_ref[...] = x.astype(BF16)


def _proj_kernel(x_ref, g_ref, w_ref, o_ref, h_ref, *, normalize):
    @pl.when(pl.program_id(1) == 0)
    def _():
        _norm_rows(x_ref, g_ref, h_ref, normalize)

    o_ref[...] = jnp.dot(h_ref[...], w_ref[...], preferred_element_type=F32)


def _proj(x, g, w, *, li, normalize, bm, bn):
    m, d = x.shape
    n = w.shape[2]
    return pl.pallas_call(
        functools.partial(_proj_kernel, normalize=normalize),
        grid=(m // bm, n // bn),
        in_specs=[
            pl.BlockSpec((bm, d), lambda i, j: (i, 0)),
            pl.BlockSpec((1, d), lambda i, j: (0, 0)),
            pl.BlockSpec((None, d, bn), lambda i, j: (li, 0, j)),
        ],
        out_specs=pl.BlockSpec((bm, bn), lambda i, j: (i, j)),
        out_shape=jax.ShapeDtypeStruct((m, n), F32),
        scratch_shapes=[pltpu.VMEM((bm, d), BF16)],
        compiler_params=pltpu.CompilerParams(
            dimension_semantics=("parallel", "arbitrary"), vmem_limit_bytes=VMEM_LIMIT),
        name="proj",
    )(x, g.reshape(1, d), w)


def _mem_proj(x, w):
    m, d = x.shape
    depth, _, n = w.shape
    bn = COL_BLK
    return pl.pallas_call(
        functools.partial(_proj_kernel, normalize=False),
        grid=(depth, n // bn),
        in_specs=[
            pl.BlockSpec((m, d), lambda li, j: (0, 0)),
            pl.BlockSpec((1, d), lambda li, j: (0, 0)),
            pl.BlockSpec((None, d, bn), lambda li, j: (li, 0, j)),
        ],
        out_specs=pl.BlockSpec((None, m, bn), lambda li, j: (li, 0, j)),
        out_shape=jax.ShapeDtypeStruct((depth, m, n), F32),
        scratch_shapes=[pltpu.VMEM((m, d), BF16)],
        compiler_params=pltpu.CompilerParams(
            dimension_semantics=("arbitrary", "arbitrary"), vmem_limit_bytes=VMEM_LIMIT),
        name="mem_proj",
    )(x, jnp.ones((1, d), F32), w)


def _proj_kv_kernel(*refs,tiles_per_seq, tail_tiles, aliased, shift, n_copy, v_start):
    x_ref, g_ref, w_ref, kc_ref, kx_ref, vc_ref, vx_ref = refs[:7]
    o_ref, ko_ref, vo_ref, sk_ref, sv_ref, h_ref = refs[11:] if aliased else refs[7:]
    i, j = pl.program_id(0), pl.program_id(1)
    step = i * pl.num_programs(1) + j
    rb = kc_ref.shape[0]

    for first, cache, nxt, out in ((0, kc_ref, kx_ref, sk_ref), (v_start, vc_ref, vx_ref, sv_ref)):
        @pl.when((step >= first) & (step < first + n_copy))
        def _(cache=cache, nxt=nxt, out=out):
            out[0:rb - shift] = cache[shift:rb]
            out[rb - shift:rb] = nxt[...]

    @pl.when(j == 0)
    def _():
        _norm_rows(x_ref, g_ref, h_ref, True)

    o_ref[...] = jnp.dot(h_ref[...], w_ref[...], preferred_element_type=F32)

    in_tail = i % tiles_per_seq >= tiles_per_seq - tail_tiles
    for cb0, dst in ((CB_K, ko_ref), (CB_V, vo_ref)):
        for cb in range(cb0, cb0 + ATT_W // COL_BLK):
            @pl.when(in_tail & (j == cb))
            def _(cb=cb, cb0=cb0, dst=dst):
                for hh in range(HEADS_PER_COL_BLK):
                    head = (cb - cb0) * HEADS_PER_COL_BLK + hh
                    dst[pl.ds(head, o_ref.shape[0], stride=N_ATT_HEADS), :] = (
                        o_ref[:, hh * HEAD_DIM:(hh + 1) * HEAD_DIM])


def _proj_kv(x, g, w, cache_k, cache_v, prev, *, li, seq, l_keep, shift):
    m, d = x.shape
    n = w.shape[2]
    bsz = m // seq
    bm, bn = PROJ_BM, COL_BLK
    tiles_per_seq, tail_tiles = seq // bm, l_keep // bm
    assert seq % bm == 0 and l_keep % bm == 0
    aliased = prev is not None
    kv_shape = jax.ShapeDtypeStruct((DEPTH, bsz, l_keep * N_ATT_HEADS, HEAD_DIM), F32)
    cache_shape = jax.ShapeDtypeStruct(cache_k.shape, F32)
    any_spec = pl.BlockSpec(memory_space=pl.ANY)
    n_steps = (m // bm) * (n // bn)
    n_seq, l_buf = cache_k.shape[1], cache_k.shape[2]
    rb = COPY_RB
    blks_per_seq = l_buf // rb
    n_copy = n_seq * blks_per_seq
    v_start = n_steps - n_copy
    assert l_buf % rb == 0 and rb % shift == 0 and 0 <= v_start

    def kv_map(i, j):
        return (li, i // tiles_per_seq, jnp.maximum(i % tiles_per_seq - (tiles_per_seq - tail_tiles), 0), 0)

    def copy_specs(first):
        def blk(i, j):
            return jnp.clip(i * (n // bn) + j - first, 0, n_copy - 1)

        def main(i, j):
            c = blk(i, j)
            return (li, c // blks_per_seq, c % blks_per_seq, 0, 0)

        def nxt(i, j):
            c = jnp.minimum(blk(i, j) + 1, n_copy - 1)
            return (li, c // blks_per_seq, (c % blks_per_seq) * (rb // shift), 0, 0)

        return (pl.BlockSpec((None, None, rb, N_ATT_HEADS, HEAD_DIM), main),
                pl.BlockSpec((None, None, shift, N_ATT_HEADS, HEAD_DIM), nxt))

    k_main, k_next = copy_specs(0)
    v_main, v_next = copy_specs(v_start)
    kv_spec = pl.BlockSpec((None, None, bm * N_ATT_HEADS, HEAD_DIM), kv_map)
    in_specs = [
        pl.BlockSpec((bm, d), lambda i, j: (i, 0)),
        pl.BlockSpec((1, d), lambda i, j: (0, 0)),
        pl.BlockSpec((None, d, bn), lambda i, j: (li, 0, j)),
        k_main, k_next, v_main, v_next,
    ]
    args = [x, g.reshape(1, d), w, cache_k, cache_k, cache_v, cache_v]
    if aliased:
        in_specs += [any_spec] * 4
        args += list(prev)
    return pl.pallas_call(
        functools.partial(_proj_kv_kernel, tiles_per_seq=tiles_per_seq, tail_tiles=tail_tiles,
                          aliased=aliased, shift=shift, n_copy=n_copy, v_start=v_start),
        grid=(m // bm, n // bn),
        in_specs=in_specs,
        out_specs=[pl.BlockSpec((bm, bn), lambda i, j: (i, j)), kv_spec, kv_spec, k_main, v_main],
        out_shape=[jax.ShapeDtypeStruct((m, n), F32), kv_shape, kv_shape, cache_shape, cache_shape],
        scratch_shapes=[pltpu.VMEM((bm, d), BF16)],
        input_output_aliases={7: 1, 8: 2, 9: 3, 10: 4} if aliased else {},
        compiler_params=pltpu.CompilerParams(
            dimension_semantics=("arbitrary", "arbitrary"), vmem_limit_bytes=VMEM_LIMIT),
        name="proj_kv",
    )(*args)


def _qk(q, k):
    return lax.dot_general(q, k, (((1,), (1,)), ((), ())), preferred_element_type=F32)


def _aligned(x, m):
    return x if isinstance(x, int) else pl.multiple_of(x, m)


def _prompt_attn_kernel(q_ref, k_ref, v_ref, bn_ref, bf_ref, o_ref,
                        qb_ref, kb_ref, vb_ref, qf_ref, kf_ref, vf_ref, of_ref, lf_ref, tmp_ref, *, seq):
    sub = seq // FAR_D
    chunk = 2 * Q_BLK
    val = slice(0, HEAD_DIM)

    def cast_body(c, carry):
        r0 = pl.multiple_of(c * chunk, chunk)
        rows = pl.ds(r0, chunk)
        qb_ref[rows, :] = (q_ref[rows, :] * (SCALE * LOG2E)).astype(BF16)
        kb_ref[rows, :] = k_ref[rows, :].astype(BF16)
        vb_ref[rows, val] = v_ref[rows, :].astype(BF16)
        vb_ref[rows, HEAD_DIM:] = jnp.ones((chunk, HEAD_DIM), BF16)
        vf_ref[rows, HEAD_DIM:] = jnp.ones((chunk, HEAD_DIM), BF16)
        return carry

    lax.fori_loop(0, seq // chunk, cast_body, 0)

    quarter = seq // FAR_STEP
    for src_ref, dst_ref, scale in ((q_ref, qf_ref, SCALE * LOG2E), (k_ref, kf_ref, None),
                                    (v_ref, vf_ref, None)):
        for r1 in range(FAR_STEP):
            tmp_ref[r1 * quarter:(r1 + 1) * quarter, :] = src_ref[pl.ds(r1, quarter, stride=FAR_STEP), :]
        for r1 in range(FAR_STEP):
            for r2 in range(FAR_STEP):
                r = FAR_STEP * r2 + r1
                x = tmp_ref[pl.ds(r1 * quarter + r2, sub, stride=FAR_STEP), :]
                if scale is not None:
                    x = x * scale
                dst_ref[r * sub:(r + 1) * sub, val] = x.astype(BF16)

    def far_tiles(tiles):
        geo = []
        for r, a_blk in tiles:
            n_kt = min(a_blk, 1) + 1
            q0 = pl.multiple_of(r * sub + a_blk * Q_BLK, Q_BLK)
            k0 = pl.multiple_of(r * sub + (a_blk + 1 - n_kt) * Q_BLK, Q_BLK)
            dst = pl.ds(a_blk * Q_BLK * FAR_D + r, Q_BLK, stride=FAR_D)
            geo.append((pl.ds(q0, Q_BLK), pl.ds(k0, n_kt * Q_BLK), n_kt, dst))
        scores = [_qk(qf_ref[rows, :], kf_ref[keys, :]) + bf_ref[:, (2 - n_kt) * Q_BLK:]
                  for rows, keys, n_kt, _ in geo]
        probs = []
        for s in scores:
            m = jnp.max(s, axis=-1, keepdims=True)
            probs.append((jnp.exp2(s - m).astype(BF16), m))
        for (_, keys, _, dst), (p, m) in zip(geo, probs):
            o = jnp.dot(p, vf_ref[keys, :], preferred_element_type=F32)
            l = o[:, HEAD_DIM:]
            of_ref[dst, :] = o[:, val] / l
            lf_ref[dst, :] = m + jnp.log2(l)

    def far_body(g, carry):
        far_tiles([(g * FAR_GROUP + r_off, a_blk)
                   for r_off in range(FAR_GROUP) for a_blk in range(sub // Q_BLK)])
        return carry

    lax.fori_loop(0, FAR_D // FAR_GROUP, far_body, 0)

    def near_tiles(tiles):
        geo = []
        for i, n_kt in tiles:
            q0 = _aligned(i * Q_BLK, Q_BLK)
            k0 = _aligned((i + 1 - n_kt) * Q_BLK, Q_BLK)
            geo.append((pl.ds(q0, Q_BLK), pl.ds(k0, n_kt * Q_BLK), n_kt))
        scores = [_qk(qb_ref[rows, :], kb_ref[keys, :]) + bn_ref[:, (NEAR_KBLK - n_kt) * Q_BLK:]
                  for rows, keys, n_kt in geo]
        probs = []
        for (rows, _, _), s in zip(geo, scores):
            m_far = lf_ref[rows, 0:1]
            m = jnp.maximum(jnp.max(s, axis=-1, keepdims=True), m_far)
            probs.append((jnp.exp2(s - m).astype(BF16), jnp.exp2(m_far - m)))
        for (rows, keys, _), (p, w_far) in zip(geo, probs):
            o = jnp.dot(p, vb_ref[keys, :], preferred_element_type=F32)
            acc = w_far * of_ref[rows, :] + o[:, val]
            o_ref[rows, :] = acc / (w_far + o[:, HEAD_DIM:])

    n_blk = seq // Q_BLK
    n_edge = NEAR_KBLK - 1
    assert n_blk > n_edge and (n_blk - n_edge) % NEAR_GROUP == 0
    near_tiles([(i, i + 1) for i in range(n_edge)])

    def near_body(g, carry):
        near_tiles([(n_edge + g * NEAR_GROUP + off, NEAR_KBLK) for off in range(NEAR_GROUP)])
        return carry

    lax.fori_loop(0, (n_blk - n_edge) // NEAR_GROUP, near_body, 0)


def _prompt_attn(z, bias_near, bias_far):
    b, s, _ = z.shape
    assert s % (FAR_D * Q_BLK) == 0

    def head_cols(cb):
        return pl.BlockSpec((None, s, HEAD_DIM), lambda bi, h: (bi, 0, cb * HEADS_PER_COL_BLK + h))

    return pl.pallas_call(
        functools.partial(_prompt_attn_kernel, seq=s),
        grid=(b, N_ATT_HEADS),
        in_specs=[
            head_cols(CB_Q), head_cols(CB_K), head_cols(CB_V),
            pl.BlockSpec((None, Q_BLK, NEAR_KBLK * Q_BLK), lambda bi, h: (h, 0, 0)),
            pl.BlockSpec((None, Q_BLK, 2 * Q_BLK), lambda bi, h: (h, 0, 0)),
        ],
        out_specs=pl.BlockSpec((None, s, HEAD_DIM), lambda bi, h: (bi, 0, h)),
        out_shape=jax.ShapeDtypeStruct((b, s, ATT_W), F32),
        scratch_shapes=[pltpu.VMEM((s, HEAD_DIM), BF16), pltpu.VMEM((s, HEAD_DIM), BF16),
                        pltpu.VMEM((s, 2 * HEAD_DIM), BF16)] * 2 + [pltpu.VMEM((s, HEAD_DIM), F32)] * 3,
        compiler_params=pltpu.CompilerParams(
            dimension_semantics=("parallel", "parallel"), vmem_limit_bytes=VMEM_LIMIT),
        name="prompt_attn",
    )(z, z, z, bias_near, bias_far)


def _sample_attn_kernel(q_ref, kn_ref, vn_ref, kc_ref, vc_ref, bias_ref, biasn_ref, sk_hbm, sv_hbm,
                        o_ref, ko_ref, vo_ref, qa_ref, kn3_ref, vn3_ref, m_ref, l_ref, acc_ref, *, t_len, rb):
    del sk_hbm, sv_hbm
    j = pl.program_id(1)
    last = pl.num_programs(1) - 1
    n_heads = N_ATT_HEADS

    @pl.when(j == 0)
    def _():
        for h in range(n_heads):
            cols = slice(h * HEAD_DIM, (h + 1) * HEAD_DIM)
            qa_ref[h * t_len:(h + 1) * t_len, :] = q_ref[:, cols] * SCALE
            kn3_ref[:, h, :] = kn_ref[:, cols]
            vn3_ref[:, h, :] = vn_ref[:, cols]
        m_ref[...] = jnp.full(m_ref.shape, NEG, F32)
        l_ref[...] = jnp.zeros(l_ref.shape, F32)
        acc_ref[...] = jnp.zeros(acc_ref.shape, F32)

    qa = qa_ref[...].astype(BF16)

    def accumulate(k, v, bias):
        s = _qk(qa, k.astype(BF16)) + bias
        m_old = m_ref[...]
        m_new = jnp.maximum(m_old, jnp.max(s, axis=-1, keepdims=True))
        alpha = jnp.exp(m_old - m_new)
        p = jnp.exp(s - m_new)
        l_ref[...] = alpha * l_ref[...] + jnp.sum(p, axis=-1, keepdims=True)
        acc_ref[...] = alpha * acc_ref[...] + jnp.dot(p.astype(BF16), v.astype(BF16),
                                                      preferred_element_type=F32)
        m_ref[...] = m_new

    accumulate(kc_ref[...].reshape(rb * n_heads, HEAD_DIM), vc_ref[...].reshape(rb * n_heads, HEAD_DIM),
               bias_ref[...])

    @pl.when(j == last)
    def _():
        ko_ref[...] = kn3_ref[...]
        vo_ref[...] = vn3_ref[...]
        accumulate(kn3_ref[...].reshape(t_len * n_heads, HEAD_DIM),
                   vn3_ref[...].reshape(t_len * n_heads, HEAD_DIM), biasn_ref[...])
        o = acc_ref[...] / l_ref[...]
        for h in range(n_heads):
            o_ref[:, h * HEAD_DIM:(h + 1) * HEAD_DIM] = o[h * t_len:(h + 1) * t_len, :]


def _sample_attn(z, cache_k, cache_v, bias, bias_new, stack_k, stack_v, *, li):
    b, t_len, _ = z.shape
    l_buf = cache_k.shape[2]
    rb = SAMPLE_RB
    assert t_len == SUBLANES and l_buf % rb == 0
    rows = N_ATT_HEADS * t_len

    def zspec(cb):
        return pl.BlockSpec((None, t_len, ATT_W), lambda bi, j: (bi, 0, cb * COL_BLK // ATT_W))

    blk_spec = pl.BlockSpec((None, None, rb, N_ATT_HEADS, HEAD_DIM), lambda bi, j: (li, bi, j, 0, 0))
    new_spec = pl.BlockSpec((None, None, t_len, N_ATT_HEADS, HEAD_DIM),
                            lambda bi, j: (li, bi, l_buf // t_len - 1, 0, 0))
    any_spec = pl.BlockSpec(memory_space=pl.ANY)
    in_specs = [
        zspec(CB_Q), zspec(CB_K), zspec(CB_V), blk_spec, blk_spec,
        pl.BlockSpec((rows, rb * N_ATT_HEADS), lambda bi, j: (0, j)),
        pl.BlockSpec((rows, rows), lambda bi, j: (0, 0)),
        any_spec, any_spec,
    ]
    args = [z, z, z, cache_k, cache_v, bias, bias_new, stack_k, stack_v]
    return pl.pallas_call(
        functools.partial(_sample_attn_kernel, t_len=t_len, rb=rb),
        grid=(b, l_buf // rb),
        in_specs=in_specs,
        out_specs=[pl.BlockSpec((None, t_len, ATT_W), lambda bi, j: (bi, 0, 0)), new_spec, new_spec],
        out_shape=[
            jax.ShapeDtypeStruct((b, t_len, ATT_W), F32),
            jax.ShapeDtypeStruct(cache_k.shape, F32),
            jax.ShapeDtypeStruct(cache_v.shape, F32),
        ],
        scratch_shapes=[
            pltpu.VMEM((rows, HEAD_DIM), F32),
            pltpu.VMEM((t_len, N_ATT_HEADS, HEAD_DIM), F32), pltpu.VMEM((t_len, N_ATT_HEADS, HEAD_DIM), F32),
            pltpu.VMEM((rows, 1), F32), pltpu.VMEM((rows, 1), F32), pltpu.VMEM((rows, HEAD_DIM), F32),
        ],
        input_output_aliases={7: 1, 8: 2},
        compiler_params=pltpu.CompilerParams(
            dimension_semantics=("arbitrary", "arbitrary"), vmem_limit_bytes=VMEM_LIMIT),
        name="sample_attn",
    )(*args)


def _silu(x):
    return x * jax.nn.sigmoid(x)


def _mix_kernel(x_ref, a_ref, ga_ref, uv_ref, ug_ref, gc_ref, qm_ref, gm_ref, cinit_ref, mk_ref, mv_ref,
                wdw_ref, bdw_ref, lng_ref, lnb_ref, wpw_ref, wout_ref, gpost_ref,
                xo_ref, cs_ref, uext_ref, mix_ref, *, nb, tile, conv_chunk):
    @pl.when(pl.program_id(1) == 0)
    def _():
        uext_ref[:, 0:HALO, :] = cinit_ref[...]

    first = HALO - (CONV_K - 1)
    chunks = []
    for b in range(nb):
        ub_ref = uext_ref.at[b]
        ub_ref[HALO:HALO + tile, :] = uv_ref[b] * jax.nn.sigmoid(ug_ref[b])
        for c0 in range(0, tile, conv_chunk):
            acc = jnp.broadcast_to(bdw_ref[...], (conv_chunk, CONV_CH))
            for phase in range(SUBLANES):
                rows = conv_chunk + (SUBLANES if phase else 0)
                part = None
                for k in range(CONV_K):
                    if (first + k) % SUBLANES == phase:
                        base = c0 + first + k - phase
                        term = ub_ref[base:base + rows, :] * wdw_ref[k:k + 1, :]
                        part = term if part is None else part + term
                if part is not None:
                    acc = acc + part[phase:phase + conv_chunk, :]
            chunks.append(acc)
        cs_ref[b] = ub_ref[tile + first:tile + HALO, :]
        tail = ub_ref[tile:tile + HALO, :]
        ub_ref[0:HALO, :] = tail
    c = jnp.concatenate(chunks, axis=0) if len(chunks) > 1 else chunks[0]

    mu = jnp.mean(c, axis=-1, keepdims=True)
    var = jnp.mean(jnp.square(c - mu), axis=-1, keepdims=True)
    c = (c - mu) * lax.rsqrt(var + EPS) * lng_ref[...] + lnb_ref[...]
    c = jnp.dot(_silu(c).astype(BF16), wpw_ref[...], preferred_element_type=F32)

    def mem_head(ref, h):
        if len(ref.shape) == 3:
            return ref[:, h, :].astype(BF16)
        return ref[:, h * HEAD_DIM:(h + 1) * HEAD_DIM].astype(BF16)

    for b in range(nb):
        rows = slice(b * tile, (b + 1) * tile)
        mix_ref[rows, 0:ATT_W] = (a_ref[b] * _silu(ga_ref[b])).astype(BF16)
        mix_ref[rows, ATT_W:ATT_W + CONV_CH] = (c[rows] * _silu(gc_ref[b])).astype(BF16)
        for h in range(N_X_HEADS):
            cols = slice(h * HEAD_DIM, (h + 1) * HEAD_DIM)
            q = (qm_ref[b, :, cols] * SCALE).astype(BF16)
            s = _qk(q, mem_head(mk_ref.at[b], h))
            p = jnp.exp(s - jnp.max(s, axis=-1, keepdims=True))
            o = jnp.dot(p.astype(BF16), mem_head(mv_ref.at[b], h), preferred_element_type=F32)
            o = o / jnp.sum(p, axis=-1, keepdims=True)
            off = ATT_W + CONV_CH + h * HEAD_DIM
            mix_ref[rows, off:off + HEAD_DIM] = (o * _silu(gm_ref[b, :, cols])).astype(BF16)

    y = jnp.dot(mix_ref[...], wout_ref[...], preferred_element_type=F32)
    y = y * lax.rsqrt(jnp.mean(y * y, axis=-1, keepdims=True) + EPS) * gpost_ref[...]
    for b in range(nb):
        xo_ref[b] = x_ref[b] + y[b * tile:(b + 1) * tile]


def _mix(x, z, a, cinit, mk_arr, mk_block, mv_arr, mv_block, w_dw, b_dw, ln_g, ln_b, w_pw2, w_out, g_post,
         *, li, nb, tile):
    b, s, _ = x.shape
    conv_chunk = min(tile, 32)
    assert b % nb == 0 and s % tile == 0

    def rows(width, cb=0):
        return pl.BlockSpec((nb, tile, width), lambda bi, t: (bi, t, cb * COL_BLK // width))

    def const(shape):
        return pl.BlockSpec(shape, lambda bi, t: (0,) * len(shape))

    def layer(shape):
        return pl.BlockSpec((None,) + shape, lambda bi, t: (li,) + (0,) * len(shape))

    def mem(block):
        shape, index = block
        return pl.BlockSpec(shape, lambda bi, t: index(bi))

    return pl.pallas_call(
        functools.partial(_mix_kernel, nb=nb, tile=tile, conv_chunk=conv_chunk),
        grid=(b // nb, s // tile),
        in_specs=[
            rows(D_MODEL), rows(ATT_W),
            rows(ATT_W, CB_GA), rows(COL_BLK, CB_UV), rows(COL_BLK, CB_UG), rows(COL_BLK, CB_GC),
            rows(COL_BLK, CB_QM), rows(COL_BLK, CB_GM),
            pl.BlockSpec((nb, HALO, CONV_CH), lambda bi, t: (bi, 0, 0)),
            mem(mk_block), mem(mv_block),
            const((CONV_K, CONV_CH)), const((1, CONV_CH)), const((1, CONV_CH)), const((1, CONV_CH)),
            layer((CONV_CH, CONV_CH)), layer((MIX_W, D_MODEL)), const((1, D_MODEL)),
        ],
        out_specs=[
            rows(D_MODEL),
            pl.BlockSpec((nb, CONV_K - 1, CONV_CH), lambda bi, t: (bi, 0, 0)),
        ],
        out_shape=[
            jax.ShapeDtypeStruct((b, s, D_MODEL), F32),
            jax.ShapeDtypeStruct((b, CONV_K - 1, CONV_CH), F32),
        ],
        scratch_shapes=[pltpu.VMEM((nb, HALO + tile, CONV_CH), F32), pltpu.VMEM((nb * tile, MIX_W), BF16)],
        compiler_params=pltpu.CompilerParams(
            dimension_semantics=("parallel", "arbitrary"), vmem_limit_bytes=VMEM_LIMIT),
        name="mix",
    )(x, a, z, z, z, z, z, z, cinit, mk_arr, mv_arr,
      w_dw, b_dw.reshape(1, -1), ln_g.reshape(1, -1), ln_b.reshape(1, -1), w_pw2, w_out,
      g_post.reshape(1, -1))


def kernel(x_prompt, x_sample, mem_prompt, cache_attn_k, cache_attn_v, state_conv, cache_mem_k, cache_mem_v,
           rel_bias, norm_pre_g, w_in, w_dw, b_dw, ln_conv_g, ln_conv_b, w_pw2, w_mem_kv, w_out, norm_post_g):
    bp, s_len, _ = x_prompt.shape
    bs, t_len, _ = x_sample.shape
    l_buf = cache_attn_k.shape[2]
    l_prompt = min(WIN, s_len)

    w_in_b, w_pw2_b = w_in.astype(BF16), w_pw2.astype(BF16)
    w_mem_b, w_out_b = w_mem_kv.astype(BF16), w_out.astype(BF16)

    tab_near = _distance_table(rel_bias, np.arange(NEAR_REACH + 1), NEAR_PATTERNS)
    tab_far = _distance_table(rel_bias, FAR_D * np.arange(FAR_KEYS + 1), DIL_PATTERNS[-1:])
    bias_near = _toeplitz(tab_near * LOG2E, NEAR_REACH, Q_BLK, NEAR_KBLK * Q_BLK)
    bias_far = _toeplitz(tab_far * LOG2E, FAR_KEYS, Q_BLK, 2 * Q_BLK)
    tab_all = _distance_table(rel_bias, np.arange(WIN + 1), DIL_PATTERNS)
    bias_s = _head_matched_bias(tab_all, l_buf, t_len, l_buf)
    bias_s_new = _head_matched_bias(tab_all, 0, t_len, t_len)

    mkv = _mem_proj(mem_prompt.reshape(bp * N_MEM, D_MODEL), w_mem_b).reshape(DEPTH, bp, N_MEM, 2 * X_W)
    cinit_p = jnp.zeros((bp, HALO, CONV_CH), F32)
    pad_s = jnp.zeros((DEPTH, bs, HALO - (CONV_K - 1), CONV_CH), F32)
    cinit_s = jnp.concatenate([pad_s, state_conv], axis=2)
    xp, xs = x_prompt, x_sample
    stacks = None
    cvp, cvs = [], []
    for li in range(DEPTH):
        wargs = (w_dw[li], b_dw[li], ln_conv_g[li], ln_conv_b[li], w_pw2_b, w_out_b, norm_post_g[li])

        zp, akp, avp, aks, avs = _proj_kv(
            xp.reshape(bp * s_len, D_MODEL), norm_pre_g[li], w_in_b, cache_attn_k, cache_attn_v, stacks,
            li=li, seq=s_len, l_keep=l_prompt, shift=t_len)
        zp = zp.reshape(bp, s_len, IN_W)
        a = _prompt_attn(zp, bias_near, bias_far)
        mk_block, mv_block = (((None, 1, N_MEM, X_W), lambda bi, li=li, half=half: (li, bi, 0, half))
                              for half in (0, 1))
        xp, cst = _mix(xp, zp, a, cinit_p, mkv, mk_block, mkv, mv_block, *wargs, li=li, nb=1, tile=256)
        cvp.append(cst)

        zs = _proj(xs.reshape(bs * t_len, D_MODEL), norm_pre_g[li], w_in_b, li=li,
                   normalize=True, bm=bs * t_len, bn=COL_BLK).reshape(bs, t_len, IN_W)
        a, aks, avs = _sample_attn(zs, cache_attn_k, cache_attn_v, bias_s, bias_s_new, aks, avs, li=li)
        stacks = (akp, avp, aks, avs)
        mem_block = ((None, bs, N_MEM, N_X_HEADS, HEAD_DIM), lambda bi, li=li: (li, bi, 0, 0, 0))
        xs, cst = _mix(xs, zs, a, cinit_s[li], cache_mem_k, mem_block, cache_mem_v, mem_block, *wargs,
                       li=li, nb=bs, tile=t_len)
        cvs.append(cst)

    kv5 = (DEPTH, bp, l_prompt, N_ATT_HEADS, HEAD_DIM)
    mem5 = (DEPTH, bp, N_MEM, N_X_HEADS, HEAD_DIM)
    return (xp, xs, akp.reshape(kv5), avp.reshape(kv5), jnp.stack(cvp),
            mkv[..., :X_W].reshape(mem5), mkv[..., X_W:].reshape(mem5), aks, avs, jnp.stack(cvs))
```

```python
import functools

import numpy as np
import jax
import jax.numpy as jnp
from jax import lax
from jax.experimental import pallas as pl
from jax.experimental.pallas import tpu as pltpu

D_MODEL = 2048
DEPTH = 4
N_MEM = 256
HEAD_DIM = 128
ATT_W = D_MODEL // 2
N_ATT_HEADS = ATT_W // HEAD_DIM
DIL_PATTERNS = ((128, 1), (512, 4), (2048, 16))
WIN = max(w for w, _ in DIL_PATTERNS)
N_BUCKETS = 32
MAX_DIST = WIN
CONV_CH = D_MODEL // 4
CONV_K = 31
X_W = D_MODEL // 4
N_X_HEADS = X_W // HEAD_DIM
MIX_W = ATT_W + CONV_CH + X_W
IN_W = 4 * ATT_W + 3 * CONV_CH + 2 * X_W
EPS = 1e-6
NEG = -1e30
SCALE = HEAD_DIM ** -0.5
LOG2E = 1.4426950408889634

COL_BLK = 512
CB_Q, CB_K, CB_V, CB_GA = 0, 2, 4, 6
CB_UV, CB_UG, CB_GC, CB_QM, CB_GM = 8, 9, 10, 11, 12
HEADS_PER_COL_BLK = COL_BLK // HEAD_DIM

Q_BLK = 128
FAR_W, FAR_D = DIL_PATTERNS[-1]
NEAR_PATTERNS = DIL_PATTERNS[:-1]
NEAR_REACH = max(w for w, _ in NEAR_PATTERNS)
NEAR_KBLK = NEAR_REACH // Q_BLK + 1
FAR_KEYS = FAR_W // FAR_D
FAR_STEP = 4
assert FAR_KEYS == Q_BLK and NEAR_REACH % Q_BLK == 0 and FAR_STEP * FAR_STEP == FAR_D
STAGE_TILES = 16
PROJ_BM = 1024
COPY_RB = 256
SAMPLE_RB = 1024
HALO = 32
SUBLANES = 8
LANES = 128
VMEM_LIMIT = 56 * 1024 * 1024

BF16 = jnp.bfloat16
F32 = jnp.float32


def _t5_bucket(dist):
    dist = np.asarray(dist)
    max_exact = N_BUCKETS // 2
    large = max_exact + (np.log(np.maximum(dist, 1) / max_exact)
                         / np.log(MAX_DIST / max_exact) * (N_BUCKETS - max_exact)).astype(np.int32)
    large = np.minimum(large, N_BUCKETS - 1)
    return np.where(dist < max_exact, dist, large).astype(np.int32)


def _distance_table(rel_bias, dists, patterns):
    dists = np.asarray(dists)
    cnt = np.zeros(dists.shape, np.int64)
    for w, dil in patterns:
        cnt += ((dists % dil == 0) & (dists <= w)).astype(np.int64)
    logc = jnp.log(jnp.asarray(np.maximum(cnt, 1), F32))
    tab = rel_bias[_t5_bucket(dists)].T.astype(F32) + logc[None, :]
    tab = jnp.where(jnp.asarray(cnt > 0)[None, :], tab, NEG)
    return jnp.concatenate([tab, jnp.full((tab.shape[0], 1), NEG, F32)], axis=1)


def _toeplitz(tab, off, rows, cols):
    n_valid = tab.shape[1] - 1
    lp = rows + cols
    j = np.concatenate([np.arange(cols), np.full(lp - (rows + cols - 1), off + n_valid),
                        np.arange(-(rows - 1), 0)])
    d = off - j
    idx = np.where((d >= 0) & (d < n_valid), d, n_valid)
    v = jnp.take(tab, jnp.asarray(idx, jnp.int32), axis=1)
    flat = jnp.tile(v, (1, rows))[:, :rows * (lp - 1)]
    return flat.reshape(tab.shape[0], rows, lp - 1)[:, :, :cols]


def _head_matched_bias(tab, off, t_len, n_rows):
    n_h, n_valid = tab.shape[0], tab.shape[1] - 1
    n_cols = n_rows * n_h
    gaps = jnp.full((n_h, n_valid, n_h - 1), NEG, F32)
    s = jnp.concatenate([tab[:, :n_valid, None], gaps], axis=2).reshape(n_h, n_valid * n_h)
    k_max = n_h * (off + t_len - 1) + n_h - 1
    pad_l = max(0, n_cols - 1 - n_h * off)
    pad_r = max(0, k_max - (n_valid * n_h - 1))
    s = jnp.concatenate([jnp.full((n_h, pad_l), NEG, F32), s, jnp.full((n_h, pad_r), NEG, F32)], axis=1)
    n_s = s.shape[1]
    lane_pad = -n_s % LANES
    rev = jnp.concatenate([s[:, ::-1], jnp.full((n_h, lane_pad), NEG, F32)], axis=1)
    starts = tuple((h, n_s - 1 - (n_h * (off + t) + h) - pad_l) for h in range(n_h) for t in range(t_len))
    return pl.pallas_call(
        functools.partial(_slice_rows_kernel, starts=starts, n_cols=n_cols),
        out_shape=jax.ShapeDtypeStruct((len(starts), n_cols), F32),
        name="bias_rows",
    )(rev)


def _slice_rows_kernel(src_ref, out_ref, *, starts, n_cols):
    for r, (h, start) in enumerate(starts):
        lo = start // LANES * LANES
        hi = -(-(start + n_cols) // LANES) * LANES
        window = src_ref[h:h + 1, lo:hi]
        out_ref[r:r + 1, :] = window[:, start - lo:start - lo + n_cols]


def _norm_rows(x_ref, g_ref, h_ref, normalize):
    x = x_ref[...]
    if normalize:
        x = x * lax.rsqrt(jnp.mean(x * x, axis=-1, keepdims=True) + EPS) * g_ref[...]
    h_ref[...] = x.astype(BF16)


def _proj_kernel(x_ref, g_ref, w_ref, o_ref, h_ref, *, normalize):
    @pl.when(pl.program_id(1) == 0)
    def _():
        _norm_rows(x_ref, g_ref, h_ref, normalize)

    o_ref[...] = jnp.dot(h_ref[...], w_ref[...], preferred_element_type=F32)


def _proj(x, g, w, *, li, normalize, bm, bn):
    m, d = x.shape
    n = w.shape[2]
    return pl.pallas_call(
        functools.partial(_proj_kernel, normalize=normalize),
        grid=(m // bm, n // bn),
        in_specs=[
            pl.BlockSpec((bm, d), lambda i, j: (i, 0)),
            pl.BlockSpec((1, d), lambda i, j: (0, 0)),
            pl.BlockSpec((None, d, bn), lambda i, j: (li, 0, j)),
        ],
        out_specs=pl.BlockSpec((bm, bn), lambda i, j: (i, j)),
        out_shape=jax.ShapeDtypeStruct((m, n), F32),
        scratch_shapes=[pltpu.VMEM((bm, d), BF16)],
        compiler_params=pltpu.CompilerParams(
            dimension_semantics=("parallel", "arbitrary"), vmem_limit_bytes=VMEM_LIMIT),
        name="proj",
    )(x, g.reshape(1, d), w)


def _mem_proj(x, w):
    m, d = x.shape
    depth, _, n = w.shape
    bn = COL_BLK
    return pl.pallas_call(
        functools.partial(_proj_kernel, normalize=False),
        grid=(depth, n // bn),
        in_specs=[
            pl.BlockSpec((m, d), lambda li, j: (0, 0)),
            pl.BlockSpec((1, d), lambda li, j: (0, 0)),
            pl.BlockSpec((None, d, bn), lambda li, j: (li, 0, j)),
        ],
        out_specs=pl.BlockSpec((None, m, bn), lambda li, j: (li, 0, j)),
        out_shape=jax.ShapeDtypeStruct((depth, m, n), F32),
        scratch_shapes=[pltpu.VMEM((m, d), BF16)],
        compiler_params=pltpu.CompilerParams(
            dimension_semantics=("arbitrary", "arbitrary"), vmem_limit_bytes=VMEM_LIMIT),
        name="mem_proj",
    )(x, jnp.ones((1, d), F32), w)


def _proj_kv_kernel(*refs,tiles_per_seq, tail_tiles, aliased, shift, n_copy, v_start):
    x_ref, g_ref, w_ref, kc_ref, kx_ref, vc_ref, vx_ref = refs[:7]
    o_ref, ko_ref, vo_ref, sk_ref, sv_ref, h_ref = refs[11:] if aliased else refs[7:]
    i, j = pl.program_id(0), pl.program_id(1)
    step = i * pl.num_programs(1) + j
    rb = kc_ref.shape[0]

    for first, cache, nxt, out in ((0, kc_ref, kx_ref, sk_ref), (v_start, vc_ref, vx_ref, sv_ref)):
        @pl.when((step >= first) & (step < first + n_copy))
        def _(cache=cache, nxt=nxt, out=out):
            out[0:rb - shift] = cache[shift:rb]
            out[rb - shift:rb] = nxt[...]

    @pl.when(j == 0)
    def _():
        _norm_rows(x_ref, g_ref, h_ref, True)

    o_ref[...] = jnp.dot(h_ref[...], w_ref[...], preferred_element_type=F32)

    in_tail = i % tiles_per_seq >= tiles_per_seq - tail_tiles
    for cb0, dst in ((CB_K, ko_ref), (CB_V, vo_ref)):
        for cb in range(cb0, cb0 + ATT_W // COL_BLK):
            @pl.when(in_tail & (j == cb))
            def _(cb=cb, cb0=cb0, dst=dst):
                for hh in range(HEADS_PER_COL_BLK):
                    head = (cb - cb0) * HEADS_PER_COL_BLK + hh
                    dst[pl.ds(head, o_ref.shape[0], stride=N_ATT_HEADS), :] = (
                        o_ref[:, hh * HEAD_DIM:(hh + 1) * HEAD_DIM])


def _proj_kv(x, g, w, cache_k, cache_v, prev, *, li, seq, l_keep, shift):
    m, d = x.shape
    n = w.shape[2]
    bsz = m // seq
    bm, bn = PROJ_BM, COL_BLK
    tiles_per_seq, tail_tiles = seq // bm, l_keep // bm
    assert seq % bm == 0 and l_keep % bm == 0
    aliased = prev is not None
    kv_shape = jax.ShapeDtypeStruct((DEPTH, bsz, l_keep * N_ATT_HEADS, HEAD_DIM), F32)
    cache_shape = jax.ShapeDtypeStruct(cache_k.shape, F32)
    any_spec = pl.BlockSpec(memory_space=pl.ANY)
    n_steps = (m // bm) * (n // bn)
    n_seq, l_buf = cache_k.shape[1], cache_k.shape[2]
    rb = COPY_RB
    blks_per_seq = l_buf // rb
    n_copy = n_seq * blks_per_seq
    v_start = n_steps - n_copy
    assert l_buf % rb == 0 and rb % shift == 0 and 0 <= v_start

    def kv_map(i, j):
        return (li, i // tiles_per_seq, jnp.maximum(i % tiles_per_seq - (tiles_per_seq - tail_tiles), 0), 0)

    def copy_specs(first):
        def blk(i, j):
            return jnp.clip(i * (n // bn) + j - first, 0, n_copy - 1)

        def main(i, j):
            c = blk(i, j)
            return (li, c // blks_per_seq, c % blks_per_seq, 0, 0)

        def nxt(i, j):
            c = jnp.minimum(blk(i, j) + 1, n_copy - 1)
            return (li, c // blks_per_seq, (c % blks_per_seq) * (rb // shift), 0, 0)

        return (pl.BlockSpec((None, None, rb, N_ATT_HEADS, HEAD_DIM), main),
                pl.BlockSpec((None, None, shift, N_ATT_HEADS, HEAD_DIM), nxt))

    k_main, k_next = copy_specs(0)
    v_main, v_next = copy_specs(v_start)
    kv_spec = pl.BlockSpec((None, None, bm * N_ATT_HEADS, HEAD_DIM), kv_map)
    in_specs = [
        pl.BlockSpec((bm, d), lambda i, j: (i, 0)),
        pl.BlockSpec((1, d), lambda i, j: (0, 0)),
        pl.BlockSpec((None, d, bn), lambda i, j: (li, 0, j)),
        k_main, k_next, v_main, v_next,
    ]
    args = [x, g.reshape(1, d), w, cache_k, cache_k, cache_v, cache_v]
    if aliased:
        in_specs += [any_spec] * 4
        args += list(prev)
    return pl.pallas_call(
        functools.partial(_proj_kv_kernel, tiles_per_seq=tiles_per_seq, tail_tiles=tail_tiles,
                          aliased=aliased, shift=shift, n_copy=n_copy, v_start=v_start),
        grid=(m // bm, n // bn),
        in_specs=in_specs,
        out_specs=[pl.BlockSpec((bm, bn), lambda i, j: (i, j)), kv_spec, kv_spec, k_main, v_main],
        out_shape=[jax.ShapeDtypeStruct((m, n), F32), kv_shape, kv_shape, cache_shape, cache_shape],
        scratch_shapes=[pltpu.VMEM((bm, d), BF16)],
        input_output_aliases={7: 1, 8: 2, 9: 3, 10: 4} if aliased else {},
        compiler_params=pltpu.CompilerParams(
            dimension_semantics=("arbitrary", "arbitrary"), vmem_limit_bytes=VMEM_LIMIT),
        name="proj_kv",
    )(*args)


def _qk(q, k):
    return lax.dot_general(q, k, (((1,), (1,)), ((), ())), preferred_element_type=F32)


def _prompt_attn_kernel(q_ref, k_ref, v_ref, bn_ref, bf_ref, o_ref,
                        qb_ref, kb_ref, vb_ref, qf_ref, kf_ref, vf_ref, of_ref, lf_ref, tmp_ref, tmp2_ref,
                        *, seq):
    sub = seq // FAR_D
    chunk = 2 * Q_BLK
    val = slice(0, HEAD_DIM)

    for r0 in range(0, seq, chunk):
        rows = pl.ds(r0, chunk)
        qb_ref[rows, :] = (q_ref[rows, :] * (SCALE * LOG2E)).astype(BF16)
        kb_ref[rows, :] = k_ref[rows, :].astype(BF16)
        vb_ref[rows, val] = v_ref[rows, :].astype(BF16)
        vb_ref[rows, HEAD_DIM:] = jnp.ones((chunk, HEAD_DIM), BF16)
        vf_ref[rows, HEAD_DIM:] = jnp.ones((chunk, HEAD_DIM), BF16)

    quarter = seq // FAR_STEP
    for src_ref, dst_ref, scale in ((q_ref, qf_ref, SCALE * LOG2E), (k_ref, kf_ref, None),
                                    (v_ref, vf_ref, None)):
        for r1 in range(FAR_STEP):
            tmp_ref[r1 * quarter:(r1 + 1) * quarter, :] = src_ref[pl.ds(r1, quarter, stride=FAR_STEP), :]
        for r1 in range(FAR_STEP):
            for r2 in range(FAR_STEP):
                r = FAR_STEP * r2 + r1
                x = tmp_ref[pl.ds(r1 * quarter + r2, sub, stride=FAR_STEP), :]
                if scale is not None:
                    x = x * scale
                dst_ref[r * sub:(r + 1) * sub, val] = x.astype(BF16)

    far_geo = []
    for r in range(FAR_D):
        for a_blk in range(sub // Q_BLK):
            n_kt = min(a_blk, 1) + 1
            rows = pl.ds(r * sub + a_blk * Q_BLK, Q_BLK)
            keys = pl.ds(r * sub + (a_blk + 1 - n_kt) * Q_BLK, n_kt * Q_BLK)
            r1, r2 = r % FAR_STEP, r // FAR_STEP
            dst = pl.ds(r1 * quarter + FAR_STEP * a_blk * Q_BLK + r2, Q_BLK, stride=FAR_STEP)
            far_geo.append((rows, keys, n_kt, dst))
    near_geo = []
    for i in range(seq // Q_BLK):
        n_kt = min(i + 1, NEAR_KBLK)
        near_geo.append((pl.ds(i * Q_BLK, Q_BLK), pl.ds((i + 1 - n_kt) * Q_BLK, n_kt * Q_BLK), n_kt))

    for g0 in range(0, len(far_geo), STAGE_TILES):
        geo = far_geo[g0:g0 + STAGE_TILES]
        scores = [_qk(qf_ref[rows, :], kf_ref[keys, :]) + bf_ref[:, (2 - n_kt) * Q_BLK:]
                  for rows, keys, n_kt, _ in geo]
        probs = []
        for s in scores:
            m = jnp.max(s, axis=-1, keepdims=True)
            probs.append((jnp.exp2(s - m).astype(BF16), m))
        for (_, keys, _, dst), (p, m) in zip(geo, probs):
            o = jnp.dot(p, vf_ref[keys, :], preferred_element_type=F32)
            l = o[:, HEAD_DIM:]
            tmp_ref[dst, :] = o[:, val] / l
            tmp2_ref[dst, :] = m + jnp.log2(l)
    for r1 in range(FAR_STEP):
        of_ref[pl.ds(r1, quarter, stride=FAR_STEP), :] = tmp_ref[r1 * quarter:(r1 + 1) * quarter, :]
        lf_ref[pl.ds(r1, quarter, stride=FAR_STEP), :] = tmp2_ref[r1 * quarter:(r1 + 1) * quarter, :]

    for g0 in range(0, len(near_geo), STAGE_TILES):
        geo = near_geo[g0:g0 + STAGE_TILES]
        scores = [_qk(qb_ref[rows, :], kb_ref[keys, :]) + bn_ref[:, (NEAR_KBLK - n_kt) * Q_BLK:]
                  for rows, keys, n_kt in geo]
        probs = []
        for (rows, _, _), s in zip(geo, scores):
            m_far = lf_ref[rows, 0:1]
            m = jnp.maximum(jnp.max(s, axis=-1, keepdims=True), m_far)
            probs.append((jnp.exp2(s - m).astype(BF16), jnp.exp2(m_far - m)))
        for (rows, keys, _), (p, w_far) in zip(geo, probs):
            o = jnp.dot(p, vb_ref[keys, :], preferred_element_type=F32)
            acc = w_far * of_ref[rows, :] + o[:, val]
            o_ref[rows, :] = acc / (w_far + o[:, HEAD_DIM:])


def _prompt_attn(z, bias_near, bias_far):
    b, s, _ = z.shape
    assert s % (FAR_D * Q_BLK) == 0

    def head_cols(cb):
        return pl.BlockSpec((None, s, HEAD_DIM), lambda bi, h: (bi, 0, cb * HEADS_PER_COL_BLK + h))

    return pl.pallas_call(
        functools.partial(_prompt_attn_kernel, seq=s),
        grid=(b, N_ATT_HEADS),
        in_specs=[
            head_cols(CB_Q), head_cols(CB_K), head_cols(CB_V),
            pl.BlockSpec((None, Q_BLK, NEAR_KBLK * Q_BLK), lambda bi, h: (h, 0, 0)),
            pl.BlockSpec((None, Q_BLK, 2 * Q_BLK), lambda bi, h: (h, 0, 0)),
        ],
        out_specs=pl.BlockSpec((None, s, HEAD_DIM), lambda bi, h: (bi, 0, h)),
        out_shape=jax.ShapeDtypeStruct((b, s, ATT_W), F32),
        scratch_shapes=[pltpu.VMEM((s, HEAD_DIM), BF16), pltpu.VMEM((s, HEAD_DIM), BF16),
                        pltpu.VMEM((s, 2 * HEAD_DIM), BF16)] * 2 + [pltpu.VMEM((s, HEAD_DIM), F32)] * 4,
        compiler_params=pltpu.CompilerParams(
            dimension_semantics=("parallel", "parallel"), vmem_limit_bytes=VMEM_LIMIT),
        name="prompt_attn",
    )(z, z, z, bias_near, bias_far)


def _sample_attn_kernel(q_ref, kn_ref, vn_ref, kc_ref, vc_ref, bias_ref, biasn_ref, sk_hbm, sv_hbm,
                        o_ref, ko_ref, vo_ref, qa_ref, kn3_ref, vn3_ref, m_ref, l_ref, acc_ref, *, t_len, rb):
    del sk_hbm, sv_hbm
    j = pl.program_id(1)
    last = pl.num_programs(1) - 1
    n_heads = N_ATT_HEADS

    @pl.when(j == 0)
    def _():
        for h in range(n_heads):
            cols = slice(h * HEAD_DIM, (h + 1) * HEAD_DIM)
            qa_ref[h * t_len:(h + 1) * t_len, :] = q_ref[:, cols] * SCALE
            kn3_ref[:, h, :] = kn_ref[:, cols]
            vn3_ref[:, h, :] = vn_ref[:, cols]
        m_ref[...] = jnp.full(m_ref.shape, NEG, F32)
        l_ref[...] = jnp.zeros(l_ref.shape, F32)
        acc_ref[...] = jnp.zeros(acc_ref.shape, F32)

    qa = qa_ref[...].astype(BF16)

    def accumulate(k, v, bias):
        s = _qk(qa, k.astype(BF16)) + bias
        m_old = m_ref[...]
        m_new = jnp.maximum(m_old, jnp.max(s, axis=-1, keepdims=True))
        alpha = jnp.exp(m_old - m_new)
        p = jnp.exp(s - m_new)
        l_ref[...] = alpha * l_ref[...] + jnp.sum(p, axis=-1, keepdims=True)
        acc_ref[...] = alpha * acc_ref[...] + jnp.dot(p.astype(BF16), v.astype(BF16),
                                                      preferred_element_type=F32)
        m_ref[...] = m_new

    accumulate(kc_ref[...].reshape(rb * n_heads, HEAD_DIM), vc_ref[...].reshape(rb * n_heads, HEAD_DIM),
               bias_ref[...])

    @pl.when(j == last)
    def _():
        ko_ref[...] = kn3_ref[...]
        vo_ref[...] = vn3_ref[...]
        accumulate(kn3_ref[...].reshape(t_len * n_heads, HEAD_DIM),
                   vn3_ref[...].reshape(t_len * n_heads, HEAD_DIM), biasn_ref[...])
        o = acc_ref[...] / l_ref[...]
        for h in range(n_heads):
            o_ref[:, h * HEAD_DIM:(h + 1) * HEAD_DIM] = o[h * t_len:(h + 1) * t_len, :]


def _sample_attn(z, cache_k, cache_v, bias, bias_new, stack_k, stack_v, *, li):
    b, t_len, _ = z.shape
    l_buf = cache_k.shape[2]
    rb = SAMPLE_RB
    assert t_len == SUBLANES and l_buf % rb == 0
    rows = N_ATT_HEADS * t_len

    def zspec(cb):
        return pl.BlockSpec((None, t_len, ATT_W), lambda bi, j: (bi, 0, cb * COL_BLK // ATT_W))

    blk_spec = pl.BlockSpec((None, None, rb, N_ATT_HEADS, HEAD_DIM), lambda bi, j: (li, bi, j, 0, 0))
    new_spec = pl.BlockSpec((None, None, t_len, N_ATT_HEADS, HEAD_DIM),
                            lambda bi, j: (li, bi, l_buf // t_len - 1, 0, 0))
    any_spec = pl.BlockSpec(memory_space=pl.ANY)
    in_specs = [
        zspec(CB_Q), zspec(CB_K), zspec(CB_V), blk_spec, blk_spec,
        pl.BlockSpec((rows, rb * N_ATT_HEADS), lambda bi, j: (0, j)),
        pl.BlockSpec((rows, rows), lambda bi, j: (0, 0)),
        any_spec, any_spec,
    ]
    args = [z, z, z, cache_k, cache_v, bias, bias_new, stack_k, stack_v]
    return pl.pallas_call(
        functools.partial(_sample_attn_kernel, t_len=t_len, rb=rb),
        grid=(b, l_buf // rb),
        in_specs=in_specs,
        out_specs=[pl.BlockSpec((None, t_len, ATT_W), lambda bi, j: (bi, 0, 0)), new_spec, new_spec],
        out_shape=[
            jax.ShapeDtypeStruct((b, t_len, ATT_W), F32),
            jax.ShapeDtypeStruct(cache_k.shape, F32),
            jax.ShapeDtypeStruct(cache_v.shape, F32),
        ],
        scratch_shapes=[
            pltpu.VMEM((rows, HEAD_DIM), F32),
            pltpu.VMEM((t_len, N_ATT_HEADS, HEAD_DIM), F32), pltpu.VMEM((t_len, N_ATT_HEADS, HEAD_DIM), F32),
            pltpu.VMEM((rows, 1), F32), pltpu.VMEM((rows, 1), F32), pltpu.VMEM((rows, HEAD_DIM), F32),
        ],
        input_output_aliases={7: 1, 8: 2},
        compiler_params=pltpu.CompilerParams(
            dimension_semantics=("arbitrary", "arbitrary"), vmem_limit_bytes=VMEM_LIMIT),
        name="sample_attn",
    )(*args)


def _silu(x):
    return x * jax.nn.sigmoid(x)


def _mix_kernel(x_ref, a_ref, ga_ref, uv_ref, ug_ref, gc_ref, qm_ref, gm_ref, cinit_ref, mk_ref, mv_ref,
                wdw_ref, bdw_ref, lng_ref, lnb_ref, wpw_ref, wout_ref, gpost_ref,
                xo_ref, cs_ref, uext_ref, mix_ref, *, nb, tile, conv_chunk):
    @pl.when(pl.program_id(1) == 0)
    def _():
        uext_ref[:, 0:HALO, :] = cinit_ref[...]

    first = HALO - (CONV_K - 1)
    chunks = []
    for b in range(nb):
        ub_ref = uext_ref.at[b]
        ub_ref[HALO:HALO + tile, :] = uv_ref[b] * jax.nn.sigmoid(ug_ref[b])
        for c0 in range(0, tile, conv_chunk):
            acc = jnp.broadcast_to(bdw_ref[...], (conv_chunk, CONV_CH))
            for phase in range(SUBLANES):
                rows = conv_chunk + (SUBLANES if phase else 0)
                part = None
                for k in range(CONV_K):
                    if (first + k) % SUBLANES == phase:
                        base = c0 + first + k - phase
                        term = ub_ref[base:base + rows, :] * wdw_ref[k:k + 1, :]
                        part = term if part is None else part + term
                if part is not None:
                    acc = acc + part[phase:phase + conv_chunk, :]
            chunks.append(acc)
        cs_ref[b] = ub_ref[tile + first:tile + HALO, :]
        tail = ub_ref[tile:tile + HALO, :]
        ub_ref[0:HALO, :] = tail
    c = jnp.concatenate(chunks, axis=0) if len(chunks) > 1 else chunks[0]

    mu = jnp.mean(c, axis=-1, keepdims=True)
    var = jnp.mean(jnp.square(c - mu), axis=-1, keepdims=True)
    c = (c - mu) * lax.rsqrt(var + EPS) * lng_ref[...] + lnb_ref[...]
    c = jnp.dot(_silu(c).astype(BF16), wpw_ref[...], preferred_element_type=F32)

    def mem_head(ref, h):
        if len(ref.shape) == 3:
            return ref[:, h, :].astype(BF16)
        return ref[:, h * HEAD_DIM:(h + 1) * HEAD_DIM].astype(BF16)

    for b in range(nb):
        rows = slice(b * tile, (b + 1) * tile)
        mix_ref[rows, 0:ATT_W] = (a_ref[b] * _silu(ga_ref[b])).astype(BF16)
        mix_ref[rows, ATT_W:ATT_W + CONV_CH] = (c[rows] * _silu(gc_ref[b])).astype(BF16)
        for h in range(N_X_HEADS):
            cols = slice(h * HEAD_DIM, (h + 1) * HEAD_DIM)
            q = (qm_ref[b, :, cols] * SCALE).astype(BF16)
            s = _qk(q, mem_head(mk_ref.at[b], h))
            p = jnp.exp(s - jnp.max(s, axis=-1, keepdims=True))
            o = jnp.dot(p.astype(BF16), mem_head(mv_ref.at[b], h), preferred_element_type=F32)
            o = o / jnp.sum(p, axis=-1, keepdims=True)
            off = ATT_W + CONV_CH + h * HEAD_DIM
            mix_ref[rows, off:off + HEAD_DIM] = (o * _silu(gm_ref[b, :, cols])).astype(BF16)

    y = jnp.dot(mix_ref[...], wout_ref[...], preferred_element_type=F32)
    y = y * lax.rsqrt(jnp.mean(y * y, axis=-1, keepdims=True) + EPS) * gpost_ref[...]
    for b in range(nb):
        xo_ref[b] = x_ref[b] + y[b * tile:(b + 1) * tile]


def _mix(x, z, a, cinit, mk_arr, mk_block, mv_arr, mv_block, w_dw, b_dw, ln_g, ln_b, w_pw2, w_out, g_post,
         *, li, nb, tile):
    b, s, _ = x.shape
    conv_chunk = min(tile, 32)
    assert b % nb == 0 and s % tile == 0

    def rows(width, cb=0):
        return pl.BlockSpec((nb, tile, width), lambda bi, t: (bi, t, cb * COL_BLK // width))

    def const(shape):
        return pl.BlockSpec(shape, lambda bi, t: (0,) * len(shape))

    def layer(shape):
        return pl.BlockSpec((None,) + shape, lambda bi, t: (li,) + (0,) * len(shape))

    def mem(block):
        shape, index = block
        return pl.BlockSpec(shape, lambda bi, t: index(bi))

    return pl.pallas_call(
        functools.partial(_mix_kernel, nb=nb, tile=tile, conv_chunk=conv_chunk),
        grid=(b // nb, s // tile),
        in_specs=[
            rows(D_MODEL), rows(ATT_W),
            rows(ATT_W, CB_GA), rows(COL_BLK, CB_UV), rows(COL_BLK, CB_UG), rows(COL_BLK, CB_GC),
            rows(COL_BLK, CB_QM), rows(COL_BLK, CB_GM),
            pl.BlockSpec((nb, HALO, CONV_CH), lambda bi, t: (bi, 0, 0)),
            mem(mk_block), mem(mv_block),
            const((CONV_K, CONV_CH)), const((1, CONV_CH)), const((1, CONV_CH)), const((1, CONV_CH)),
            layer((CONV_CH, CONV_CH)), layer((MIX_W, D_MODEL)), const((1, D_MODEL)),
        ],
        out_specs=[
            rows(D_MODEL),
            pl.BlockSpec((nb, CONV_K - 1, CONV_CH), lambda bi, t: (bi, 0, 0)),
        ],
        out_shape=[
            jax.ShapeDtypeStruct((b, s, D_MODEL), F32),
            jax.ShapeDtypeStruct((b, CONV_K - 1, CONV_CH), F32),
        ],
        scratch_shapes=[pltpu.VMEM((nb, HALO + tile, CONV_CH), F32), pltpu.VMEM((nb * tile, MIX_W), BF16)],
        compiler_params=pltpu.CompilerParams(
            dimension_semantics=("parallel", "arbitrary"), vmem_limit_bytes=VMEM_LIMIT),
        name="mix",
    )(x, a, z, z, z, z, z, z, cinit, mk_arr, mv_arr,
      w_dw, b_dw.reshape(1, -1), ln_g.reshape(1, -1), ln_b.reshape(1, -1), w_pw2, w_out,
      g_post.reshape(1, -1))


def kernel(x_prompt, x_sample, mem_prompt, cache_attn_k, cache_attn_v, state_conv, cache_mem_k, cache_mem_v,
           rel_bias, norm_pre_g, w_in, w_dw, b_dw, ln_conv_g, ln_conv_b, w_pw2, w_mem_kv, w_out, norm_post_g):
    bp, s_len, _ = x_prompt.shape
    bs, t_len, _ = x_sample.shape
    l_buf = cache_attn_k.shape[2]
    l_prompt = min(WIN, s_len)

    w_in_b, w_pw2_b = w_in.astype(BF16), w_pw2.astype(BF16)
    w_mem_b, w_out_b = w_mem_kv.astype(BF16), w_out.astype(BF16)

    tab_near = _distance_table(rel_bias, np.arange(NEAR_REACH + 1), NEAR_PATTERNS)
    tab_far = _distance_table(rel_bias, FAR_D * np.arange(FAR_KEYS + 1), DIL_PATTERNS[-1:])
    bias_near = _toeplitz(tab_near * LOG2E, NEAR_REACH, Q_BLK, NEAR_KBLK * Q_BLK)
    bias_far = _toeplitz(tab_far * LOG2E, FAR_KEYS, Q_BLK, 2 * Q_BLK)
    tab_all = _distance_table(rel_bias, np.arange(WIN + 1), DIL_PATTERNS)
    bias_s = _head_matched_bias(tab_all, l_buf, t_len, l_buf)
    bias_s_new = _head_matched_bias(tab_all, 0, t_len, t_len)

    mkv = _mem_proj(mem_prompt.reshape(bp * N_MEM, D_MODEL), w_mem_b).reshape(DEPTH, bp, N_MEM, 2 * X_W)
    cinit_p = jnp.zeros((bp, HALO, CONV_CH), F32)
    pad_s = jnp.zeros((DEPTH, bs, HALO - (CONV_K - 1), CONV_CH), F32)
    cinit_s = jnp.concatenate([pad_s, state_conv], axis=2)
    xp, xs = x_prompt, x_sample
    stacks = None
    cvp, cvs = [], []
    for li in range(DEPTH):
        wargs = (w_dw[li], b_dw[li], ln_conv_g[li], ln_conv_b[li], w_pw2_b, w_out_b, norm_post_g[li])

        zp, akp, avp, aks, avs = _proj_kv(
            xp.reshape(bp * s_len, D_MODEL), norm_pre_g[li], w_in_b, cache_attn_k, cache_attn_v, stacks,
            li=li, seq=s_len, l_keep=l_prompt, shift=t_len)
        zp = zp.reshape(bp, s_len, IN_W)
        a = _prompt_attn(zp, bias_near, bias_far)
        mk_block, mv_block = (((None, 1, N_MEM, X_W), lambda bi, li=li, half=half: (li, bi, 0, half))
                              for half in (0, 1))
        xp, cst = _mix(xp, zp, a, cinit_p, mkv, mk_block, mkv, mv_block, *wargs, li=li, nb=1, tile=256)
        cvp.append(cst)

        zs = _proj(xs.reshape(bs * t_len, D_MODEL), norm_pre_g[li], w_in_b, li=li,
                   normalize=True, bm=bs * t_len, bn=COL_BLK).reshape(bs, t_len, IN_W)
        a, aks, avs = _sample_attn(zs, cache_attn_k, cache_attn_v, bias_s, bias_s_new, aks, avs, li=li)
        stacks = (akp, avp, aks, avs)
        mem_block = ((None, bs, N_MEM, N_X_HEADS, HEAD_DIM), lambda bi, li=li: (li, bi, 0, 0, 0))
        xs, cst = _mix(xs, zs, a, cinit_s[li], cache_mem_k, mem_block, cache_mem_v, mem_block, *wargs,
                       li=li, nb=bs, tile=t_len)
        cvs.append(cst)

    kv5 = (DEPTH, bp, l_prompt, N_ATT_HEADS, HEAD_DIM)
    mem5 = (DEPTH, bp, N_MEM, N_X_HEADS, HEAD_DIM)
    return (xp, xs, akp.reshape(kv5), avp.reshape(kv5), jnp.stack(cvp),
            mkv[..., :X_W].reshape(mem5), mkv[..., X_W:].reshape(mem5), aks, avs, jnp.stack(cvs))
```

```python
import functools

import numpy as np
import jax
import jax.numpy as jnp
from jax import lax
from jax.experimental import pallas as pl
from jax.experimental.pallas import tpu as pltpu

D_MODEL = 2048
DEPTH = 4
N_MEM = 256
HEAD_DIM = 128
ATT_W = D_MODEL // 2
N_ATT_HEADS = ATT_W // HEAD_DIM
DIL_PATTERNS = ((128, 1), (512, 4), (2048, 16))
WIN = max(w for w, _ in DIL_PATTERNS)
N_BUCKETS = 32
MAX_DIST = WIN
CONV_CH = D_MODEL // 4
CONV_K = 31
X_W = D_MODEL // 4
N_X_HEADS = X_W // HEAD_DIM
MIX_W = ATT_W + CONV_CH + X_W
IN_W = 4 * ATT_W + 3 * CONV_CH + 2 * X_W
EPS = 1e-6
NEG = -1e30
SCALE = HEAD_DIM ** -0.5
LOG2E = 1.4426950408889634

COL_BLK = 512
CB_Q, CB_K, CB_V, CB_GA = 0, 2, 4, 6
CB_UV, CB_UG, CB_GC, CB_QM, CB_GM = 8, 9, 10, 11, 12
HEADS_PER_COL_BLK = COL_BLK // HEAD_DIM

Q_BLK = 128
FAR_W, FAR_D = DIL_PATTERNS[-1]
NEAR_PATTERNS = DIL_PATTERNS[:-1]
NEAR_REACH = max(w for w, _ in NEAR_PATTERNS)
NEAR_KBLK = NEAR_REACH // Q_BLK + 1
FAR_KEYS = FAR_W // FAR_D
FAR_STEP = 4
assert FAR_KEYS == Q_BLK and NEAR_REACH % Q_BLK == 0 and FAR_STEP * FAR_STEP == FAR_D
MIX_NB, MIX_TILE = 1, 512
STAGE_TILES = 16
PROJ_BM = 1024
COPY_RB = 256
SAMPLE_RB = 1024
HALO = 32
SUBLANES = 8
LANES = 128
VMEM_LIMIT = 56 * 1024 * 1024

BF16 = jnp.bfloat16
F32 = jnp.float32


def _t5_bucket(dist):
    dist = np.asarray(dist)
    max_exact = N_BUCKETS // 2
    large = max_exact + (np.log(np.maximum(dist, 1) / max_exact)
                         / np.log(MAX_DIST / max_exact) * (N_BUCKETS - max_exact)).astype(np.int32)
    large = np.minimum(large, N_BUCKETS - 1)
    return np.where(dist < max_exact, dist, large).astype(np.int32)


def _distance_table(rel_bias, dists, patterns):
    dists = np.asarray(dists)
    cnt = np.zeros(dists.shape, np.int64)
    for w, dil in patterns:
        cnt += ((dists % dil == 0) & (dists <= w)).astype(np.int64)
    logc = jnp.log(jnp.asarray(np.maximum(cnt, 1), F32))
    tab = rel_bias[_t5_bucket(dists)].T.astype(F32) + logc[None, :]
    tab = jnp.where(jnp.asarray(cnt > 0)[None, :], tab, NEG)
    return jnp.concatenate([tab, jnp.full((tab.shape[0], 1), NEG, F32)], axis=1)


def _toeplitz(tab, off, rows, cols):
    n_valid = tab.shape[1] - 1
    lp = rows + cols
    j = np.concatenate([np.arange(cols), np.full(lp - (rows + cols - 1), off + n_valid),
                        np.arange(-(rows - 1), 0)])
    d = off - j
    idx = np.where((d >= 0) & (d < n_valid), d, n_valid)
    v = jnp.take(tab, jnp.asarray(idx, jnp.int32), axis=1)
    flat = jnp.tile(v, (1, rows))[:, :rows * (lp - 1)]
    return flat.reshape(tab.shape[0], rows, lp - 1)[:, :, :cols]


def _head_matched_bias(tab, off, t_len, n_rows):
    n_h, n_valid = tab.shape[0], tab.shape[1] - 1
    n_cols = n_rows * n_h
    gaps = jnp.full((n_h, n_valid, n_h - 1), NEG, F32)
    s = jnp.concatenate([tab[:, :n_valid, None], gaps], axis=2).reshape(n_h, n_valid * n_h)
    k_max = n_h * (off + t_len - 1) + n_h - 1
    pad_l = max(0, n_cols - 1 - n_h * off)
    pad_r = max(0, k_max - (n_valid * n_h - 1))
    s = jnp.concatenate([jnp.full((n_h, pad_l), NEG, F32), s, jnp.full((n_h, pad_r), NEG, F32)], axis=1)
    n_s = s.shape[1]
    lane_pad = -n_s % LANES
    rev = jnp.concatenate([s[:, ::-1], jnp.full((n_h, lane_pad), NEG, F32)], axis=1)
    starts = tuple((h, n_s - 1 - (n_h * (off + t) + h) - pad_l) for h in range(n_h) for t in range(t_len))
    return pl.pallas_call(
        functools.partial(_slice_rows_kernel, starts=starts, n_cols=n_cols),
        out_shape=jax.ShapeDtypeStruct((len(starts), n_cols), F32),
        name="bias_rows",
    )(rev)


def _slice_rows_kernel(src_ref, out_ref, *, starts, n_cols):
    for r, (h, start) in enumerate(starts):
        lo = start // LANES * LANES
        hi = -(-(start + n_cols) // LANES) * LANES
        window = src_ref[h:h + 1, lo:hi]
        out_ref[r:r + 1, :] = window[:, start - lo:start - lo + n_cols]


def _norm_rows(x_ref, g_ref, h_ref, normalize):
    x = x_ref[...]
    if normalize:
        x = x * lax.rsqrt(jnp.mean(x * x, axis=-1, keepdims=True) + EPS) * g_ref[...]
    h_ref[...] = x.astype(BF16)


def _proj_kernel(x_ref, g_ref, w_ref, o_ref, h_ref, *, normalize):
    @pl.when(pl.program_id(1) == 0)
    def _():
        _norm_rows(x_ref, g_ref, h_ref, normalize)

    o_ref[...] = jnp.dot(h_ref[...], w_ref[...], preferred_element_type=F32)


def _proj_cast_kernel(x_ref, g_ref, w_ref, o_ref, wb_ref, h_ref):
    @pl.when(pl.program_id(0) == 0)
    def _():
        _norm_rows(x_ref, g_ref, h_ref, True)

    w = w_ref[...].astype(BF16)
    wb_ref[...] = w
    o_ref[...] = jnp.dot(h_ref[...], w, preferred_element_type=F32)


def _proj_cast(x, g, w, *, li, bn):
    m, d = x.shape
    n = w.shape[2]
    return pl.pallas_call(
        _proj_cast_kernel,
        grid=(n // bn,),
        in_specs=[
            pl.BlockSpec((m, d), lambda j: (0, 0)),
            pl.BlockSpec((1, d), lambda j: (0, 0)),
            pl.BlockSpec((None, d, bn), lambda j: (li, 0, j)),
        ],
        out_specs=[pl.BlockSpec((m, bn), lambda j: (0, j)), pl.BlockSpec((d, bn), lambda j: (0, j))],
        out_shape=[jax.ShapeDtypeStruct((m, n), F32), jax.ShapeDtypeStruct((d, n), BF16)],
        scratch_shapes=[pltpu.VMEM((m, d), BF16)],
        compiler_params=pltpu.CompilerParams(
            dimension_semantics=("arbitrary",), vmem_limit_bytes=VMEM_LIMIT),
        name="proj_cast",
    )(x, g.reshape(1, d), w)


def _mem_proj(x, w):
    m, d = x.shape
    depth, _, n = w.shape
    bn = COL_BLK
    return pl.pallas_call(
        functools.partial(_proj_kernel, normalize=False),
        grid=(depth, n // bn),
        in_specs=[
            pl.BlockSpec((m, d), lambda li, j: (0, 0)),
            pl.BlockSpec((1, d), lambda li, j: (0, 0)),
            pl.BlockSpec((None, d, bn), lambda li, j: (li, 0, j)),
        ],
        out_specs=pl.BlockSpec((None, m, bn), lambda li, j: (li, 0, j)),
        out_shape=jax.ShapeDtypeStruct((depth, m, n), F32),
        scratch_shapes=[pltpu.VMEM((m, d), BF16)],
        compiler_params=pltpu.CompilerParams(
            dimension_semantics=("arbitrary", "arbitrary"), vmem_limit_bytes=VMEM_LIMIT),
        name="mem_proj",
    )(x, jnp.ones((1, d), F32), w)


def _proj_kv_kernel(*refs,tiles_per_seq, tail_tiles, aliased, shift, n_copy, v_start):
    x_ref, g_ref, w_ref, kc_ref, kx_ref, vc_ref, vx_ref = refs[:7]
    o_ref, ko_ref, vo_ref, sk_ref, sv_ref, h_ref = refs[11:] if aliased else refs[7:]
    i, j = pl.program_id(0), pl.program_id(1)
    step = i * pl.num_programs(1) + j
    rb = kc_ref.shape[0]

    for first, cache, nxt, out in ((0, kc_ref, kx_ref, sk_ref), (v_start, vc_ref, vx_ref, sv_ref)):
        @pl.when((step >= first) & (step < first + n_copy))
        def _(cache=cache, nxt=nxt, out=out):
            out[0:rb - shift] = cache[shift:rb]
            out[rb - shift:rb] = nxt[...]

    @pl.when(j == 0)
    def _():
        _norm_rows(x_ref, g_ref, h_ref, True)

    o_ref[...] = jnp.dot(h_ref[...], w_ref[...], preferred_element_type=F32)

    in_tail = i % tiles_per_seq >= tiles_per_seq - tail_tiles
    for cb0, dst in ((CB_K, ko_ref), (CB_V, vo_ref)):
        for cb in range(cb0, cb0 + ATT_W // COL_BLK):
            @pl.when(in_tail & (j == cb))
            def _(cb=cb, cb0=cb0, dst=dst):
                for hh in range(HEADS_PER_COL_BLK):
                    head = (cb - cb0) * HEADS_PER_COL_BLK + hh
                    dst[pl.ds(head, o_ref.shape[0], stride=N_ATT_HEADS), :] = (
                        o_ref[:, hh * HEAD_DIM:(hh + 1) * HEAD_DIM])


def _proj_kv(x, g, w, cache_k, cache_v, prev, *, li, seq, l_keep, shift):
    m, d = x.shape
    n = w.shape[1]
    bsz = m // seq
    bm, bn = PROJ_BM, COL_BLK
    tiles_per_seq, tail_tiles = seq // bm, l_keep // bm
    assert seq % bm == 0 and l_keep % bm == 0
    aliased = prev is not None
    kv_shape = jax.ShapeDtypeStruct((DEPTH, bsz, l_keep * N_ATT_HEADS, HEAD_DIM), F32)
    cache_shape = jax.ShapeDtypeStruct(cache_k.shape, F32)
    any_spec = pl.BlockSpec(memory_space=pl.ANY)
    n_steps = (m // bm) * (n // bn)
    n_seq, l_buf = cache_k.shape[1], cache_k.shape[2]
    rb = COPY_RB
    blks_per_seq = l_buf // rb
    n_copy = n_seq * blks_per_seq
    v_start = n_steps - n_copy
    assert l_buf % rb == 0 and rb % shift == 0 and 0 <= v_start

    def kv_map(i, j):
        return (li, i // tiles_per_seq, jnp.maximum(i % tiles_per_seq - (tiles_per_seq - tail_tiles), 0), 0)

    def copy_specs(first):
        def blk(i, j):
            return jnp.clip(i * (n // bn) + j - first, 0, n_copy - 1)

        def main(i, j):
            c = blk(i, j)
            return (li, c // blks_per_seq, c % blks_per_seq, 0, 0)

        def nxt(i, j):
            c = jnp.minimum(blk(i, j) + 1, n_copy - 1)
            return (li, c // blks_per_seq, (c % blks_per_seq) * (rb // shift), 0, 0)

        return (pl.BlockSpec((None, None, rb, N_ATT_HEADS, HEAD_DIM), main),
                pl.BlockSpec((None, None, shift, N_ATT_HEADS, HEAD_DIM), nxt))

    k_main, k_next = copy_specs(0)
    v_main, v_next = copy_specs(v_start)
    kv_spec = pl.BlockSpec((None, None, bm * N_ATT_HEADS, HEAD_DIM), kv_map)
    in_specs = [
        pl.BlockSpec((bm, d), lambda i, j: (i, 0)),
        pl.BlockSpec((1, d), lambda i, j: (0, 0)),
        pl.BlockSpec((d, bn), lambda i, j: (0, j)),
        k_main, k_next, v_main, v_next,
    ]
    args = [x, g.reshape(1, d), w, cache_k, cache_k, cache_v, cache_v]
    if aliased:
        in_specs += [any_spec] * 4
        args += list(prev)
    return pl.pallas_call(
        functools.partial(_proj_kv_kernel, tiles_per_seq=tiles_per_seq, tail_tiles=tail_tiles,
                          aliased=aliased, shift=shift, n_copy=n_copy, v_start=v_start),
        grid=(m // bm, n // bn),
        in_specs=in_specs,
        out_specs=[pl.BlockSpec((bm, bn), lambda i, j: (i, j)), kv_spec, kv_spec, k_main, v_main],
        out_shape=[jax.ShapeDtypeStruct((m, n), F32), kv_shape, kv_shape, cache_shape, cache_shape],
        scratch_shapes=[pltpu.VMEM((bm, d), BF16)],
        input_output_aliases={7: 1, 8: 2, 9: 3, 10: 4} if aliased else {},
        compiler_params=pltpu.CompilerParams(
            dimension_semantics=("arbitrary", "arbitrary"), vmem_limit_bytes=VMEM_LIMIT),
        name="proj_kv",
    )(*args)


def _qk(q, k):
    return lax.dot_general(q, k, (((1,), (1,)), ((), ())), preferred_element_type=F32)


def _prompt_attn_kernel(q_ref, k_ref, v_ref, bn_ref, bf_ref, o_ref,
                        qb_ref, kb_ref, vb_ref, qf_ref, kf_ref, vf_ref, of_ref, lf_ref, tmp_ref, tmp2_ref,
                        *, seq):
    sub = seq // FAR_D
    chunk = 2 * Q_BLK
    val = slice(0, HEAD_DIM)

    for r0 in range(0, seq, chunk):
        rows = pl.ds(r0, chunk)
        qb_ref[rows, :] = (q_ref[rows, :] * (SCALE * LOG2E)).astype(BF16)
        kb_ref[rows, :] = k_ref[rows, :].astype(BF16)
        vb_ref[rows, val] = v_ref[rows, :].astype(BF16)
        vb_ref[rows, HEAD_DIM:] = jnp.ones((chunk, HEAD_DIM), BF16)
        vf_ref[rows, HEAD_DIM:] = jnp.ones((chunk, HEAD_DIM), BF16)

    quarter = seq // FAR_STEP
    for src_ref, dst_ref, scale in ((q_ref, qf_ref, SCALE * LOG2E), (k_ref, kf_ref, None),
                                    (v_ref, vf_ref, None)):
        for r1 in range(FAR_STEP):
            tmp_ref[r1 * quarter:(r1 + 1) * quarter, :] = src_ref[pl.ds(r1, quarter, stride=FAR_STEP), :]
        for r1 in range(FAR_STEP):
            for r2 in range(FAR_STEP):
                r = FAR_STEP * r2 + r1
                x = tmp_ref[pl.ds(r1 * quarter + r2, sub, stride=FAR_STEP), :]
                if scale is not None:
                    x = x * scale
                dst_ref[r * sub:(r + 1) * sub, val] = x.astype(BF16)

    far_geo = []
    for r in range(FAR_D):
        for a_blk in range(sub // Q_BLK):
            n_kt = min(a_blk, 1) + 1
            rows = pl.ds(r * sub + a_blk * Q_BLK, Q_BLK)
            keys = pl.ds(r * sub + (a_blk + 1 - n_kt) * Q_BLK, n_kt * Q_BLK)
            r1, r2 = r % FAR_STEP, r // FAR_STEP
            dst = pl.ds(r1 * quarter + FAR_STEP * a_blk * Q_BLK + r2, Q_BLK, stride=FAR_STEP)
            far_geo.append((rows, keys, n_kt, dst))
    near_geo = []
    for i in range(seq // Q_BLK):
        n_kt = min(i + 1, NEAR_KBLK)
        near_geo.append((pl.ds(i * Q_BLK, Q_BLK), pl.ds((i + 1 - n_kt) * Q_BLK, n_kt * Q_BLK), n_kt))

    for g0 in range(0, len(far_geo), STAGE_TILES):
        geo = far_geo[g0:g0 + STAGE_TILES]
        scores = [_qk(qf_ref[rows, :], kf_ref[keys, :]) + bf_ref[:, (2 - n_kt) * Q_BLK:]
                  for rows, keys, n_kt, _ in geo]
        probs = []
        for s in scores:
            m = jnp.max(s, axis=-1, keepdims=True)
            probs.append((jnp.exp2(s - m).astype(BF16), m))
        for (_, keys, _, dst), (p, m) in zip(geo, probs):
            o = jnp.dot(p, vf_ref[keys, :], preferred_element_type=F32)
            l = o[:, HEAD_DIM:]
            tmp_ref[dst, :] = o[:, val] / l
            tmp2_ref[dst, :] = m + jnp.log2(l)
    for r1 in range(FAR_STEP):
        of_ref[pl.ds(r1, quarter, stride=FAR_STEP), :] = tmp_ref[r1 * quarter:(r1 + 1) * quarter, :]
        lf_ref[pl.ds(r1, quarter, stride=FAR_STEP), :] = tmp2_ref[r1 * quarter:(r1 + 1) * quarter, :]

    for g0 in range(0, len(near_geo), STAGE_TILES):
        geo = near_geo[g0:g0 + STAGE_TILES]
        scores = [_qk(qb_ref[rows, :], kb_ref[keys, :]) + bn_ref[:, (NEAR_KBLK - n_kt) * Q_BLK:]
                  for rows, keys, n_kt in geo]
        probs = []
        for (rows, _, _), s in zip(geo, scores):
            m_far = lf_ref[rows, 0:1]
            m = jnp.maximum(jnp.max(s, axis=-1, keepdims=True), m_far)
            probs.append((jnp.exp2(s - m).astype(BF16), jnp.exp2(m_far - m)))
        for (rows, keys, _), (p, w_far) in zip(geo, probs):
            o = jnp.dot(p, vb_ref[keys, :], preferred_element_type=F32)
            acc = w_far * of_ref[rows, :] + o[:, val]
            o_ref[rows, :] = acc / (w_far + o[:, HEAD_DIM:])


def _prompt_attn(z, bias_near, bias_far):
    b, s, _ = z.shape
    assert s % (FAR_D * Q_BLK) == 0

    def head_cols(cb):
        return pl.BlockSpec((None, s, HEAD_DIM), lambda bi, h: (bi, 0, cb * HEADS_PER_COL_BLK + h))

    return pl.pallas_call(
        functools.partial(_prompt_attn_kernel, seq=s),
        grid=(b, N_ATT_HEADS),
        in_specs=[
            head_cols(CB_Q), head_cols(CB_K), head_cols(CB_V),
            pl.BlockSpec((None, Q_BLK, NEAR_KBLK * Q_BLK), lambda bi, h: (h, 0, 0)),
            pl.BlockSpec((None, Q_BLK, 2 * Q_BLK), lambda bi, h: (h, 0, 0)),
        ],
        out_specs=pl.BlockSpec((None, s, HEAD_DIM), lambda bi, h: (bi, 0, h)),
        out_shape=jax.ShapeDtypeStruct((b, s, ATT_W), F32),
        scratch_shapes=[pltpu.VMEM((s, HEAD_DIM), BF16), pltpu.VMEM((s, HEAD_DIM), BF16),
                        pltpu.VMEM((s, 2 * HEAD_DIM), BF16)] * 2 + [pltpu.VMEM((s, HEAD_DIM), F32)] * 4,
        compiler_params=pltpu.CompilerParams(
            dimension_semantics=("parallel", "parallel"), vmem_limit_bytes=VMEM_LIMIT),
        name="prompt_attn",
    )(z, z, z, bias_near, bias_far)


def _sample_attn_kernel(q_ref, kn_ref, vn_ref, kc_ref, vc_ref, bias_ref, biasn_ref, sk_hbm, sv_hbm,
                        o_ref, ko_ref, vo_ref, qa_ref, kn3_ref, vn3_ref, m_ref, l_ref, acc_ref, *, t_len, rb):
    del sk_hbm, sv_hbm
    j = pl.program_id(1)
    last = pl.num_programs(1) - 1
    n_heads = N_ATT_HEADS

    @pl.when(j == 0)
    def _():
        for h in range(n_heads):
            cols = slice(h * HEAD_DIM, (h + 1) * HEAD_DIM)
            qa_ref[h * t_len:(h + 1) * t_len, :] = q_ref[:, cols] * SCALE
            kn3_ref[:, h, :] = kn_ref[:, cols]
            vn3_ref[:, h, :] = vn_ref[:, cols]
        m_ref[...] = jnp.full(m_ref.shape, NEG, F32)
        l_ref[...] = jnp.zeros(l_ref.shape, F32)
        acc_ref[...] = jnp.zeros(acc_ref.shape, F32)

    qa = qa_ref[...].astype(BF16)

    def accumulate(k, v, bias):
        s = _qk(qa, k.astype(BF16)) + bias
        m_old = m_ref[...]
        m_new = jnp.maximum(m_old, jnp.max(s, axis=-1, keepdims=True))
        alpha = jnp.exp(m_old - m_new)
        p = jnp.exp(s - m_new)
        l_ref[...] = alpha * l_ref[...] + jnp.sum(p, axis=-1, keepdims=True)
        acc_ref[...] = alpha * acc_ref[...] + jnp.dot(p.astype(BF16), v.astype(BF16),
                                                      preferred_element_type=F32)
        m_ref[...] = m_new

    accumulate(kc_ref[...].reshape(rb * n_heads, HEAD_DIM), vc_ref[...].reshape(rb * n_heads, HEAD_DIM),
               bias_ref[...])

    @pl.when(j == last)
    def _():
        ko_ref[...] = kn3_ref[...]
        vo_ref[...] = vn3_ref[...]
        accumulate(kn3_ref[...].reshape(t_len * n_heads, HEAD_DIM),
                   vn3_ref[...].reshape(t_len * n_heads, HEAD_DIM), biasn_ref[...])
        o = acc_ref[...] / l_ref[...]
        for h in range(n_heads):
            o_ref[:, h * HEAD_DIM:(h + 1) * HEAD_DIM] = o[h * t_len:(h + 1) * t_len, :]


def _sample_attn(z, cache_k, cache_v, bias, bias_new, stack_k, stack_v, *, li):
    b, t_len, _ = z.shape
    l_buf = cache_k.shape[2]
    rb = SAMPLE_RB
    assert t_len == SUBLANES and l_buf % rb == 0
    rows = N_ATT_HEADS * t_len

    def zspec(cb):
        return pl.BlockSpec((None, t_len, ATT_W), lambda bi, j: (bi, 0, cb * COL_BLK // ATT_W))

    blk_spec = pl.BlockSpec((None, None, rb, N_ATT_HEADS, HEAD_DIM), lambda bi, j: (li, bi, j, 0, 0))
    new_spec = pl.BlockSpec((None, None, t_len, N_ATT_HEADS, HEAD_DIM),
                            lambda bi, j: (li, bi, l_buf // t_len - 1, 0, 0))
    any_spec = pl.BlockSpec(memory_space=pl.ANY)
    in_specs = [
        zspec(CB_Q), zspec(CB_K), zspec(CB_V), blk_spec, blk_spec,
        pl.BlockSpec((rows, rb * N_ATT_HEADS), lambda bi, j: (0, j)),
        pl.BlockSpec((rows, rows), lambda bi, j: (0, 0)),
        any_spec, any_spec,
    ]
    args = [z, z, z, cache_k, cache_v, bias, bias_new, stack_k, stack_v]
    return pl.pallas_call(
        functools.partial(_sample_attn_kernel, t_len=t_len, rb=rb),
        grid=(b, l_buf // rb),
        in_specs=in_specs,
        out_specs=[pl.BlockSpec((None, t_len, ATT_W), lambda bi, j: (bi, 0, 0)), new_spec, new_spec],
        out_shape=[
            jax.ShapeDtypeStruct((b, t_len, ATT_W), F32),
            jax.ShapeDtypeStruct(cache_k.shape, F32),
            jax.ShapeDtypeStruct(cache_v.shape, F32),
        ],
        scratch_shapes=[
            pltpu.VMEM((rows, HEAD_DIM), F32),
            pltpu.VMEM((t_len, N_ATT_HEADS, HEAD_DIM), F32), pltpu.VMEM((t_len, N_ATT_HEADS, HEAD_DIM), F32),
            pltpu.VMEM((rows, 1), F32), pltpu.VMEM((rows, 1), F32), pltpu.VMEM((rows, HEAD_DIM), F32),
        ],
        input_output_aliases={7: 1, 8: 2},
        compiler_params=pltpu.CompilerParams(
            dimension_semantics=("arbitrary", "arbitrary"), vmem_limit_bytes=VMEM_LIMIT),
        name="sample_attn",
    )(*args)


def _silu(x):
    return x * jax.nn.sigmoid(x)


def _mix_kernel(x_ref, a_ref, ga_ref, uv_ref, ug_ref, gc_ref, qm_ref, gm_ref, cinit_ref, mk_ref, mv_ref,
                wdw_ref, bdw_ref, lng_ref, lnb_ref, wpw_ref, wout_ref, gpost_ref,
                xo_ref, cs_ref, uext_ref, mix_ref, *, nb, tile, conv_chunk):
    @pl.when(pl.program_id(1) == 0)
    def _():
        uext_ref[:, 0:HALO, :] = cinit_ref[...]

    first = HALO - (CONV_K - 1)
    chunks = []
    for b in range(nb):
        ub_ref = uext_ref.at[b]
        ub_ref[HALO:HALO + tile, :] = uv_ref[b] * jax.nn.sigmoid(ug_ref[b])
        for c0 in range(0, tile, conv_chunk):
            acc = jnp.broadcast_to(bdw_ref[...], (conv_chunk, CONV_CH))
            for phase in range(SUBLANES):
                rows = conv_chunk + (SUBLANES if phase else 0)
                part = None
                for k in range(CONV_K):
                    if (first + k) % SUBLANES == phase:
                        base = c0 + first + k - phase
                        term = ub_ref[base:base + rows, :] * wdw_ref[k:k + 1, :]
                        part = term if part is None else part + term
                if part is not None:
                    acc = acc + part[phase:phase + conv_chunk, :]
            chunks.append(acc)
        cs_ref[b] = ub_ref[tile + first:tile + HALO, :]
        tail = ub_ref[tile:tile + HALO, :]
        ub_ref[0:HALO, :] = tail
    c = jnp.concatenate(chunks, axis=0) if len(chunks) > 1 else chunks[0]

    mu = jnp.mean(c, axis=-1, keepdims=True)
    var = jnp.mean(jnp.square(c - mu), axis=-1, keepdims=True)
    c = (c - mu) * lax.rsqrt(var + EPS) * lng_ref[...] + lnb_ref[...]
    c = jnp.dot(_silu(c).astype(BF16), wpw_ref[...], preferred_element_type=F32)

    def mem_head(ref, h):
        if len(ref.shape) == 3:
            return ref[:, h, :].astype(BF16)
        return ref[:, h * HEAD_DIM:(h + 1) * HEAD_DIM].astype(BF16)

    for b in range(nb):
        rows = slice(b * tile, (b + 1) * tile)
        mix_ref[rows, 0:ATT_W] = (a_ref[b] * _silu(ga_ref[b])).astype(BF16)
        mix_ref[rows, ATT_W:ATT_W + CONV_CH] = (c[rows] * _silu(gc_ref[b])).astype(BF16)
        for h in range(N_X_HEADS):
            cols = slice(h * HEAD_DIM, (h + 1) * HEAD_DIM)
            q = (qm_ref[b, :, cols] * SCALE).astype(BF16)
            s = _qk(q, mem_head(mk_ref.at[b], h))
            p = jnp.exp(s - jnp.max(s, axis=-1, keepdims=True))
            o = jnp.dot(p.astype(BF16), mem_head(mv_ref.at[b], h), preferred_element_type=F32)
            o = o / jnp.sum(p, axis=-1, keepdims=True)
            off = ATT_W + CONV_CH + h * HEAD_DIM
            mix_ref[rows, off:off + HEAD_DIM] = (o * _silu(gm_ref[b, :, cols])).astype(BF16)

    y = jnp.dot(mix_ref[...], wout_ref[...], preferred_element_type=F32)
    y = y * lax.rsqrt(jnp.mean(y * y, axis=-1, keepdims=True) + EPS) * gpost_ref[...]
    for b in range(nb):
        xo_ref[b] = x_ref[b] + y[b * tile:(b + 1) * tile]


def _mix(x, z, a, cinit, mk_arr, mk_block, mv_arr, mv_block, w_dw, b_dw, ln_g, ln_b, w_pw2, w_out, g_post,
         *, li, nb, tile):
    b, s, _ = x.shape
    conv_chunk = min(tile, 32)
    assert b % nb == 0 and s % tile == 0

    def rows(width, cb=0):
        return pl.BlockSpec((nb, tile, width), lambda bi, t: (bi, t, cb * COL_BLK // width))

    def const(shape):
        return pl.BlockSpec(shape, lambda bi, t: (0,) * len(shape))

    def layer(shape):
        return pl.BlockSpec((None,) + shape, lambda bi, t: (li,) + (0,) * len(shape))

    def mem(block):
        shape, index = block
        return pl.BlockSpec(shape, lambda bi, t: index(bi))

    return pl.pallas_call(
        functools.partial(_mix_kernel, nb=nb, tile=tile, conv_chunk=conv_chunk),
        grid=(b // nb, s // tile),
        in_specs=[
            rows(D_MODEL), rows(ATT_W),
            rows(ATT_W, CB_GA), rows(COL_BLK, CB_UV), rows(COL_BLK, CB_UG), rows(COL_BLK, CB_GC),
            rows(COL_BLK, CB_QM), rows(COL_BLK, CB_GM),
            pl.BlockSpec((nb, HALO, CONV_CH), lambda bi, t: (bi, 0, 0)),
            mem(mk_block), mem(mv_block),
            const((CONV_K, CONV_CH)), const((1, CONV_CH)), const((1, CONV_CH)), const((1, CONV_CH)),
            layer((CONV_CH, CONV_CH)), layer((MIX_W, D_MODEL)), const((1, D_MODEL)),
        ],
        out_specs=[
            rows(D_MODEL),
            pl.BlockSpec((nb, CONV_K - 1, CONV_CH), lambda bi, t: (bi, 0, 0)),
        ],
        out_shape=[
            jax.ShapeDtypeStruct((b, s, D_MODEL), F32),
            jax.ShapeDtypeStruct((b, CONV_K - 1, CONV_CH), F32),
        ],
        scratch_shapes=[pltpu.VMEM((nb, HALO + tile, CONV_CH), F32), pltpu.VMEM((nb * tile, MIX_W), BF16)],
        compiler_params=pltpu.CompilerParams(
            dimension_semantics=("parallel", "arbitrary"), vmem_limit_bytes=VMEM_LIMIT),
        name="mix",
    )(x, a, z, z, z, z, z, z, cinit, mk_arr, mv_arr,
      w_dw, b_dw.reshape(1, -1), ln_g.reshape(1, -1), ln_b.reshape(1, -1), w_pw2, w_out,
      g_post.reshape(1, -1))


def kernel(x_prompt, x_sample, mem_prompt, cache_attn_k, cache_attn_v, state_conv, cache_mem_k, cache_mem_v,
           rel_bias, norm_pre_g, w_in, w_dw, b_dw, ln_conv_g, ln_conv_b, w_pw2, w_mem_kv, w_out, norm_post_g):
    bp, s_len, _ = x_prompt.shape
    bs, t_len, _ = x_sample.shape
    l_buf = cache_attn_k.shape[2]
    l_prompt = min(WIN, s_len)

    w_pw2_b = w_pw2.astype(BF16)
    w_mem_b, w_out_b = w_mem_kv.astype(BF16), w_out.astype(BF16)

    tab_near = _distance_table(rel_bias, np.arange(NEAR_REACH + 1), NEAR_PATTERNS)
    tab_far = _distance_table(rel_bias, FAR_D * np.arange(FAR_KEYS + 1), DIL_PATTERNS[-1:])
    bias_near = _toeplitz(tab_near * LOG2E, NEAR_REACH, Q_BLK, NEAR_KBLK * Q_BLK)
    bias_far = _toeplitz(tab_far * LOG2E, FAR_KEYS, Q_BLK, 2 * Q_BLK)
    tab_all = _distance_table(rel_bias, np.arange(WIN + 1), DIL_PATTERNS)
    bias_s = _head_matched_bias(tab_all, l_buf, t_len, l_buf)
    bias_s_new = _head_matched_bias(tab_all, 0, t_len, t_len)

    mkv = _mem_proj(mem_prompt.reshape(bp * N_MEM, D_MODEL), w_mem_b).reshape(DEPTH, bp, N_MEM, 2 * X_W)
    cinit_p = jnp.zeros((bp, HALO, CONV_CH), F32)
    pad_s = jnp.zeros((DEPTH, bs, HALO - (CONV_K - 1), CONV_CH), F32)
    cinit_s = jnp.concatenate([pad_s, state_conv], axis=2)
    xp, xs = x_prompt, x_sample
    stacks = None
    cvp, cvs = [], []
    for li in range(DEPTH):
        wargs = (w_dw[li], b_dw[li], ln_conv_g[li], ln_conv_b[li], w_pw2_b, w_out_b, norm_post_g[li])

        zs, w_in_li = _proj_cast(xs.reshape(bs * t_len, D_MODEL), norm_pre_g[li], w_in, li=li, bn=COL_BLK)
        zs = zs.reshape(bs, t_len, IN_W)

        zp, akp, avp, aks, avs = _proj_kv(
            xp.reshape(bp * s_len, D_MODEL), norm_pre_g[li], w_in_li, cache_attn_k, cache_attn_v, stacks,
            li=li, seq=s_len, l_keep=l_prompt, shift=t_len)
        zp = zp.reshape(bp, s_len, IN_W)
        a = _prompt_attn(zp, bias_near, bias_far)
        mk_block, mv_block = (((None, MIX_NB, N_MEM, X_W), lambda bi, li=li, half=half: (li, bi, 0, half))
                              for half in (0, 1))
        xp, cst = _mix(xp, zp, a, cinit_p, mkv, mk_block, mkv, mv_block, *wargs,
                       li=li, nb=MIX_NB, tile=MIX_TILE)
        cvp.append(cst)

        a, aks, avs = _sample_attn(zs, cache_attn_k, cache_attn_v, bias_s, bias_s_new, aks, avs, li=li)
        stacks = (akp, avp, aks, avs)
        mem_block = ((None, bs, N_MEM, N_X_HEADS, HEAD_DIM), lambda bi, li=li: (li, bi, 0, 0, 0))
        xs, cst = _mix(xs, zs, a, cinit_s[li], cache_mem_k, mem_block, cache_mem_v, mem_block, *wargs,
                       li=li, nb=bs, tile=t_len)
        cvs.append(cst)

    kv5 = (DEPTH, bp, l_prompt, N_ATT_HEADS, HEAD_DIM)
    mem5 = (DEPTH, bp, N_MEM, N_X_HEADS, HEAD_DIM)
    return (xp, xs, akp.reshape(kv5), avp.reshape(kv5), jnp.stack(cvp),
            mkv[..., :X_W].reshape(mem5), mkv[..., X_W:].reshape(mem5), aks, avs, jnp.stack(cvs))
```

```python
import functools

import numpy as np
import jax
import jax.numpy as jnp
from jax import lax
from jax.experimental import pallas as pl
from jax.experimental.pallas import tpu as pltpu
from jax.experimental.pallas import tpu_sc as plsc

D_MODEL = 2048
DEPTH = 4
N_MEM = 256
HEAD_DIM = 128
ATT_W = D_MODEL // 2
N_ATT_HEADS = ATT_W // HEAD_DIM
DIL_PATTERNS = ((128, 1), (512, 4), (2048, 16))
WIN = max(w for w, _ in DIL_PATTERNS)
N_BUCKETS = 32
MAX_DIST = WIN
CONV_CH = D_MODEL // 4
CONV_K = 31
X_W = D_MODEL // 4
N_X_HEADS = X_W // HEAD_DIM
MIX_W = ATT_W + CONV_CH + X_W
IN_W = 4 * ATT_W + 3 * CONV_CH + 2 * X_W
EPS = 1e-6
NEG = -1e30
SCALE = HEAD_DIM ** -0.5
LOG2E = 1.4426950408889634

COL_BLK = 512
CB_Q, CB_K, CB_V, CB_GA = 0, 2, 4, 6
CB_UV, CB_UG, CB_GC, CB_QM, CB_GM = 8, 9, 10, 11, 12
HEADS_PER_COL_BLK = COL_BLK // HEAD_DIM

Q_BLK = 128
FAR_W, FAR_D = DIL_PATTERNS[-1]
NEAR_PATTERNS = DIL_PATTERNS[:-1]
NEAR_REACH = max(w for w, _ in NEAR_PATTERNS)
NEAR_KBLK = NEAR_REACH // Q_BLK + 1
FAR_KEYS = FAR_W // FAR_D
FAR_STEP = 4
assert FAR_KEYS == Q_BLK and NEAR_REACH % Q_BLK == 0 and FAR_STEP * FAR_STEP == FAR_D
MIX_NB, MIX_TILE = 1, 512
STAGE_TILES = 16
PROJ_BM = 1024
SC_COPY_ROWS = 60
SAMPLE_RB = 1024
HALO = 32
SUBLANES = 8
LANES = 128
VMEM_LIMIT = 56 * 1024 * 1024

BF16 = jnp.bfloat16
F32 = jnp.float32


def _t5_bucket(dist):
    dist = np.asarray(dist)
    max_exact = N_BUCKETS // 2
    large = max_exact + (np.log(np.maximum(dist, 1) / max_exact)
                         / np.log(MAX_DIST / max_exact) * (N_BUCKETS - max_exact)).astype(np.int32)
    large = np.minimum(large, N_BUCKETS - 1)
    return np.where(dist < max_exact, dist, large).astype(np.int32)


def _distance_table(rel_bias, dists, patterns):
    dists = np.asarray(dists)
    cnt = np.zeros(dists.shape, np.int64)
    for w, dil in patterns:
        cnt += ((dists % dil == 0) & (dists <= w)).astype(np.int64)
    logc = jnp.log(jnp.asarray(np.maximum(cnt, 1), F32))
    tab = rel_bias[_t5_bucket(dists)].T.astype(F32) + logc[None, :]
    tab = jnp.where(jnp.asarray(cnt > 0)[None, :], tab, NEG)
    return jnp.concatenate([tab, jnp.full((tab.shape[0], 1), NEG, F32)], axis=1)


def _toeplitz(tab, off, rows, cols):
    n_valid = tab.shape[1] - 1
    lp = rows + cols
    j = np.concatenate([np.arange(cols), np.full(lp - (rows + cols - 1), off + n_valid),
                        np.arange(-(rows - 1), 0)])
    d = off - j
    idx = np.where((d >= 0) & (d < n_valid), d, n_valid)
    v = jnp.take(tab, jnp.asarray(idx, jnp.int32), axis=1)
    flat = jnp.tile(v, (1, rows))[:, :rows * (lp - 1)]
    return flat.reshape(tab.shape[0], rows, lp - 1)[:, :, :cols]


def _head_matched_bias(tab, off, t_len, n_rows):
    n_h, n_valid = tab.shape[0], tab.shape[1] - 1
    n_cols = n_rows * n_h
    gaps = jnp.full((n_h, n_valid, n_h - 1), NEG, F32)
    s = jnp.concatenate([tab[:, :n_valid, None], gaps], axis=2).reshape(n_h, n_valid * n_h)
    k_max = n_h * (off + t_len - 1) + n_h - 1
    pad_l = max(0, n_cols - 1 - n_h * off)
    pad_r = max(0, k_max - (n_valid * n_h - 1))
    s = jnp.concatenate([jnp.full((n_h, pad_l), NEG, F32), s, jnp.full((n_h, pad_r), NEG, F32)], axis=1)
    n_s = s.shape[1]
    lane_pad = -n_s % LANES
    rev = jnp.concatenate([s[:, ::-1], jnp.full((n_h, lane_pad), NEG, F32)], axis=1)
    starts = tuple((h, n_s - 1 - (n_h * (off + t) + h) - pad_l) for h in range(n_h) for t in range(t_len))
    return pl.pallas_call(
        functools.partial(_slice_rows_kernel, starts=starts, n_cols=n_cols),
        out_shape=jax.ShapeDtypeStruct((len(starts), n_cols), F32),
        name="bias_rows",
    )(rev)


def _slice_rows_kernel(src_ref, out_ref, *, starts, n_cols):
    for r, (h, start) in enumerate(starts):
        lo = start // LANES * LANES
        hi = -(-(start + n_cols) // LANES) * LANES
        window = src_ref[h:h + 1, lo:hi]
        out_ref[r:r + 1, :] = window[:, start - lo:start - lo + n_cols]


def _norm_rows(x_ref, g_ref, h_ref, normalize):
    x = x_ref[...]
    if normalize:
        x = x * lax.rsqrt(jnp.mean(x * x, axis=-1, keepdims=True) + EPS) * g_ref[...]
    h_ref[...] = x.astype(BF16)


def _proj_kernel(x_ref, g_ref, w_ref, o_ref, h_ref, *, normalize):
    @pl.when(pl.program_id(1) == 0)
    def _():
        _norm_rows(x_ref, g_ref, h_ref, normalize)

    o_ref[...] = jnp.dot(h_ref[...], w_ref[...], preferred_element_type=F32)


def _proj_cast_kernel(x_ref, g_ref, w_ref, o_ref, wb_ref, h_ref):
    @pl.when(pl.program_id(0) == 0)
    def _():
        _norm_rows(x_ref, g_ref, h_ref, True)

    w = w_ref[...].astype(BF16)
    wb_ref[...] = w
    o_ref[...] = jnp.dot(h_ref[...], w, preferred_element_type=F32)


def _proj_cast(x, g, w, *, li, bn):
    m, d = x.shape
    n = w.shape[2]
    return pl.pallas_call(
        _proj_cast_kernel,
        grid=(n // bn,),
        in_specs=[
            pl.BlockSpec((m, d), lambda j: (0, 0)),
            pl.BlockSpec((1, d), lambda j: (0, 0)),
            pl.BlockSpec((None, d, bn), lambda j: (li, 0, j)),
        ],
        out_specs=[pl.BlockSpec((m, bn), lambda j: (0, j)), pl.BlockSpec((d, bn), lambda j: (0, j))],
        out_shape=[jax.ShapeDtypeStruct((m, n), F32), jax.ShapeDtypeStruct((d, n), BF16)],
        scratch_shapes=[pltpu.VMEM((m, d), BF16)],
        compiler_params=pltpu.CompilerParams(
            dimension_semantics=("arbitrary",), vmem_limit_bytes=VMEM_LIMIT),
        name="proj_cast",
    )(x, g.reshape(1, d), w)


def _mem_proj(x, w):
    m, d = x.shape
    depth, _, n = w.shape
    bn = COL_BLK
    return pl.pallas_call(
        functools.partial(_proj_kernel, normalize=False),
        grid=(depth, n // bn),
        in_specs=[
            pl.BlockSpec((m, d), lambda li, j: (0, 0)),
            pl.BlockSpec((1, d), lambda li, j: (0, 0)),
            pl.BlockSpec((None, d, bn), lambda li, j: (li, 0, j)),
        ],
        out_specs=pl.BlockSpec((None, m, bn), lambda li, j: (li, 0, j)),
        out_shape=jax.ShapeDtypeStruct((depth, m, n), F32),
        scratch_shapes=[pltpu.VMEM((m, d), BF16)],
        compiler_params=pltpu.CompilerParams(
            dimension_semantics=("arbitrary", "arbitrary"), vmem_limit_bytes=VMEM_LIMIT),
        name="mem_proj",
    )(x, jnp.ones((1, d), F32), w)


def _proj_kv_kernel(*refs, tiles_per_seq, tail_tiles, aliased):
    x_ref, g_ref, w_ref = refs[:3]
    o_ref, ko_ref, vo_ref, h_ref = refs[5:] if aliased else refs[3:]
    i, j = pl.program_id(0), pl.program_id(1)

    @pl.when(j == 0)
    def _():
        _norm_rows(x_ref, g_ref, h_ref, True)

    o_ref[...] = jnp.dot(h_ref[...], w_ref[...], preferred_element_type=F32)

    in_tail = i % tiles_per_seq >= tiles_per_seq - tail_tiles
    for cb0, dst in ((CB_K, ko_ref), (CB_V, vo_ref)):
        for cb in range(cb0, cb0 + ATT_W // COL_BLK):
            @pl.when(in_tail & (j == cb))
            def _(cb=cb, cb0=cb0, dst=dst):
                for hh in range(HEADS_PER_COL_BLK):
                    head = (cb - cb0) * HEADS_PER_COL_BLK + hh
                    dst[pl.ds(head, o_ref.shape[0], stride=N_ATT_HEADS), :] = (
                        o_ref[:, hh * HEAD_DIM:(hh + 1) * HEAD_DIM])


def _shift_caches(cache_k, cache_v, *, shift):
    depth, n_seq, l_buf = cache_k.shape[:3]
    info = plsc.get_sparse_core_info()
    n_workers = info.num_cores * info.num_subcores
    n_jobs = depth * n_seq
    assert n_jobs % n_workers == 0 and (l_buf - shift) % SC_COPY_ROWS == 0
    n_chunks = (l_buf - shift) // SC_COPY_ROWS
    mesh = plsc.VectorSubcoreMesh(core_axis_name="core", subcore_axis_name="subcore")
    out = jax.ShapeDtypeStruct(cache_k.shape, cache_k.dtype)

    @pl.kernel(out_type=(out, out), mesh=mesh,
               scratch_types=[pltpu.VMEM((SC_COPY_ROWS,) + cache_k.shape[3:], cache_k.dtype)],
               name="shift_caches")
    def shift_kernel(ck_hbm, cv_hbm, ok_hbm, ov_hbm, buf):
        worker = lax.axis_index("core") * info.num_subcores + lax.axis_index("subcore")
        for job in range(n_jobs // n_workers):
            seq_id = worker * (n_jobs // n_workers) + job
            li, b = seq_id // n_seq, seq_id % n_seq
            for src, dst in ((ck_hbm, ok_hbm), (cv_hbm, ov_hbm)):
                @pl.loop(0, n_chunks)
                def _(c, src=src, dst=dst, li=li, b=b):
                    r0 = c * SC_COPY_ROWS
                    pltpu.sync_copy(src.at[li, b, pl.ds(r0 + shift, SC_COPY_ROWS)], buf)
                    pltpu.sync_copy(buf, dst.at[li, b, pl.ds(r0, SC_COPY_ROWS)])

    return shift_kernel(cache_k, cache_v)


def _proj_kv(x, g, w, prev, *, li, seq, l_keep):
    m, d = x.shape
    n = w.shape[1]
    bsz = m // seq
    bm, bn = PROJ_BM, COL_BLK
    tiles_per_seq, tail_tiles = seq // bm, l_keep // bm
    assert seq % bm == 0 and l_keep % bm == 0
    aliased = prev is not None
    kv_shape = jax.ShapeDtypeStruct((DEPTH, bsz, l_keep * N_ATT_HEADS, HEAD_DIM), F32)
    any_spec = pl.BlockSpec(memory_space=pl.ANY)

    def kv_map(i, j):
        return (li, i // tiles_per_seq, jnp.maximum(i % tiles_per_seq - (tiles_per_seq - tail_tiles), 0), 0)

    kv_spec = pl.BlockSpec((None, None, bm * N_ATT_HEADS, HEAD_DIM), kv_map)
    in_specs = [
        pl.BlockSpec((bm, d), lambda i, j: (i, 0)),
        pl.BlockSpec((1, d), lambda i, j: (0, 0)),
        pl.BlockSpec((d, bn), lambda i, j: (0, j)),
    ]
    args = [x, g.reshape(1, d), w]
    if aliased:
        in_specs += [any_spec] * 2
        args += list(prev)
    return pl.pallas_call(
        functools.partial(_proj_kv_kernel, tiles_per_seq=tiles_per_seq, tail_tiles=tail_tiles,
                          aliased=aliased),
        grid=(m // bm, n // bn),
        in_specs=in_specs,
        out_specs=[pl.BlockSpec((bm, bn), lambda i, j: (i, j)), kv_spec, kv_spec],
        out_shape=[jax.ShapeDtypeStruct((m, n), F32), kv_shape, kv_shape],
        scratch_shapes=[pltpu.VMEM((bm, d), BF16)],
        input_output_aliases={3: 1, 4: 2} if aliased else {},
        compiler_params=pltpu.CompilerParams(
            dimension_semantics=("arbitrary", "arbitrary"), vmem_limit_bytes=VMEM_LIMIT),
        name="proj_kv",
    )(*args)


def _qk(q, k):
    return lax.dot_general(q, k, (((1,), (1,)), ((), ())), preferred_element_type=F32)


def _prompt_attn_kernel(q_ref, k_ref, v_ref, bn_ref, bf_ref, o_ref,
                        qb_ref, kb_ref, vb_ref, qf_ref, kf_ref, vf_ref, of_ref, lf_ref, tmp_ref, tmp2_ref,
                        *, seq):
    sub = seq // FAR_D
    chunk = 2 * Q_BLK
    val = slice(0, HEAD_DIM)

    for r0 in range(0, seq, chunk):
        rows = pl.ds(r0, chunk)
        qb_ref[rows, :] = (q_ref[rows, :] * (SCALE * LOG2E)).astype(BF16)
        kb_ref[rows, :] = k_ref[rows, :].astype(BF16)
        vb_ref[rows, val] = v_ref[rows, :].astype(BF16)
        vb_ref[rows, HEAD_DIM:] = jnp.ones((chunk, HEAD_DIM), BF16)
        vf_ref[rows, HEAD_DIM:] = jnp.ones((chunk, HEAD_DIM), BF16)

    quarter = seq // FAR_STEP
    for src_ref, dst_ref, scale in ((q_ref, qf_ref, SCALE * LOG2E), (k_ref, kf_ref, None),
                                    (v_ref, vf_ref, None)):
        for r1 in range(FAR_STEP):
            tmp_ref[r1 * quarter:(r1 + 1) * quarter, :] = src_ref[pl.ds(r1, quarter, stride=FAR_STEP), :]
        for r1 in range(FAR_STEP):
            for r2 in range(FAR_STEP):
                r = FAR_STEP * r2 + r1
                x = tmp_ref[pl.ds(r1 * quarter + r2, sub, stride=FAR_STEP), :]
                if scale is not None:
                    x = x * scale
                dst_ref[r * sub:(r + 1) * sub, val] = x.astype(BF16)

    far_geo = []
    for r in range(FAR_D):
        for a_blk in range(sub // Q_BLK):
            n_kt = min(a_blk, 1) + 1
            rows = pl.ds(r * sub + a_blk * Q_BLK, Q_BLK)
            keys = pl.ds(r * sub + (a_blk + 1 - n_kt) * Q_BLK, n_kt * Q_BLK)
            r1, r2 = r % FAR_STEP, r // FAR_STEP
            dst = pl.ds(r1 * quarter + FAR_STEP * a_blk * Q_BLK + r2, Q_BLK, stride=FAR_STEP)
            far_geo.append((rows, keys, n_kt, dst))
    near_geo = []
    for i in range(seq // Q_BLK):
        n_kt = min(i + 1, NEAR_KBLK)
        near_geo.append((pl.ds(i * Q_BLK, Q_BLK), pl.ds((i + 1 - n_kt) * Q_BLK, n_kt * Q_BLK), n_kt))

    for g0 in range(0, len(far_geo), STAGE_TILES):
        geo = far_geo[g0:g0 + STAGE_TILES]
        scores = [_qk(qf_ref[rows, :], kf_ref[keys, :]) + bf_ref[:, (2 - n_kt) * Q_BLK:]
                  for rows, keys, n_kt, _ in geo]
        probs = []
        for s in scores:
            m = jnp.max(s, axis=-1, keepdims=True)
            probs.append((jnp.exp2(s - m).astype(BF16), m))
        for (_, keys, _, dst), (p, m) in zip(geo, probs):
            o = jnp.dot(p, vf_ref[keys, :], preferred_element_type=F32)
            l = o[:, HEAD_DIM:]
            tmp_ref[dst, :] = o[:, val] / l
            tmp2_ref[dst, :] = m + jnp.log2(l)
    for r1 in range(FAR_STEP):
        of_ref[pl.ds(r1, quarter, stride=FAR_STEP), :] = tmp_ref[r1 * quarter:(r1 + 1) * quarter, :]
        lf_ref[pl.ds(r1, quarter, stride=FAR_STEP), :] = tmp2_ref[r1 * quarter:(r1 + 1) * quarter, :]

    for g0 in range(0, len(near_geo), STAGE_TILES):
        geo = near_geo[g0:g0 + STAGE_TILES]
        scores = [_qk(qb_ref[rows, :], kb_ref[keys, :]) + bn_ref[:, (NEAR_KBLK - n_kt) * Q_BLK:]
                  for rows, keys, n_kt in geo]
        probs = []
        for (rows, _, _), s in zip(geo, scores):
            m_far = lf_ref[rows, 0:1]
            m = jnp.maximum(jnp.max(s, axis=-1, keepdims=True), m_far)
            probs.append((jnp.exp2(s - m).astype(BF16), jnp.exp2(m_far - m)))
        for (rows, keys, _), (p, w_far) in zip(geo, probs):
            o = jnp.dot(p, vb_ref[keys, :], preferred_element_type=F32)
            acc = w_far * of_ref[rows, :] + o[:, val]
            o_ref[rows, :] = acc / (w_far + o[:, HEAD_DIM:])


def _prompt_attn(z, bias_near, bias_far):
    b, s, _ = z.shape
    assert s % (FAR_D * Q_BLK) == 0

    def head_cols(cb):
        return pl.BlockSpec((None, s, HEAD_DIM), lambda bi, h: (bi, 0, cb * HEADS_PER_COL_BLK + h))

    return pl.pallas_call(
        functools.partial(_prompt_attn_kernel, seq=s),
        grid=(b, N_ATT_HEADS),
        in_specs=[
            head_cols(CB_Q), head_cols(CB_K), head_cols(CB_V),
            pl.BlockSpec((None, Q_BLK, NEAR_KBLK * Q_BLK), lambda bi, h: (h, 0, 0)),
            pl.BlockSpec((None, Q_BLK, 2 * Q_BLK), lambda bi, h: (h, 0, 0)),
        ],
        out_specs=pl.BlockSpec((None, s, HEAD_DIM), lambda bi, h: (bi, 0, h)),
        out_shape=jax.ShapeDtypeStruct((b, s, ATT_W), F32),
        scratch_shapes=[pltpu.VMEM((s, HEAD_DIM), BF16), pltpu.VMEM((s, HEAD_DIM), BF16),
                        pltpu.VMEM((s, 2 * HEAD_DIM), BF16)] * 2 + [pltpu.VMEM((s, HEAD_DIM), F32)] * 4,
        compiler_params=pltpu.CompilerParams(
            dimension_semantics=("parallel", "parallel"), vmem_limit_bytes=VMEM_LIMIT),
        name="prompt_attn",
    )(z, z, z, bias_near, bias_far)


def _sample_attn_kernel(q_ref, kn_ref, vn_ref, kc_ref, vc_ref, bias_ref, biasn_ref, sk_hbm, sv_hbm,
                        o_ref, ko_ref, vo_ref, qa_ref, kn3_ref, vn3_ref, m_ref, l_ref, acc_ref, *, t_len, rb):
    del sk_hbm, sv_hbm
    j = pl.program_id(1)
    last = pl.num_programs(1) - 1
    n_heads = N_ATT_HEADS

    @pl.when(j == 0)
    def _():
        for h in range(n_heads):
            cols = slice(h * HEAD_DIM, (h + 1) * HEAD_DIM)
            qa_ref[h * t_len:(h + 1) * t_len, :] = q_ref[:, cols] * SCALE
            kn3_ref[:, h, :] = kn_ref[:, cols]
            vn3_ref[:, h, :] = vn_ref[:, cols]
        m_ref[...] = jnp.full(m_ref.shape, NEG, F32)
        l_ref[...] = jnp.zeros(l_ref.shape, F32)
        acc_ref[...] = jnp.zeros(acc_ref.shape, F32)

    qa = qa_ref[...].astype(BF16)

    def accumulate(k, v, bias):
        s = _qk(qa, k.astype(BF16)) + bias
        m_old = m_ref[...]
        m_new = jnp.maximum(m_old, jnp.max(s, axis=-1, keepdims=True))
        alpha = jnp.exp(m_old - m_new)
        p = jnp.exp(s - m_new)
        l_ref[...] = alpha * l_ref[...] + jnp.sum(p, axis=-1, keepdims=True)
        acc_ref[...] = alpha * acc_ref[...] + jnp.dot(p.astype(BF16), v.astype(BF16),
                                                      preferred_element_type=F32)
        m_ref[...] = m_new

    accumulate(kc_ref[...].reshape(rb * n_heads, HEAD_DIM), vc_ref[...].reshape(rb * n_heads, HEAD_DIM),
               bias_ref[...])

    @pl.when(j == last)
    def _():
        ko_ref[...] = kn3_ref[...]
        vo_ref[...] = vn3_ref[...]
        accumulate(kn3_ref[...].reshape(t_len * n_heads, HEAD_DIM),
                   vn3_ref[...].reshape(t_len * n_heads, HEAD_DIM), biasn_ref[...])
        o = acc_ref[...] / l_ref[...]
        for h in range(n_heads):
            o_ref[:, h * HEAD_DIM:(h + 1) * HEAD_DIM] = o[h * t_len:(h + 1) * t_len, :]


def _sample_attn(z, cache_k, cache_v, bias, bias_new, stack_k, stack_v, *, li):
    b, t_len, _ = z.shape
    l_buf = cache_k.shape[2]
    rb = SAMPLE_RB
    assert t_len == SUBLANES and l_buf % rb == 0
    rows = N_ATT_HEADS * t_len

    def zspec(cb):
        return pl.BlockSpec((None, t_len, ATT_W), lambda bi, j: (bi, 0, cb * COL_BLK // ATT_W))

    blk_spec = pl.BlockSpec((None, None, rb, N_ATT_HEADS, HEAD_DIM), lambda bi, j: (li, bi, j, 0, 0))
    new_spec = pl.BlockSpec((None, None, t_len, N_ATT_HEADS, HEAD_DIM),
                            lambda bi, j: (li, bi, l_buf // t_len - 1, 0, 0))
    any_spec = pl.BlockSpec(memory_space=pl.ANY)
    in_specs = [
        zspec(CB_Q), zspec(CB_K), zspec(CB_V), blk_spec, blk_spec,
        pl.BlockSpec((rows, rb * N_ATT_HEADS), lambda bi, j: (0, j)),
        pl.BlockSpec((rows, rows), lambda bi, j: (0, 0)),
        any_spec, any_spec,
    ]
    args = [z, z, z, cache_k, cache_v, bias, bias_new, stack_k, stack_v]
    return pl.pallas_call(
        functools.partial(_sample_attn_kernel, t_len=t_len, rb=rb),
        grid=(b, l_buf // rb),
        in_specs=in_specs,
        out_specs=[pl.BlockSpec((None, t_len, ATT_W), lambda bi, j: (bi, 0, 0)), new_spec, new_spec],
        out_shape=[
            jax.ShapeDtypeStruct((b, t_len, ATT_W), F32),
            jax.ShapeDtypeStruct(cache_k.shape, F32),
            jax.ShapeDtypeStruct(cache_v.shape, F32),
        ],
        scratch_shapes=[
            pltpu.VMEM((rows, HEAD_DIM), F32),
            pltpu.VMEM((t_len, N_ATT_HEADS, HEAD_DIM), F32), pltpu.VMEM((t_len, N_ATT_HEADS, HEAD_DIM), F32),
            pltpu.VMEM((rows, 1), F32), pltpu.VMEM((rows, 1), F32), pltpu.VMEM((rows, HEAD_DIM), F32),
        ],
        input_output_aliases={7: 1, 8: 2},
        compiler_params=pltpu.CompilerParams(
            dimension_semantics=("arbitrary", "arbitrary"), vmem_limit_bytes=VMEM_LIMIT),
        name="sample_attn",
    )(*args)


def _silu(x):
    return x * jax.nn.sigmoid(x)


def _mix_kernel(x_ref, a_ref, ga_ref, uv_ref, ug_ref, gc_ref, qm_ref, gm_ref, cinit_ref, mk_ref, mv_ref,
                wdw_ref, bdw_ref, lng_ref, lnb_ref, wpw_ref, wout_ref, gpost_ref,
                xo_ref, cs_ref, uext_ref, mix_ref, *, nb, tile, conv_chunk):
    @pl.when(pl.program_id(1) == 0)
    def _():
        uext_ref[:, 0:HALO, :] = cinit_ref[...]

    first = HALO - (CONV_K - 1)
    chunks = []
    for b in range(nb):
        ub_ref = uext_ref.at[b]
        ub_ref[HALO:HALO + tile, :] = uv_ref[b] * jax.nn.sigmoid(ug_ref[b])
        for c0 in range(0, tile, conv_chunk):
            acc = jnp.broadcast_to(bdw_ref[...], (conv_chunk, CONV_CH))
            for phase in range(SUBLANES):
                rows = conv_chunk + (SUBLANES if phase else 0)
                part = None
                for k in range(CONV_K):
                    if (first + k) % SUBLANES == phase:
                        base = c0 + first + k - phase
                        term = ub_ref[base:base + rows, :] * wdw_ref[k:k + 1, :]
                        part = term if part is None else part + term
                if part is not None:
                    acc = acc + part[phase:phase + conv_chunk, :]
            chunks.append(acc)
        cs_ref[b] = ub_ref[tile + first:tile + HALO, :]
        tail = ub_ref[tile:tile + HALO, :]
        ub_ref[0:HALO, :] = tail
    c = jnp.concatenate(chunks, axis=0) if len(chunks) > 1 else chunks[0]

    mu = jnp.mean(c, axis=-1, keepdims=True)
    var = jnp.mean(jnp.square(c - mu), axis=-1, keepdims=True)
    c = (c - mu) * lax.rsqrt(var + EPS) * lng_ref[...] + lnb_ref[...]
    c = jnp.dot(_silu(c).astype(BF16), wpw_ref[...], preferred_element_type=F32)

    def mem_head(ref, h):
        if len(ref.shape) == 3:
            return ref[:, h, :].astype(BF16)
        return ref[:, h * HEAD_DIM:(h + 1) * HEAD_DIM].astype(BF16)

    for b in range(nb):
        rows = slice(b * tile, (b + 1) * tile)
        mix_ref[rows, 0:ATT_W] = (a_ref[b] * _silu(ga_ref[b])).astype(BF16)
        mix_ref[rows, ATT_W:ATT_W + CONV_CH] = (c[rows] * _silu(gc_ref[b])).astype(BF16)
        for h in range(N_X_HEADS):
            cols = slice(h * HEAD_DIM, (h + 1) * HEAD_DIM)
            q = (qm_ref[b, :, cols] * SCALE).astype(BF16)
            s = _qk(q, mem_head(mk_ref.at[b], h))
            p = jnp.exp(s - jnp.max(s, axis=-1, keepdims=True))
            o = jnp.dot(p.astype(BF16), mem_head(mv_ref.at[b], h), preferred_element_type=F32)
            o = o / jnp.sum(p, axis=-1, keepdims=True)
            off = ATT_W + CONV_CH + h * HEAD_DIM
            mix_ref[rows, off:off + HEAD_DIM] = (o * _silu(gm_ref[b, :, cols])).astype(BF16)

    y = jnp.dot(mix_ref[...], wout_ref[...], preferred_element_type=F32)
    y = y * lax.rsqrt(jnp.mean(y * y, axis=-1, keepdims=True) + EPS) * gpost_ref[...]
    for b in range(nb):
        xo_ref[b] = x_ref[b] + y[b * tile:(b + 1) * tile]


def _mix(x, z, a, cinit, mk_arr, mk_block, mv_arr, mv_block, w_dw, b_dw, ln_g, ln_b, w_pw2, w_out, g_post,
         *, li, nb, tile):
    b, s, _ = x.shape
    conv_chunk = min(tile, 32)
    assert b % nb == 0 and s % tile == 0

    def rows(width, cb=0):
        return pl.BlockSpec((nb, tile, width), lambda bi, t: (bi, t, cb * COL_BLK // width))

    def const(shape):
        return pl.BlockSpec(shape, lambda bi, t: (0,) * len(shape))

    def layer(shape):
        return pl.BlockSpec((None,) + shape, lambda bi, t: (li,) + (0,) * len(shape))

    def mem(block):
        shape, index = block
        return pl.BlockSpec(shape, lambda bi, t: index(bi))

    return pl.pallas_call(
        functools.partial(_mix_kernel, nb=nb, tile=tile, conv_chunk=conv_chunk),
        grid=(b // nb, s // tile),
        in_specs=[
            rows(D_MODEL), rows(ATT_W),
            rows(ATT_W, CB_GA), rows(COL_BLK, CB_UV), rows(COL_BLK, CB_UG), rows(COL_BLK, CB_GC),
            rows(COL_BLK, CB_QM), rows(COL_BLK, CB_GM),
            pl.BlockSpec((nb, HALO, CONV_CH), lambda bi, t: (bi, 0, 0)),
            mem(mk_block), mem(mv_block),
            const((CONV_K, CONV_CH)), const((1, CONV_CH)), const((1, CONV_CH)), const((1, CONV_CH)),
            layer((CONV_CH, CONV_CH)), layer((MIX_W, D_MODEL)), const((1, D_MODEL)),
        ],
        out_specs=[
            rows(D_MODEL),
            pl.BlockSpec((nb, CONV_K - 1, CONV_CH), lambda bi, t: (bi, 0, 0)),
        ],
        out_shape=[
            jax.ShapeDtypeStruct((b, s, D_MODEL), F32),
            jax.ShapeDtypeStruct((b, CONV_K - 1, CONV_CH), F32),
        ],
        scratch_shapes=[pltpu.VMEM((nb, HALO + tile, CONV_CH), F32), pltpu.VMEM((nb * tile, MIX_W), BF16)],
        compiler_params=pltpu.CompilerParams(
            dimension_semantics=("parallel", "arbitrary"), vmem_limit_bytes=VMEM_LIMIT),
        name="mix",
    )(x, a, z, z, z, z, z, z, cinit, mk_arr, mv_arr,
      w_dw, b_dw.reshape(1, -1), ln_g.reshape(1, -1), ln_b.reshape(1, -1), w_pw2, w_out,
      g_post.reshape(1, -1))


def kernel(x_prompt, x_sample, mem_prompt, cache_attn_k, cache_attn_v, state_conv, cache_mem_k, cache_mem_v,
           rel_bias, norm_pre_g, w_in, w_dw, b_dw, ln_conv_g, ln_conv_b, w_pw2, w_mem_kv, w_out, norm_post_g):
    bp, s_len, _ = x_prompt.shape
    bs, t_len, _ = x_sample.shape
    l_buf = cache_attn_k.shape[2]
    l_prompt = min(WIN, s_len)

    w_pw2_b = w_pw2.astype(BF16)
    w_mem_b, w_out_b = w_mem_kv.astype(BF16), w_out.astype(BF16)

    tab_near = _distance_table(rel_bias, np.arange(NEAR_REACH + 1), NEAR_PATTERNS)
    tab_far = _distance_table(rel_bias, FAR_D * np.arange(FAR_KEYS + 1), DIL_PATTERNS[-1:])
    bias_near = _toeplitz(tab_near * LOG2E, NEAR_REACH, Q_BLK, NEAR_KBLK * Q_BLK)
    bias_far = _toeplitz(tab_far * LOG2E, FAR_KEYS, Q_BLK, 2 * Q_BLK)
    tab_all = _distance_table(rel_bias, np.arange(WIN + 1), DIL_PATTERNS)
    bias_s = _head_matched_bias(tab_all, l_buf, t_len, l_buf)
    bias_s_new = _head_matched_bias(tab_all, 0, t_len, t_len)

    mkv = _mem_proj(mem_prompt.reshape(bp * N_MEM, D_MODEL), w_mem_b).reshape(DEPTH, bp, N_MEM, 2 * X_W)
    cinit_p = jnp.zeros((bp, HALO, CONV_CH), F32)
    pad_s = jnp.zeros((DEPTH, bs, HALO - (CONV_K - 1), CONV_CH), F32)
    cinit_s = jnp.concatenate([pad_s, state_conv], axis=2)
    xp, xs = x_prompt, x_sample
    stacks = None
    aks, avs = _shift_caches(cache_attn_k, cache_attn_v, shift=t_len)
    cvp, cvs = [], []
    for li in range(DEPTH):
        wargs = (w_dw[li], b_dw[li], ln_conv_g[li], ln_conv_b[li], w_pw2_b, w_out_b, norm_post_g[li])

        zs, w_in_li = _proj_cast(xs.reshape(bs * t_len, D_MODEL), norm_pre_g[li], w_in, li=li, bn=COL_BLK)
        zs = zs.reshape(bs, t_len, IN_W)

        zp, akp, avp = _proj_kv(xp.reshape(bp * s_len, D_MODEL), norm_pre_g[li], w_in_li, stacks,
                                li=li, seq=s_len, l_keep=l_prompt)
        stacks = (akp, avp)
        zp = zp.reshape(bp, s_len, IN_W)
        a = _prompt_attn(zp, bias_near, bias_far)
        mk_block, mv_block = (((None, MIX_NB, N_MEM, X_W), lambda bi, li=li, half=half: (li, bi, 0, half))
                              for half in (0, 1))
        xp, cst = _mix(xp, zp, a, cinit_p, mkv, mk_block, mkv, mv_block, *wargs,
                       li=li, nb=MIX_NB, tile=MIX_TILE)
        cvp.append(cst)

        a, aks, avs = _sample_attn(zs, cache_attn_k, cache_attn_v, bias_s, bias_s_new, aks, avs, li=li)
        mem_block = ((None, bs, N_MEM, N_X_HEADS, HEAD_DIM), lambda bi, li=li: (li, bi, 0, 0, 0))
        xs, cst = _mix(xs, zs, a, cinit_s[li], cache_mem_k, mem_block, cache_mem_v, mem_block, *wargs,
                       li=li, nb=bs, tile=t_len)
        cvs.append(cst)

    kv5 = (DEPTH, bp, l_prompt, N_ATT_HEADS, HEAD_DIM)
    mem5 = (DEPTH, bp, N_MEM, N_X_HEADS, HEAD_DIM)
    return (xp, xs, akp.reshape(kv5), avp.reshape(kv5), jnp.stack(cvp),
            mkv[..., :X_W].reshape(mem5), mkv[..., X_W:].reshape(mem5), aks, avs, jnp.stack(cvs))
```

```python
import functools

import numpy as np
import jax
import jax.numpy as jnp
from jax import lax
from jax.experimental import pallas as pl
from jax.experimental.pallas import tpu as pltpu
from jax.experimental.pallas import tpu_sc as plsc

D_MODEL = 2048
DEPTH = 4
N_MEM = 256
HEAD_DIM = 128
ATT_W = D_MODEL // 2
N_ATT_HEADS = ATT_W // HEAD_DIM
DIL_PATTERNS = ((128, 1), (512, 4), (2048, 16))
WIN = max(w for w, _ in DIL_PATTERNS)
N_BUCKETS = 32
MAX_DIST = WIN
CONV_CH = D_MODEL // 4
CONV_K = 31
X_W = D_MODEL // 4
N_X_HEADS = X_W // HEAD_DIM
MIX_W = ATT_W + CONV_CH + X_W
IN_W = 4 * ATT_W + 3 * CONV_CH + 2 * X_W
EPS = 1e-6
NEG = -1e30
SCALE = HEAD_DIM ** -0.5
LOG2E = 1.4426950408889634

COL_BLK = 512
CB_Q, CB_K, CB_V, CB_GA = 0, 2, 4, 6
CB_UV, CB_UG, CB_GC, CB_QM, CB_GM = 8, 9, 10, 11, 12
HEADS_PER_COL_BLK = COL_BLK // HEAD_DIM

Q_BLK = 128
FAR_W, FAR_D = DIL_PATTERNS[-1]
NEAR_PATTERNS = DIL_PATTERNS[:-1]
NEAR_REACH = max(w for w, _ in NEAR_PATTERNS)
NEAR_KBLK = NEAR_REACH // Q_BLK + 1
FAR_KEYS = FAR_W // FAR_D
FAR_STEP = 4
assert FAR_KEYS == Q_BLK and NEAR_REACH % Q_BLK == 0 and FAR_STEP * FAR_STEP == FAR_D
MIX_NB, MIX_TILE = 1, 512
STAGE_TILES = 16
PROJ_BM = 1024
SC_ACTIVE_SUBCORES = 8
SC_COPY_ROWS = 60
SAMPLE_RB = 1024
HALO = 32
SUBLANES = 8
LANES = 128
VMEM_LIMIT = 56 * 1024 * 1024

BF16 = jnp.bfloat16
F32 = jnp.float32


def _t5_bucket(dist):
    dist = np.asarray(dist)
    max_exact = N_BUCKETS // 2
    large = max_exact + (np.log(np.maximum(dist, 1) / max_exact)
                         / np.log(MAX_DIST / max_exact) * (N_BUCKETS - max_exact)).astype(np.int32)
    large = np.minimum(large, N_BUCKETS - 1)
    return np.where(dist < max_exact, dist, large).astype(np.int32)


def _distance_table(rel_bias, dists, patterns):
    dists = np.asarray(dists)
    cnt = np.zeros(dists.shape, np.int64)
    for w, dil in patterns:
        cnt += ((dists % dil == 0) & (dists <= w)).astype(np.int64)
    logc = jnp.log(jnp.asarray(np.maximum(cnt, 1), F32))
    tab = rel_bias[_t5_bucket(dists)].T.astype(F32) + logc[None, :]
    tab = jnp.where(jnp.asarray(cnt > 0)[None, :], tab, NEG)
    return jnp.concatenate([tab, jnp.full((tab.shape[0], 1), NEG, F32)], axis=1)


def _toeplitz(tab, off, rows, cols):
    n_valid = tab.shape[1] - 1
    lp = rows + cols
    j = np.concatenate([np.arange(cols), np.full(lp - (rows + cols - 1), off + n_valid),
                        np.arange(-(rows - 1), 0)])
    d = off - j
    idx = np.where((d >= 0) & (d < n_valid), d, n_valid)
    v = jnp.take(tab, jnp.asarray(idx, jnp.int32), axis=1)
    flat = jnp.tile(v, (1, rows))[:, :rows * (lp - 1)]
    return flat.reshape(tab.shape[0], rows, lp - 1)[:, :, :cols]


def _head_matched_bias(tab, off, t_len, n_rows):
    n_h, n_valid = tab.shape[0], tab.shape[1] - 1
    n_cols = n_rows * n_h
    gaps = jnp.full((n_h, n_valid, n_h - 1), NEG, F32)
    s = jnp.concatenate([tab[:, :n_valid, None], gaps], axis=2).reshape(n_h, n_valid * n_h)
    k_max = n_h * (off + t_len - 1) + n_h - 1
    pad_l = max(0, n_cols - 1 - n_h * off)
    pad_r = max(0, k_max - (n_valid * n_h - 1))
    s = jnp.concatenate([jnp.full((n_h, pad_l), NEG, F32), s, jnp.full((n_h, pad_r), NEG, F32)], axis=1)
    n_s = s.shape[1]
    lane_pad = -n_s % LANES
    rev = jnp.concatenate([s[:, ::-1], jnp.full((n_h, lane_pad), NEG, F32)], axis=1)
    starts = tuple((h, n_s - 1 - (n_h * (off + t) + h) - pad_l) for h in range(n_h) for t in range(t_len))
    return pl.pallas_call(
        functools.partial(_slice_rows_kernel, starts=starts, n_cols=n_cols),
        out_shape=jax.ShapeDtypeStruct((len(starts), n_cols), F32),
        name="bias_rows",
    )(rev)


def _slice_rows_kernel(src_ref, out_ref, *, starts, n_cols):
    for r, (h, start) in enumerate(starts):
        lo = start // LANES * LANES
        hi = -(-(start + n_cols) // LANES) * LANES
        window = src_ref[h:h + 1, lo:hi]
        out_ref[r:r + 1, :] = window[:, start - lo:start - lo + n_cols]


def _norm_rows(x_ref, g_ref, h_ref, normalize):
    x = x_ref[...]
    if normalize:
        x = x * lax.rsqrt(jnp.mean(x * x, axis=-1, keepdims=True) + EPS) * g_ref[...]
    h_ref[...] = x.astype(BF16)


def _proj_kernel(x_ref, g_ref, w_ref, o_ref, h_ref, *, normalize):
    @pl.when(pl.program_id(1) == 0)
    def _():
        _norm_rows(x_ref, g_ref, h_ref, normalize)

    o_ref[...] = jnp.dot(h_ref[...], w_ref[...], preferred_element_type=F32)


def _proj(x, g, w, *, li, bn):
    m, d = x.shape
    n = w.shape[2]
    return pl.pallas_call(
        functools.partial(_proj_kernel, normalize=True),
        grid=(1, n // bn),
        in_specs=[
            pl.BlockSpec((m, d), lambda i, j: (0, 0)),
            pl.BlockSpec((1, d), lambda i, j: (0, 0)),
            pl.BlockSpec((None, d, bn), lambda i, j: (li, 0, j)),
        ],
        out_specs=pl.BlockSpec((m, bn), lambda i, j: (0, j)),
        out_shape=jax.ShapeDtypeStruct((m, n), F32),
        scratch_shapes=[pltpu.VMEM((m, d), BF16)],
        compiler_params=pltpu.CompilerParams(
            dimension_semantics=("arbitrary", "arbitrary"), vmem_limit_bytes=VMEM_LIMIT),
        name="proj",
    )(x, g.reshape(1, d), w)


def _mem_proj(x, w):
    m, d = x.shape
    depth, _, n = w.shape
    bn = COL_BLK
    return pl.pallas_call(
        functools.partial(_proj_kernel, normalize=False),
        grid=(depth, n // bn),
        in_specs=[
            pl.BlockSpec((m, d), lambda li, j: (0, 0)),
            pl.BlockSpec((1, d), lambda li, j: (0, 0)),
            pl.BlockSpec((None, d, bn), lambda li, j: (li, 0, j)),
        ],
        out_specs=pl.BlockSpec((None, m, bn), lambda li, j: (li, 0, j)),
        out_shape=jax.ShapeDtypeStruct((depth, m, n), F32),
        scratch_shapes=[pltpu.VMEM((m, d), BF16)],
        compiler_params=pltpu.CompilerParams(
            dimension_semantics=("arbitrary", "arbitrary"), vmem_limit_bytes=VMEM_LIMIT),
        name="mem_proj",
    )(x, jnp.ones((1, d), F32), w)


def _proj_kv_kernel(*refs, tiles_per_seq, tail_tiles, aliased):
    x_ref, g_ref, w_ref = refs[:3]
    o_ref, ko_ref, vo_ref, h_ref = refs[5:] if aliased else refs[3:]
    i, j = pl.program_id(0), pl.program_id(1)

    @pl.when(j == 0)
    def _():
        _norm_rows(x_ref, g_ref, h_ref, True)

    o_ref[...] = jnp.dot(h_ref[...], w_ref[...], preferred_element_type=F32)

    in_tail = i % tiles_per_seq >= tiles_per_seq - tail_tiles
    for cb0, dst in ((CB_K, ko_ref), (CB_V, vo_ref)):
        for cb in range(cb0, cb0 + ATT_W // COL_BLK):
            @pl.when(in_tail & (j == cb))
            def _(cb=cb, cb0=cb0, dst=dst):
                for hh in range(HEADS_PER_COL_BLK):
                    head = (cb - cb0) * HEADS_PER_COL_BLK + hh
                    dst[pl.ds(head, o_ref.shape[0], stride=N_ATT_HEADS), :] = (
                        o_ref[:, hh * HEAD_DIM:(hh + 1) * HEAD_DIM])


def _shift_caches(cache_k, cache_v, *, shift):
    depth, n_seq, l_buf = cache_k.shape[:3]
    info = plsc.get_sparse_core_info()
    n_workers = info.num_cores * SC_ACTIVE_SUBCORES
    n_jobs = depth * n_seq
    assert n_jobs % n_workers == 0 and (l_buf - shift) % SC_COPY_ROWS == 0
    assert SC_ACTIVE_SUBCORES <= info.num_subcores
    n_chunks = (l_buf - shift) // SC_COPY_ROWS
    mesh = plsc.VectorSubcoreMesh(core_axis_name="core", subcore_axis_name="subcore")
    out = jax.ShapeDtypeStruct(cache_k.shape, cache_k.dtype)

    @pl.kernel(out_type=(out, out), mesh=mesh,
               scratch_types=[pltpu.VMEM((SC_COPY_ROWS,) + cache_k.shape[3:], cache_k.dtype)],
               name="shift_caches")
    def shift_kernel(ck_hbm, cv_hbm, ok_hbm, ov_hbm, buf):
        subcore = lax.axis_index("subcore")
        worker = lax.axis_index("core") * SC_ACTIVE_SUBCORES + subcore

        @pl.when(subcore < SC_ACTIVE_SUBCORES)
        def _():
            for job in range(n_jobs // n_workers):
                seq_id = worker * (n_jobs // n_workers) + job
                li, b = seq_id // n_seq, seq_id % n_seq
                for src, dst in ((ck_hbm, ok_hbm), (cv_hbm, ov_hbm)):
                    @pl.loop(0, n_chunks)
                    def _(c, src=src, dst=dst, li=li, b=b):
                        r0 = c * SC_COPY_ROWS
                        pltpu.sync_copy(src.at[li, b, pl.ds(r0 + shift, SC_COPY_ROWS)], buf)
                        pltpu.sync_copy(buf, dst.at[li, b, pl.ds(r0, SC_COPY_ROWS)])

    return shift_kernel(cache_k, cache_v)


def _proj_kv(x, g, w, prev, *, li, seq, l_keep):
    m, d = x.shape
    n = w.shape[2]
    bsz = m // seq
    bm, bn = PROJ_BM, COL_BLK
    tiles_per_seq, tail_tiles = seq // bm, l_keep // bm
    assert seq % bm == 0 and l_keep % bm == 0
    aliased = prev is not None
    kv_shape = jax.ShapeDtypeStruct((DEPTH, bsz, l_keep * N_ATT_HEADS, HEAD_DIM), F32)
    any_spec = pl.BlockSpec(memory_space=pl.ANY)

    def kv_map(i, j):
        return (li, i // tiles_per_seq, jnp.maximum(i % tiles_per_seq - (tiles_per_seq - tail_tiles), 0), 0)

    kv_spec = pl.BlockSpec((None, None, bm * N_ATT_HEADS, HEAD_DIM), kv_map)
    in_specs = [
        pl.BlockSpec((bm, d), lambda i, j: (i, 0)),
        pl.BlockSpec((1, d), lambda i, j: (0, 0)),
        pl.BlockSpec((None, d, bn), lambda i, j: (li, 0, j)),
    ]
    args = [x, g.reshape(1, d), w]
    if aliased:
        in_specs += [any_spec] * 2
        args += list(prev)
    return pl.pallas_call(
        functools.partial(_proj_kv_kernel, tiles_per_seq=tiles_per_seq, tail_tiles=tail_tiles,
                          aliased=aliased),
        grid=(m // bm, n // bn),
        in_specs=in_specs,
        out_specs=[pl.BlockSpec((bm, bn), lambda i, j: (i, j)), kv_spec, kv_spec],
        out_shape=[jax.ShapeDtypeStruct((m, n), F32), kv_shape, kv_shape],
        scratch_shapes=[pltpu.VMEM((bm, d), BF16)],
        input_output_aliases={3: 1, 4: 2} if aliased else {},
        compiler_params=pltpu.CompilerParams(
            dimension_semantics=("arbitrary", "arbitrary"), vmem_limit_bytes=VMEM_LIMIT),
        name="proj_kv",
    )(*args)


def _qk(q, k):
    return lax.dot_general(q, k, (((1,), (1,)), ((), ())), preferred_element_type=F32)


def _prompt_attn_kernel(q_ref, k_ref, v_ref, bn_ref, bf_ref, o_ref,
                        qb_ref, kb_ref, vb_ref, qf_ref, kf_ref, vf_ref, of_ref, lf_ref, tmp_ref, tmp2_ref,
                        *, seq):
    sub = seq // FAR_D
    chunk = 2 * Q_BLK
    val = slice(0, HEAD_DIM)

    for r0 in range(0, seq, chunk):
        rows = pl.ds(r0, chunk)
        qb_ref[rows, :] = (q_ref[rows, :] * (SCALE * LOG2E)).astype(BF16)
        kb_ref[rows, :] = k_ref[rows, :].astype(BF16)
        vb_ref[rows, val] = v_ref[rows, :].astype(BF16)
        vb_ref[rows, HEAD_DIM:] = jnp.ones((chunk, HEAD_DIM), BF16)
        vf_ref[rows, HEAD_DIM:] = jnp.ones((chunk, HEAD_DIM), BF16)

    quarter = seq // FAR_STEP
    for src_ref, dst_ref, scale in ((q_ref, qf_ref, SCALE * LOG2E), (k_ref, kf_ref, None),
                                    (v_ref, vf_ref, None)):
        for r1 in range(FAR_STEP):
            tmp_ref[r1 * quarter:(r1 + 1) * quarter, :] = src_ref[pl.ds(r1, quarter, stride=FAR_STEP), :]
        for r1 in range(FAR_STEP):
            for r2 in range(FAR_STEP):
                r = FAR_STEP * r2 + r1
                x = tmp_ref[pl.ds(r1 * quarter + r2, sub, stride=FAR_STEP), :]
                if scale is not None:
                    x = x * scale
                dst_ref[r * sub:(r + 1) * sub, val] = x.astype(BF16)

    far_geo = []
    for r in range(FAR_D):
        for a_blk in range(sub // Q_BLK):
            n_kt = min(a_blk, 1) + 1
            rows = pl.ds(r * sub + a_blk * Q_BLK, Q_BLK)
            keys = pl.ds(r * sub + (a_blk + 1 - n_kt) * Q_BLK, n_kt * Q_BLK)
            r1, r2 = r % FAR_STEP, r // FAR_STEP
            dst = pl.ds(r1 * quarter + FAR_STEP * a_blk * Q_BLK + r2, Q_BLK, stride=FAR_STEP)
            far_geo.append((rows, keys, n_kt, dst))
    near_geo = []
    for i in range(seq // Q_BLK):
        n_kt = min(i + 1, NEAR_KBLK)
        near_geo.append((pl.ds(i * Q_BLK, Q_BLK), pl.ds((i + 1 - n_kt) * Q_BLK, n_kt * Q_BLK), n_kt))

    for g0 in range(0, len(far_geo), STAGE_TILES):
        geo = far_geo[g0:g0 + STAGE_TILES]
        scores = [_qk(qf_ref[rows, :], kf_ref[keys, :]) + bf_ref[:, (2 - n_kt) * Q_BLK:]
                  for rows, keys, n_kt, _ in geo]
        probs = []
        for s in scores:
            m = jnp.max(s, axis=-1, keepdims=True)
            probs.append((jnp.exp2(s - m).astype(BF16), m))
        for (_, keys, _, dst), (p, m) in zip(geo, probs):
            o = jnp.dot(p, vf_ref[keys, :], preferred_element_type=F32)
            l = o[:, HEAD_DIM:]
            tmp_ref[dst, :] = o[:, val] / l
            tmp2_ref[dst, :] = m + jnp.log2(l)
    for r1 in range(FAR_STEP):
        of_ref[pl.ds(r1, quarter, stride=FAR_STEP), :] = tmp_ref[r1 * quarter:(r1 + 1) * quarter, :]
        lf_ref[pl.ds(r1, quarter, stride=FAR_STEP), :] = tmp2_ref[r1 * quarter:(r1 + 1) * quarter, :]

    for g0 in range(0, len(near_geo), STAGE_TILES):
        geo = near_geo[g0:g0 + STAGE_TILES]
        scores = [_qk(qb_ref[rows, :], kb_ref[keys, :]) + bn_ref[:, (NEAR_KBLK - n_kt) * Q_BLK:]
                  for rows, keys, n_kt in geo]
        probs = []
        for (rows, _, _), s in zip(geo, scores):
            m_far = lf_ref[rows, 0:1]
            m = jnp.maximum(jnp.max(s, axis=-1, keepdims=True), m_far)
            probs.append((jnp.exp2(s - m).astype(BF16), jnp.exp2(m_far - m)))
        for (rows, keys, _), (p, w_far) in zip(geo, probs):
            o = jnp.dot(p, vb_ref[keys, :], preferred_element_type=F32)
            acc = w_far * of_ref[rows, :] + o[:, val]
            o_ref[rows, :] = acc / (w_far + o[:, HEAD_DIM:])


def _prompt_attn(z, bias_near, bias_far):
    b, s, _ = z.shape
    assert s % (FAR_D * Q_BLK) == 0

    def head_cols(cb):
        return pl.BlockSpec((None, s, HEAD_DIM), lambda bi, h: (bi, 0, cb * HEADS_PER_COL_BLK + h))

    return pl.pallas_call(
        functools.partial(_prompt_attn_kernel, seq=s),
        grid=(b, N_ATT_HEADS),
        in_specs=[
            head_cols(CB_Q), head_cols(CB_K), head_cols(CB_V),
            pl.BlockSpec((None, Q_BLK, NEAR_KBLK * Q_BLK), lambda bi, h: (h, 0, 0)),
            pl.BlockSpec((None, Q_BLK, 2 * Q_BLK), lambda bi, h: (h, 0, 0)),
        ],
        out_specs=pl.BlockSpec((None, s, HEAD_DIM), lambda bi, h: (bi, 0, h)),
        out_shape=jax.ShapeDtypeStruct((b, s, ATT_W), F32),
        scratch_shapes=[pltpu.VMEM((s, HEAD_DIM), BF16), pltpu.VMEM((s, HEAD_DIM), BF16),
                        pltpu.VMEM((s, 2 * HEAD_DIM), BF16)] * 2 + [pltpu.VMEM((s, HEAD_DIM), F32)] * 4,
        compiler_params=pltpu.CompilerParams(
            dimension_semantics=("parallel", "parallel"), vmem_limit_bytes=VMEM_LIMIT),
        name="prompt_attn",
    )(z, z, z, bias_near, bias_far)


def _sample_attn_kernel(q_ref, kn_ref, vn_ref, kc_ref, vc_ref, bias_ref, biasn_ref, sk_hbm, sv_hbm,
                        o_ref, ko_ref, vo_ref, qa_ref, kn3_ref, vn3_ref, m_ref, l_ref, acc_ref, *, t_len, rb):
    del sk_hbm, sv_hbm
    j = pl.program_id(1)
    last = pl.num_programs(1) - 1
    n_heads = N_ATT_HEADS

    @pl.when(j == 0)
    def _():
        for h in range(n_heads):
            cols = slice(h * HEAD_DIM, (h + 1) * HEAD_DIM)
            qa_ref[h * t_len:(h + 1) * t_len, :] = q_ref[:, cols] * SCALE
            kn3_ref[:, h, :] = kn_ref[:, cols]
            vn3_ref[:, h, :] = vn_ref[:, cols]
        m_ref[...] = jnp.full(m_ref.shape, NEG, F32)
        l_ref[...] = jnp.zeros(l_ref.shape, F32)
        acc_ref[...] = jnp.zeros(acc_ref.shape, F32)

    qa = qa_ref[...].astype(BF16)

    def accumulate(k, v, bias):
        s = _qk(qa, k.astype(BF16)) + bias
        m_old = m_ref[...]
        m_new = jnp.maximum(m_old, jnp.max(s, axis=-1, keepdims=True))
        alpha = jnp.exp(m_old - m_new)
        p = jnp.exp(s - m_new)
        l_ref[...] = alpha * l_ref[...] + jnp.sum(p, axis=-1, keepdims=True)
        acc_ref[...] = alpha * acc_ref[...] + jnp.dot(p.astype(BF16), v.astype(BF16),
                                                      preferred_element_type=F32)
        m_ref[...] = m_new

    accumulate(kc_ref[...].reshape(rb * n_heads, HEAD_DIM), vc_ref[...].reshape(rb * n_heads, HEAD_DIM),
               bias_ref[...])

    @pl.when(j == last)
    def _():
        ko_ref[...] = kn3_ref[...]
        vo_ref[...] = vn3_ref[...]
        accumulate(kn3_ref[...].reshape(t_len * n_heads, HEAD_DIM),
                   vn3_ref[...].reshape(t_len * n_heads, HEAD_DIM), biasn_ref[...])
        o = acc_ref[...] / l_ref[...]
        for h in range(n_heads):
            o_ref[:, h * HEAD_DIM:(h + 1) * HEAD_DIM] = o[h * t_len:(h + 1) * t_len, :]


def _sample_attn(z, cache_k, cache_v, bias, bias_new, stack_k, stack_v, *, li):
    b, t_len, _ = z.shape
    l_buf = cache_k.shape[2]
    rb = SAMPLE_RB
    assert t_len == SUBLANES and l_buf % rb == 0
    rows = N_ATT_HEADS * t_len

    def zspec(cb):
        return pl.BlockSpec((None, t_len, ATT_W), lambda bi, j: (bi, 0, cb * COL_BLK // ATT_W))

    blk_spec = pl.BlockSpec((None, None, rb, N_ATT_HEADS, HEAD_DIM), lambda bi, j: (li, bi, j, 0, 0))
    new_spec = pl.BlockSpec((None, None, t_len, N_ATT_HEADS, HEAD_DIM),
                            lambda bi, j: (li, bi, l_buf // t_len - 1, 0, 0))
    any_spec = pl.BlockSpec(memory_space=pl.ANY)
    in_specs = [
        zspec(CB_Q), zspec(CB_K), zspec(CB_V), blk_spec, blk_spec,
        pl.BlockSpec((rows, rb * N_ATT_HEADS), lambda bi, j: (0, j)),
        pl.BlockSpec((rows, rows), lambda bi, j: (0, 0)),
        any_spec, any_spec,
    ]
    args = [z, z, z, cache_k, cache_v, bias, bias_new, stack_k, stack_v]
    return pl.pallas_call(
        functools.partial(_sample_attn_kernel, t_len=t_len, rb=rb),
        grid=(b, l_buf // rb),
        in_specs=in_specs,
        out_specs=[pl.BlockSpec((None, t_len, ATT_W), lambda bi, j: (bi, 0, 0)), new_spec, new_spec],
        out_shape=[
            jax.ShapeDtypeStruct((b, t_len, ATT_W), F32),
            jax.ShapeDtypeStruct(cache_k.shape, F32),
            jax.ShapeDtypeStruct(cache_v.shape, F32),
        ],
        scratch_shapes=[
            pltpu.VMEM((rows, HEAD_DIM), F32),
            pltpu.VMEM((t_len, N_ATT_HEADS, HEAD_DIM), F32), pltpu.VMEM((t_len, N_ATT_HEADS, HEAD_DIM), F32),
            pltpu.VMEM((rows, 1), F32), pltpu.VMEM((rows, 1), F32), pltpu.VMEM((rows, HEAD_DIM), F32),
        ],
        input_output_aliases={7: 1, 8: 2},
        compiler_params=pltpu.CompilerParams(
            dimension_semantics=("arbitrary", "arbitrary"), vmem_limit_bytes=VMEM_LIMIT),
        name="sample_attn",
    )(*args)


def _silu(x):
    return x * jax.nn.sigmoid(x)


def _mix_kernel(x_ref, a_ref, ga_ref, uv_ref, ug_ref, gc_ref, qm_ref, gm_ref, cinit_ref, mk_ref, mv_ref,
                wdw_ref, bdw_ref, lng_ref, lnb_ref, wpw_ref, wout_ref, gpost_ref,
                xo_ref, cs_ref, uext_ref, mix_ref, *, nb, tile, conv_chunk):
    @pl.when(pl.program_id(1) == 0)
    def _():
        uext_ref[:, 0:HALO, :] = cinit_ref[...]

    first = HALO - (CONV_K - 1)
    chunks = []
    for b in range(nb):
        ub_ref = uext_ref.at[b]
        ub_ref[HALO:HALO + tile, :] = uv_ref[b] * jax.nn.sigmoid(ug_ref[b])
        for c0 in range(0, tile, conv_chunk):
            acc = jnp.broadcast_to(bdw_ref[...], (conv_chunk, CONV_CH))
            for phase in range(SUBLANES):
                rows = conv_chunk + (SUBLANES if phase else 0)
                part = None
                for k in range(CONV_K):
                    if (first + k) % SUBLANES == phase:
                        base = c0 + first + k - phase
                        term = ub_ref[base:base + rows, :] * wdw_ref[k:k + 1, :]
                        part = term if part is None else part + term
                if part is not None:
                    acc = acc + part[phase:phase + conv_chunk, :]
            chunks.append(acc)
        cs_ref[b] = ub_ref[tile + first:tile + HALO, :]
        tail = ub_ref[tile:tile + HALO, :]
        ub_ref[0:HALO, :] = tail
    c = jnp.concatenate(chunks, axis=0) if len(chunks) > 1 else chunks[0]

    mu = jnp.mean(c, axis=-1, keepdims=True)
    var = jnp.mean(jnp.square(c - mu), axis=-1, keepdims=True)
    c = (c - mu) * lax.rsqrt(var + EPS) * lng_ref[...] + lnb_ref[...]
    c = jnp.dot(_silu(c).astype(BF16), wpw_ref[...], preferred_element_type=F32)

    def mem_head(ref, h):
        if len(ref.shape) == 3:
            return ref[:, h, :].astype(BF16)
        return ref[:, h * HEAD_DIM:(h + 1) * HEAD_DIM].astype(BF16)

    for b in range(nb):
        rows = slice(b * tile, (b + 1) * tile)
        mix_ref[rows, 0:ATT_W] = (a_ref[b] * _silu(ga_ref[b])).astype(BF16)
        mix_ref[rows, ATT_W:ATT_W + CONV_CH] = (c[rows] * _silu(gc_ref[b])).astype(BF16)
        for h in range(N_X_HEADS):
            cols = slice(h * HEAD_DIM, (h + 1) * HEAD_DIM)
            q = (qm_ref[b, :, cols] * SCALE).astype(BF16)
            s = _qk(q, mem_head(mk_ref.at[b], h))
            p = jnp.exp(s - jnp.max(s, axis=-1, keepdims=True))
            o = jnp.dot(p.astype(BF16), mem_head(mv_ref.at[b], h), preferred_element_type=F32)
            o = o / jnp.sum(p, axis=-1, keepdims=True)
            off = ATT_W + CONV_CH + h * HEAD_DIM
            mix_ref[rows, off:off + HEAD_DIM] = (o * _silu(gm_ref[b, :, cols])).astype(BF16)

    y = jnp.dot(mix_ref[...], wout_ref[...], preferred_element_type=F32)
    y = y * lax.rsqrt(jnp.mean(y * y, axis=-1, keepdims=True) + EPS) * gpost_ref[...]
    for b in range(nb):
        xo_ref[b] = x_ref[b] + y[b * tile:(b + 1) * tile]


def _mix(x, z, a, cinit, mk_arr, mk_block, mv_arr, mv_block, w_dw, b_dw, ln_g, ln_b, w_pw2, w_out, g_post,
         *, li, nb, tile):
    b, s, _ = x.shape
    conv_chunk = min(tile, 32)
    assert b % nb == 0 and s % tile == 0

    def rows(width, cb=0):
        return pl.BlockSpec((nb, tile, width), lambda bi, t: (bi, t, cb * COL_BLK // width))

    def const(shape):
        return pl.BlockSpec(shape, lambda bi, t: (0,) * len(shape))

    def layer(shape):
        return pl.BlockSpec((None,) + shape, lambda bi, t: (li,) + (0,) * len(shape))

    def mem(block):
        shape, index = block
        return pl.BlockSpec(shape, lambda bi, t: index(bi))

    return pl.pallas_call(
        functools.partial(_mix_kernel, nb=nb, tile=tile, conv_chunk=conv_chunk),
        grid=(b // nb, s // tile),
        in_specs=[
            rows(D_MODEL), rows(ATT_W),
            rows(ATT_W, CB_GA), rows(COL_BLK, CB_UV), rows(COL_BLK, CB_UG), rows(COL_BLK, CB_GC),
            rows(COL_BLK, CB_QM), rows(COL_BLK, CB_GM),
            pl.BlockSpec((nb, HALO, CONV_CH), lambda bi, t: (bi, 0, 0)),
            mem(mk_block), mem(mv_block),
            const((CONV_K, CONV_CH)), const((1, CONV_CH)), const((1, CONV_CH)), const((1, CONV_CH)),
            layer((CONV_CH, CONV_CH)), layer((MIX_W, D_MODEL)), const((1, D_MODEL)),
        ],
        out_specs=[
            rows(D_MODEL),
            pl.BlockSpec((nb, CONV_K - 1, CONV_CH), lambda bi, t: (bi, 0, 0)),
        ],
        out_shape=[
            jax.ShapeDtypeStruct((b, s, D_MODEL), F32),
            jax.ShapeDtypeStruct((b, CONV_K - 1, CONV_CH), F32),
        ],
        scratch_shapes=[pltpu.VMEM((nb, HALO + tile, CONV_CH), F32), pltpu.VMEM((nb * tile, MIX_W), BF16)],
        compiler_params=pltpu.CompilerParams(
            dimension_semantics=("parallel", "arbitrary"), vmem_limit_bytes=VMEM_LIMIT),
        name="mix",
    )(x, a, z, z, z, z, z, z, cinit, mk_arr, mv_arr,
      w_dw, b_dw.reshape(1, -1), ln_g.reshape(1, -1), ln_b.reshape(1, -1), w_pw2, w_out,
      g_post.reshape(1, -1))


def kernel(x_prompt, x_sample, mem_prompt, cache_attn_k, cache_attn_v, state_conv, cache_mem_k, cache_mem_v,
           rel_bias, norm_pre_g, w_in, w_dw, b_dw, ln_conv_g, ln_conv_b, w_pw2, w_mem_kv, w_out, norm_post_g):
    bp, s_len, _ = x_prompt.shape
    bs, t_len, _ = x_sample.shape
    l_buf = cache_attn_k.shape[2]
    l_prompt = min(WIN, s_len)

    w_in_b, w_pw2_b = w_in.astype(BF16), w_pw2.astype(BF16)
    w_mem_b, w_out_b = w_mem_kv.astype(BF16), w_out.astype(BF16)

    tab_near = _distance_table(rel_bias, np.arange(NEAR_REACH + 1), NEAR_PATTERNS)
    tab_far = _distance_table(rel_bias, FAR_D * np.arange(FAR_KEYS + 1), DIL_PATTERNS[-1:])
    bias_near = _toeplitz(tab_near * LOG2E, NEAR_REACH, Q_BLK, NEAR_KBLK * Q_BLK)
    bias_far = _toeplitz(tab_far * LOG2E, FAR_KEYS, Q_BLK, 2 * Q_BLK)
    tab_all = _distance_table(rel_bias, np.arange(WIN + 1), DIL_PATTERNS)
    bias_s = _head_matched_bias(tab_all, l_buf, t_len, l_buf)
    bias_s_new = _head_matched_bias(tab_all, 0, t_len, t_len)

    mkv = _mem_proj(mem_prompt.reshape(bp * N_MEM, D_MODEL), w_mem_b).reshape(DEPTH, bp, N_MEM, 2 * X_W)
    cinit_p = jnp.zeros((bp, HALO, CONV_CH), F32)
    pad_s = jnp.zeros((DEPTH, bs, HALO - (CONV_K - 1), CONV_CH), F32)
    cinit_s = jnp.concatenate([pad_s, state_conv], axis=2)
    xp, xs = x_prompt, x_sample
    stacks = None
    aks, avs = _shift_caches(cache_attn_k, cache_attn_v, shift=t_len)
    cvp, cvs = [], []
    layer_w = [(w_dw[li], b_dw[li], ln_conv_g[li], ln_conv_b[li], w_pw2_b, w_out_b, norm_post_g[li])
               for li in range(DEPTH)]
    for li in range(DEPTH):
        zp, akp, avp = _proj_kv(xp.reshape(bp * s_len, D_MODEL), norm_pre_g[li], w_in_b, stacks,
                                li=li, seq=s_len, l_keep=l_prompt)
        stacks = (akp, avp)
        zp = zp.reshape(bp, s_len, IN_W)
        a = _prompt_attn(zp, bias_near, bias_far)
        mk_block, mv_block = (((None, MIX_NB, N_MEM, X_W), lambda bi, li=li, half=half: (li, bi, 0, half))
                              for half in (0, 1))
        xp, cst = _mix(xp, zp, a, cinit_p, mkv, mk_block, mkv, mv_block, *layer_w[li],
                       li=li, nb=MIX_NB, tile=MIX_TILE)
        cvp.append(cst)

    for li in range(DEPTH):
        wargs = layer_w[li]
        zs = _proj(xs.reshape(bs * t_len, D_MODEL), norm_pre_g[li], w_in_b, li=li, bn=COL_BLK)
        zs = zs.reshape(bs, t_len, IN_W)
        a, aks, avs = _sample_attn(zs, cache_attn_k, cache_attn_v, bias_s, bias_s_new, aks, avs, li=li)
        mem_block = ((None, bs, N_MEM, N_X_HEADS, HEAD_DIM), lambda bi, li=li: (li, bi, 0, 0, 0))
        xs, cst = _mix(xs, zs, a, cinit_s[li], cache_mem_k, mem_block, cache_mem_v, mem_block, *wargs,
                       li=li, nb=bs, tile=t_len)
        cvs.append(cst)

    kv5 = (DEPTH, bp, l_prompt, N_ATT_HEADS, HEAD_DIM)
    mem5 = (DEPTH, bp, N_MEM, N_X_HEADS, HEAD_DIM)
    return (xp, xs, akp.reshape(kv5), avp.reshape(kv5), jnp.stack(cvp),
            mkv[..., :X_W].reshape(mem5), mkv[..., X_W:].reshape(mem5), aks, avs, jnp.stack(cvs))
```

```python
import functools

import numpy as np
import jax
import jax.numpy as jnp
from jax import lax
from jax.experimental import pallas as pl
from jax.experimental.pallas import tpu as pltpu
from jax.experimental.pallas import tpu_sc as plsc

D_MODEL = 2048
DEPTH = 4
N_MEM = 256
HEAD_DIM = 128
ATT_W = D_MODEL // 2
N_ATT_HEADS = ATT_W // HEAD_DIM
DIL_PATTERNS = ((128, 1), (512, 4), (2048, 16))
WIN = max(w for w, _ in DIL_PATTERNS)
N_BUCKETS = 32
MAX_DIST = WIN
CONV_CH = D_MODEL // 4
CONV_K = 31
X_W = D_MODEL // 4
N_X_HEADS = X_W // HEAD_DIM
MIX_W = ATT_W + CONV_CH + X_W
IN_W = 4 * ATT_W + 3 * CONV_CH + 2 * X_W
EPS = 1e-6
NEG = -1e30
SCALE = HEAD_DIM ** -0.5
LOG2E = 1.4426950408889634

COL_BLK = 512
CB_Q, CB_K, CB_V, CB_GA = 0, 2, 4, 6
CB_UV, CB_UG, CB_GC, CB_QM, CB_GM = 8, 9, 10, 11, 12
HEADS_PER_COL_BLK = COL_BLK // HEAD_DIM

Q_BLK = 128
FAR_W, FAR_D = DIL_PATTERNS[-1]
NEAR_PATTERNS = DIL_PATTERNS[:-1]
NEAR_REACH = max(w for w, _ in NEAR_PATTERNS)
NEAR_KBLK = NEAR_REACH // Q_BLK + 1
FAR_KEYS = FAR_W // FAR_D
FAR_STEP = 4
assert FAR_KEYS == Q_BLK and NEAR_REACH % Q_BLK == 0 and FAR_STEP * FAR_STEP == FAR_D
MIX_NB, MIX_TILE = 1, 512
STAGE_TILES = 16
PROJ_BM = 1024
SC_ACTIVE_SUBCORES = 4
SC_COPY_ROWS = 60
SAMPLE_RB = 1024
HALO = 32
SUBLANES = 8
LANES = 128
VMEM_LIMIT = 56 * 1024 * 1024

BF16 = jnp.bfloat16
F32 = jnp.float32


def _t5_bucket(dist):
    dist = np.asarray(dist)
    max_exact = N_BUCKETS // 2
    large = max_exact + (np.log(np.maximum(dist, 1) / max_exact)
                         / np.log(MAX_DIST / max_exact) * (N_BUCKETS - max_exact)).astype(np.int32)
    large = np.minimum(large, N_BUCKETS - 1)
    return np.where(dist < max_exact, dist, large).astype(np.int32)


def _distance_table(rel_bias, dists, patterns):
    dists = np.asarray(dists)
    cnt = np.zeros(dists.shape, np.int64)
    for w, dil in patterns:
        cnt += ((dists % dil == 0) & (dists <= w)).astype(np.int64)
    logc = jnp.log(jnp.asarray(np.maximum(cnt, 1), F32))
    tab = rel_bias[_t5_bucket(dists)].T.astype(F32) + logc[None, :]
    tab = jnp.where(jnp.asarray(cnt > 0)[None, :], tab, NEG)
    return jnp.concatenate([tab, jnp.full((tab.shape[0], 1), NEG, F32)], axis=1)


def _toeplitz(tab, off, rows, cols):
    n_valid = tab.shape[1] - 1
    lp = rows + cols
    j = np.concatenate([np.arange(cols), np.full(lp - (rows + cols - 1), off + n_valid),
                        np.arange(-(rows - 1), 0)])
    d = off - j
    idx = np.where((d >= 0) & (d < n_valid), d, n_valid)
    v = jnp.take(tab, jnp.asarray(idx, jnp.int32), axis=1)
    flat = jnp.tile(v, (1, rows))[:, :rows * (lp - 1)]
    return flat.reshape(tab.shape[0], rows, lp - 1)[:, :, :cols]


def _head_matched_bias(tab, off, t_len, n_rows):
    n_h, n_valid = tab.shape[0], tab.shape[1] - 1
    n_cols = n_rows * n_h
    gaps = jnp.full((n_h, n_valid, n_h - 1), NEG, F32)
    s = jnp.concatenate([tab[:, :n_valid, None], gaps], axis=2).reshape(n_h, n_valid * n_h)
    k_max = n_h * (off + t_len - 1) + n_h - 1
    pad_l = max(0, n_cols - 1 - n_h * off)
    pad_r = max(0, k_max - (n_valid * n_h - 1))
    s = jnp.concatenate([jnp.full((n_h, pad_l), NEG, F32), s, jnp.full((n_h, pad_r), NEG, F32)], axis=1)
    n_s = s.shape[1]
    lane_pad = -n_s % LANES
    rev = jnp.concatenate([s[:, ::-1], jnp.full((n_h, lane_pad), NEG, F32)], axis=1)
    starts = tuple((h, n_s - 1 - (n_h * (off + t) + h) - pad_l) for h in range(n_h) for t in range(t_len))
    return pl.pallas_call(
        functools.partial(_slice_rows_kernel, starts=starts, n_cols=n_cols),
        out_shape=jax.ShapeDtypeStruct((len(starts), n_cols), F32),
        name="bias_rows",
    )(rev)


def _slice_rows_kernel(src_ref, out_ref, *, starts, n_cols):
    for r, (h, start) in enumerate(starts):
        lo = start // LANES * LANES
        hi = -(-(start + n_cols) // LANES) * LANES
        window = src_ref[h:h + 1, lo:hi]
        out_ref[r:r + 1, :] = window[:, start - lo:start - lo + n_cols]


def _norm_rows(x_ref, g_ref, h_ref, normalize):
    x = x_ref[...]
    if normalize:
        x = x * lax.rsqrt(jnp.mean(x * x, axis=-1, keepdims=True) + EPS) * g_ref[...]
    h_ref[...] = x.astype(BF16)


def _proj_kernel(x_ref, g_ref, w_ref, o_ref, h_ref, *, normalize):
    @pl.when(pl.program_id(1) == 0)
    def _():
        _norm_rows(x_ref, g_ref, h_ref, normalize)

    o_ref[...] = jnp.dot(h_ref[...], w_ref[...], preferred_element_type=F32)


def _proj(x, g, w, *, li, bn):
    m, d = x.shape
    n = w.shape[2]
    return pl.pallas_call(
        functools.partial(_proj_kernel, normalize=True),
        grid=(1, n // bn),
        in_specs=[
            pl.BlockSpec((m, d), lambda i, j: (0, 0)),
            pl.BlockSpec((1, d), lambda i, j: (0, 0)),
            pl.BlockSpec((None, d, bn), lambda i, j: (li, 0, j)),
        ],
        out_specs=pl.BlockSpec((m, bn), lambda i, j: (0, j)),
        out_shape=jax.ShapeDtypeStruct((m, n), F32),
        scratch_shapes=[pltpu.VMEM((m, d), BF16)],
        compiler_params=pltpu.CompilerParams(
            dimension_semantics=("arbitrary", "arbitrary"), vmem_limit_bytes=VMEM_LIMIT),
        name="proj",
    )(x, g.reshape(1, d), w)


def _mem_proj(x, w):
    m, d = x.shape
    depth, _, n = w.shape
    bn = COL_BLK
    return pl.pallas_call(
        functools.partial(_proj_kernel, normalize=False),
        grid=(depth, n // bn),
        in_specs=[
            pl.BlockSpec((m, d), lambda li, j: (0, 0)),
            pl.BlockSpec((1, d), lambda li, j: (0, 0)),
            pl.BlockSpec((None, d, bn), lambda li, j: (li, 0, j)),
        ],
        out_specs=pl.BlockSpec((None, m, bn), lambda li, j: (li, 0, j)),
        out_shape=jax.ShapeDtypeStruct((depth, m, n), F32),
        scratch_shapes=[pltpu.VMEM((m, d), BF16)],
        compiler_params=pltpu.CompilerParams(
            dimension_semantics=("arbitrary", "arbitrary"), vmem_limit_bytes=VMEM_LIMIT),
        name="mem_proj",
    )(x, jnp.ones((1, d), F32), w)


def _proj_kv_kernel(*refs, tiles_per_seq, tail_tiles, aliased):
    x_ref, g_ref, w_ref = refs[:3]
    o_ref, ko_ref, vo_ref, h_ref = refs[5:] if aliased else refs[3:]
    i, j = pl.program_id(0), pl.program_id(1)

    @pl.when(j == 0)
    def _():
        _norm_rows(x_ref, g_ref, h_ref, True)

    o_ref[...] = jnp.dot(h_ref[...], w_ref[...], preferred_element_type=F32)

    in_tail = i % tiles_per_seq >= tiles_per_seq - tail_tiles
    for cb0, dst in ((CB_K, ko_ref), (CB_V, vo_ref)):
        for cb in range(cb0, cb0 + ATT_W // COL_BLK):
            @pl.when(in_tail & (j == cb))
            def _(cb=cb, cb0=cb0, dst=dst):
                for hh in range(HEADS_PER_COL_BLK):
                    head = (cb - cb0) * HEADS_PER_COL_BLK + hh
                    dst[pl.ds(head, o_ref.shape[0], stride=N_ATT_HEADS), :] = (
                        o_ref[:, hh * HEAD_DIM:(hh + 1) * HEAD_DIM])


def _shift_caches(cache_k, cache_v, after, *, shift):
    depth, n_seq, l_buf = cache_k.shape[:3]
    info = plsc.get_sparse_core_info()
    n_workers = info.num_cores * SC_ACTIVE_SUBCORES
    n_jobs = depth * n_seq
    assert n_jobs % n_workers == 0 and (l_buf - shift) % SC_COPY_ROWS == 0
    assert SC_ACTIVE_SUBCORES <= info.num_subcores
    n_chunks = (l_buf - shift) // SC_COPY_ROWS
    mesh = plsc.VectorSubcoreMesh(core_axis_name="core", subcore_axis_name="subcore")
    out = jax.ShapeDtypeStruct(cache_k.shape, cache_k.dtype)

    @pl.kernel(out_type=(out, out), mesh=mesh,
               scratch_types=[pltpu.VMEM((SC_COPY_ROWS,) + cache_k.shape[3:], cache_k.dtype)],
               name="shift_caches")
    def shift_kernel(ck_hbm, cv_hbm, after_hbm, ok_hbm, ov_hbm, buf):
        del after_hbm
        subcore = lax.axis_index("subcore")
        worker = lax.axis_index("core") * SC_ACTIVE_SUBCORES + subcore

        @pl.when(subcore < SC_ACTIVE_SUBCORES)
        def _():
            for job in range(n_jobs // n_workers):
                seq_id = worker * (n_jobs // n_workers) + job
                li, b = seq_id // n_seq, seq_id % n_seq
                for src, dst in ((ck_hbm, ok_hbm), (cv_hbm, ov_hbm)):
                    @pl.loop(0, n_chunks)
                    def _(c, src=src, dst=dst, li=li, b=b):
                        r0 = c * SC_COPY_ROWS
                        pltpu.sync_copy(src.at[li, b, pl.ds(r0 + shift, SC_COPY_ROWS)], buf)
                        pltpu.sync_copy(buf, dst.at[li, b, pl.ds(r0, SC_COPY_ROWS)])

    return shift_kernel(cache_k, cache_v, after)


def _proj_kv(x, g, w, prev, *, li, seq, l_keep):
    m, d = x.shape
    n = w.shape[2]
    bsz = m // seq
    bm, bn = PROJ_BM, COL_BLK
    tiles_per_seq, tail_tiles = seq // bm, l_keep // bm
    assert seq % bm == 0 and l_keep % bm == 0
    aliased = prev is not None
    kv_shape = jax.ShapeDtypeStruct((DEPTH, bsz, l_keep * N_ATT_HEADS, HEAD_DIM), F32)
    any_spec = pl.BlockSpec(memory_space=pl.ANY)

    def kv_map(i, j):
        return (li, i // tiles_per_seq, jnp.maximum(i % tiles_per_seq - (tiles_per_seq - tail_tiles), 0), 0)

    kv_spec = pl.BlockSpec((None, None, bm * N_ATT_HEADS, HEAD_DIM), kv_map)
    in_specs = [
        pl.BlockSpec((bm, d), lambda i, j: (i, 0)),
        pl.BlockSpec((1, d), lambda i, j: (0, 0)),
        pl.BlockSpec((None, d, bn), lambda i, j: (li, 0, j)),
    ]
    args = [x, g.reshape(1, d), w]
    if aliased:
        in_specs += [any_spec] * 2
        args += list(prev)
    return pl.pallas_call(
        functools.partial(_proj_kv_kernel, tiles_per_seq=tiles_per_seq, tail_tiles=tail_tiles,
                          aliased=aliased),
        grid=(m // bm, n // bn),
        in_specs=in_specs,
        out_specs=[pl.BlockSpec((bm, bn), lambda i, j: (i, j)), kv_spec, kv_spec],
        out_shape=[jax.ShapeDtypeStruct((m, n), F32), kv_shape, kv_shape],
        scratch_shapes=[pltpu.VMEM((bm, d), BF16)],
        input_output_aliases={3: 1, 4: 2} if aliased else {},
        compiler_params=pltpu.CompilerParams(
            dimension_semantics=("arbitrary", "arbitrary"), vmem_limit_bytes=VMEM_LIMIT),
        name="proj_kv",
    )(*args)


def _qk(q, k):
    return lax.dot_general(q, k, (((1,), (1,)), ((), ())), preferred_element_type=F32)


def _prompt_attn_kernel(q_ref, k_ref, v_ref, bn_ref, bf_ref, o_ref,
                        qb_ref, kb_ref, vb_ref, qf_ref, kf_ref, vf_ref, of_ref, lf_ref, tmp_ref, tmp2_ref,
                        *, seq):
    sub = seq // FAR_D
    chunk = 2 * Q_BLK
    val = slice(0, HEAD_DIM)

    for r0 in range(0, seq, chunk):
        rows = pl.ds(r0, chunk)
        qb_ref[rows, :] = (q_ref[rows, :] * (SCALE * LOG2E)).astype(BF16)
        kb_ref[rows, :] = k_ref[rows, :].astype(BF16)
        vb_ref[rows, val] = v_ref[rows, :].astype(BF16)
        vb_ref[rows, HEAD_DIM:] = jnp.ones((chunk, HEAD_DIM), BF16)
        vf_ref[rows, HEAD_DIM:] = jnp.ones((chunk, HEAD_DIM), BF16)

    quarter = seq // FAR_STEP
    for src_ref, dst_ref, scale in ((q_ref, qf_ref, SCALE * LOG2E), (k_ref, kf_ref, None),
                                    (v_ref, vf_ref, None)):
        for r1 in range(FAR_STEP):
            tmp_ref[r1 * quarter:(r1 + 1) * quarter, :] = src_ref[pl.ds(r1, quarter, stride=FAR_STEP), :]
        for r1 in range(FAR_STEP):
            for r2 in range(FAR_STEP):
                r = FAR_STEP * r2 + r1
                x = tmp_ref[pl.ds(r1 * quarter + r2, sub, stride=FAR_STEP), :]
                if scale is not None:
                    x = x * scale
                dst_ref[r * sub:(r + 1) * sub, val] = x.astype(BF16)

    far_geo = []
    for r in range(FAR_D):
        for a_blk in range(sub // Q_BLK):
            n_kt = min(a_blk, 1) + 1
            rows = pl.ds(r * sub + a_blk * Q_BLK, Q_BLK)
            keys = pl.ds(r * sub + (a_blk + 1 - n_kt) * Q_BLK, n_kt * Q_BLK)
            r1, r2 = r % FAR_STEP, r // FAR_STEP
            dst = pl.ds(r1 * quarter + FAR_STEP * a_blk * Q_BLK + r2, Q_BLK, stride=FAR_STEP)
            far_geo.append((rows, keys, n_kt, dst))
    near_geo = []
    for i in range(seq // Q_BLK):
        n_kt = min(i + 1, NEAR_KBLK)
        near_geo.append((pl.ds(i * Q_BLK, Q_BLK), pl.ds((i + 1 - n_kt) * Q_BLK, n_kt * Q_BLK), n_kt))

    for g0 in range(0, len(far_geo), STAGE_TILES):
        geo = far_geo[g0:g0 + STAGE_TILES]
        scores = [_qk(qf_ref[rows, :], kf_ref[keys, :]) + bf_ref[:, (2 - n_kt) * Q_BLK:]
                  for rows, keys, n_kt, _ in geo]
        probs = []
        for s in scores:
            m = jnp.max(s, axis=-1, keepdims=True)
            probs.append((jnp.exp2(s - m).astype(BF16), m))
        for (_, keys, _, dst), (p, m) in zip(geo, probs):
            o = jnp.dot(p, vf_ref[keys, :], preferred_element_type=F32)
            l = o[:, HEAD_DIM:]
            tmp_ref[dst, :] = o[:, val] / l
            tmp2_ref[dst, :] = m + jnp.log2(l)
    for r1 in range(FAR_STEP):
        of_ref[pl.ds(r1, quarter, stride=FAR_STEP), :] = tmp_ref[r1 * quarter:(r1 + 1) * quarter, :]
        lf_ref[pl.ds(r1, quarter, stride=FAR_STEP), :] = tmp2_ref[r1 * quarter:(r1 + 1) * quarter, :]

    for g0 in range(0, len(near_geo), STAGE_TILES):
        geo = near_geo[g0:g0 + STAGE_TILES]
        scores = [_qk(qb_ref[rows, :], kb_ref[keys, :]) + bn_ref[:, (NEAR_KBLK - n_kt) * Q_BLK:]
                  for rows, keys, n_kt in geo]
        probs = []
        for (rows, _, _), s in zip(geo, scores):
            m_far = lf_ref[rows, 0:1]
            m = jnp.maximum(jnp.max(s, axis=-1, keepdims=True), m_far)
            probs.append((jnp.exp2(s - m).astype(BF16), jnp.exp2(m_far - m)))
        for (rows, keys, _), (p, w_far) in zip(geo, probs):
            o = jnp.dot(p, vb_ref[keys, :], preferred_element_type=F32)
            acc = w_far * of_ref[rows, :] + o[:, val]
            o_ref[rows, :] = acc / (w_far + o[:, HEAD_DIM:])


def _prompt_attn(z, bias_near, bias_far):
    b, s, _ = z.shape
    assert s % (FAR_D * Q_BLK) == 0

    def head_cols(cb):
        return pl.BlockSpec((None, s, HEAD_DIM), lambda bi, h: (bi, 0, cb * HEADS_PER_COL_BLK + h))

    return pl.pallas_call(
        functools.partial(_prompt_attn_kernel, seq=s),
        grid=(b, N_ATT_HEADS),
        in_specs=[
            head_cols(CB_Q), head_cols(CB_K), head_cols(CB_V),
            pl.BlockSpec((None, Q_BLK, NEAR_KBLK * Q_BLK), lambda bi, h: (h, 0, 0)),
            pl.BlockSpec((None, Q_BLK, 2 * Q_BLK), lambda bi, h: (h, 0, 0)),
        ],
        out_specs=pl.BlockSpec((None, s, HEAD_DIM), lambda bi, h: (bi, 0, h)),
        out_shape=jax.ShapeDtypeStruct((b, s, ATT_W), F32),
        scratch_shapes=[pltpu.VMEM((s, HEAD_DIM), BF16), pltpu.VMEM((s, HEAD_DIM), BF16),
                        pltpu.VMEM((s, 2 * HEAD_DIM), BF16)] * 2 + [pltpu.VMEM((s, HEAD_DIM), F32)] * 4,
        compiler_params=pltpu.CompilerParams(
            dimension_semantics=("parallel", "parallel"), vmem_limit_bytes=VMEM_LIMIT),
        name="prompt_attn",
    )(z, z, z, bias_near, bias_far)


def _sample_attn_kernel(q_ref, kn_ref, vn_ref, kc_ref, vc_ref, bias_ref, biasn_ref, sk_hbm, sv_hbm,
                        o_ref, ko_ref, vo_ref, qa_ref, kn3_ref, vn3_ref, m_ref, l_ref, acc_ref, *, t_len, rb):
    del sk_hbm, sv_hbm
    j = pl.program_id(1)
    last = pl.num_programs(1) - 1
    n_heads = N_ATT_HEADS

    @pl.when(j == 0)
    def _():
        for h in range(n_heads):
            cols = slice(h * HEAD_DIM, (h + 1) * HEAD_DIM)
            qa_ref[h * t_len:(h + 1) * t_len, :] = q_ref[:, cols] * SCALE
            kn3_ref[:, h, :] = kn_ref[:, cols]
            vn3_ref[:, h, :] = vn_ref[:, cols]
        m_ref[...] = jnp.full(m_ref.shape, NEG, F32)
        l_ref[...] = jnp.zeros(l_ref.shape, F32)
        acc_ref[...] = jnp.zeros(acc_ref.shape, F32)

    qa = qa_ref[...].astype(BF16)

    def accumulate(k, v, bias):
        s = _qk(qa, k.astype(BF16)) + bias
        m_old = m_ref[...]
        m_new = jnp.maximum(m_old, jnp.max(s, axis=-1, keepdims=True))
        alpha = jnp.exp(m_old - m_new)
        p = jnp.exp(s - m_new)
        l_ref[...] = alpha * l_ref[...] + jnp.sum(p, axis=-1, keepdims=True)
        acc_ref[...] = alpha * acc_ref[...] + jnp.dot(p.astype(BF16), v.astype(BF16),
                                                      preferred_element_type=F32)
        m_ref[...] = m_new

    accumulate(kc_ref[...].reshape(rb * n_heads, HEAD_DIM), vc_ref[...].reshape(rb * n_heads, HEAD_DIM),
               bias_ref[...])

    @pl.when(j == last)
    def _():
        ko_ref[...] = kn3_ref[...]
        vo_ref[...] = vn3_ref[...]
        accumulate(kn3_ref[...].reshape(t_len * n_heads, HEAD_DIM),
                   vn3_ref[...].reshape(t_len * n_heads, HEAD_DIM), biasn_ref[...])
        o = acc_ref[...] / l_ref[...]
        for h in range(n_heads):
            o_ref[:, h * HEAD_DIM:(h + 1) * HEAD_DIM] = o[h * t_len:(h + 1) * t_len, :]


def _sample_attn(z, cache_k, cache_v, bias, bias_new, stack_k, stack_v, *, li):
    b, t_len, _ = z.shape
    l_buf = cache_k.shape[2]
    rb = SAMPLE_RB
    assert t_len == SUBLANES and l_buf % rb == 0
    rows = N_ATT_HEADS * t_len

    def zspec(cb):
        return pl.BlockSpec((None, t_len, ATT_W), lambda bi, j: (bi, 0, cb * COL_BLK // ATT_W))

    blk_spec = pl.BlockSpec((None, None, rb, N_ATT_HEADS, HEAD_DIM), lambda bi, j: (li, bi, j, 0, 0))
    new_spec = pl.BlockSpec((None, None, t_len, N_ATT_HEADS, HEAD_DIM),
                            lambda bi, j: (li, bi, l_buf // t_len - 1, 0, 0))
    any_spec = pl.BlockSpec(memory_space=pl.ANY)
    in_specs = [
        zspec(CB_Q), zspec(CB_K), zspec(CB_V), blk_spec, blk_spec,
        pl.BlockSpec((rows, rb * N_ATT_HEADS), lambda bi, j: (0, j)),
        pl.BlockSpec((rows, rows), lambda bi, j: (0, 0)),
        any_spec, any_spec,
    ]
    args = [z, z, z, cache_k, cache_v, bias, bias_new, stack_k, stack_v]
    return pl.pallas_call(
        functools.partial(_sample_attn_kernel, t_len=t_len, rb=rb),
        grid=(b, l_buf // rb),
        in_specs=in_specs,
        out_specs=[pl.BlockSpec((None, t_len, ATT_W), lambda bi, j: (bi, 0, 0)), new_spec, new_spec],
        out_shape=[
            jax.ShapeDtypeStruct((b, t_len, ATT_W), F32),
            jax.ShapeDtypeStruct(cache_k.shape, F32),
            jax.ShapeDtypeStruct(cache_v.shape, F32),
        ],
        scratch_shapes=[
            pltpu.VMEM((rows, HEAD_DIM), F32),
            pltpu.VMEM((t_len, N_ATT_HEADS, HEAD_DIM), F32), pltpu.VMEM((t_len, N_ATT_HEADS, HEAD_DIM), F32),
            pltpu.VMEM((rows, 1), F32), pltpu.VMEM((rows, 1), F32), pltpu.VMEM((rows, HEAD_DIM), F32),
        ],
        input_output_aliases={7: 1, 8: 2},
        compiler_params=pltpu.CompilerParams(
            dimension_semantics=("arbitrary", "arbitrary"), vmem_limit_bytes=VMEM_LIMIT),
        name="sample_attn",
    )(*args)


def _silu(x):
    return x * jax.nn.sigmoid(x)


def _mix_kernel(x_ref, a_ref, ga_ref, uv_ref, ug_ref, gc_ref, qm_ref, gm_ref, cinit_ref, mk_ref, mv_ref,
                wdw_ref, bdw_ref, lng_ref, lnb_ref, wpw_ref, wout_ref, gpost_ref,
                xo_ref, cs_ref, uext_ref, mix_ref, *, nb, tile, conv_chunk):
    @pl.when(pl.program_id(1) == 0)
    def _():
        uext_ref[:, 0:HALO, :] = cinit_ref[...]

    first = HALO - (CONV_K - 1)
    chunks = []
    for b in range(nb):
        ub_ref = uext_ref.at[b]
        ub_ref[HALO:HALO + tile, :] = uv_ref[b] * jax.nn.sigmoid(ug_ref[b])
        for c0 in range(0, tile, conv_chunk):
            acc = jnp.broadcast_to(bdw_ref[...], (conv_chunk, CONV_CH))
            for phase in range(SUBLANES):
                rows = conv_chunk + (SUBLANES if phase else 0)
                part = None
                for k in range(CONV_K):
                    if (first + k) % SUBLANES == phase:
                        base = c0 + first + k - phase
                        term = ub_ref[base:base + rows, :] * wdw_ref[k:k + 1, :]
                        part = term if part is None else part + term
                if part is not None:
                    acc = acc + part[phase:phase + conv_chunk, :]
            chunks.append(acc)
        cs_ref[b] = ub_ref[tile + first:tile + HALO, :]
        tail = ub_ref[tile:tile + HALO, :]
        ub_ref[0:HALO, :] = tail
    c = jnp.concatenate(chunks, axis=0) if len(chunks) > 1 else chunks[0]

    mu = jnp.mean(c, axis=-1, keepdims=True)
    var = jnp.mean(jnp.square(c - mu), axis=-1, keepdims=True)
    c = (c - mu) * lax.rsqrt(var + EPS) * lng_ref[...] + lnb_ref[...]
    c = jnp.dot(_silu(c).astype(BF16), wpw_ref[...], preferred_element_type=F32)

    def mem_head(ref, h):
        if len(ref.shape) == 3:
            return ref[:, h, :].astype(BF16)
        return ref[:, h * HEAD_DIM:(h + 1) * HEAD_DIM].astype(BF16)

    for b in range(nb):
        rows = slice(b * tile, (b + 1) * tile)
        mix_ref[rows, 0:ATT_W] = (a_ref[b] * _silu(ga_ref[b])).astype(BF16)
        mix_ref[rows, ATT_W:ATT_W + CONV_CH] = (c[rows] * _silu(gc_ref[b])).astype(BF16)
        for h in range(N_X_HEADS):
            cols = slice(h * HEAD_DIM, (h + 1) * HEAD_DIM)
            q = (qm_ref[b, :, cols] * SCALE).astype(BF16)
            s = _qk(q, mem_head(mk_ref.at[b], h))
            p = jnp.exp(s - jnp.max(s, axis=-1, keepdims=True))
            o = jnp.dot(p.astype(BF16), mem_head(mv_ref.at[b], h), preferred_element_type=F32)
            o = o / jnp.sum(p, axis=-1, keepdims=True)
            off = ATT_W + CONV_CH + h * HEAD_DIM
            mix_ref[rows, off:off + HEAD_DIM] = (o * _silu(gm_ref[b, :, cols])).astype(BF16)

    y = jnp.dot(mix_ref[...], wout_ref[...], preferred_element_type=F32)
    y = y * lax.rsqrt(jnp.mean(y * y, axis=-1, keepdims=True) + EPS) * gpost_ref[...]
    for b in range(nb):
        xo_ref[b] = x_ref[b] + y[b * tile:(b + 1) * tile]


def _mix(x, z, a, cinit, mk_arr, mk_block, mv_arr, mv_block, w_dw, b_dw, ln_g, ln_b, w_pw2, w_out, g_post,
         *, li, nb, tile):
    b, s, _ = x.shape
    conv_chunk = min(tile, 32)
    assert b % nb == 0 and s % tile == 0

    def rows(width, cb=0):
        return pl.BlockSpec((nb, tile, width), lambda bi, t: (bi, t, cb * COL_BLK // width))

    def const(shape):
        return pl.BlockSpec(shape, lambda bi, t: (0,) * len(shape))

    def layer(shape):
        return pl.BlockSpec((None,) + shape, lambda bi, t: (li,) + (0,) * len(shape))

    def mem(block):
        shape, index = block
        return pl.BlockSpec(shape, lambda bi, t: index(bi))

    return pl.pallas_call(
        functools.partial(_mix_kernel, nb=nb, tile=tile, conv_chunk=conv_chunk),
        grid=(b // nb, s // tile),
        in_specs=[
            rows(D_MODEL), rows(ATT_W),
            rows(ATT_W, CB_GA), rows(COL_BLK, CB_UV), rows(COL_BLK, CB_UG), rows(COL_BLK, CB_GC),
            rows(COL_BLK, CB_QM), rows(COL_BLK, CB_GM),
            pl.BlockSpec((nb, HALO, CONV_CH), lambda bi, t: (bi, 0, 0)),
            mem(mk_block), mem(mv_block),
            const((CONV_K, CONV_CH)), const((1, CONV_CH)), const((1, CONV_CH)), const((1, CONV_CH)),
            layer((CONV_CH, CONV_CH)), layer((MIX_W, D_MODEL)), const((1, D_MODEL)),
        ],
        out_specs=[
            rows(D_MODEL),
            pl.BlockSpec((nb, CONV_K - 1, CONV_CH), lambda bi, t: (bi, 0, 0)),
        ],
        out_shape=[
            jax.ShapeDtypeStruct((b, s, D_MODEL), F32),
            jax.ShapeDtypeStruct((b, CONV_K - 1, CONV_CH), F32),
        ],
        scratch_shapes=[pltpu.VMEM((nb, HALO + tile, CONV_CH), F32), pltpu.VMEM((nb * tile, MIX_W), BF16)],
        compiler_params=pltpu.CompilerParams(
            dimension_semantics=("parallel", "arbitrary"), vmem_limit_bytes=VMEM_LIMIT),
        name="mix",
    )(x, a, z, z, z, z, z, z, cinit, mk_arr, mv_arr,
      w_dw, b_dw.reshape(1, -1), ln_g.reshape(1, -1), ln_b.reshape(1, -1), w_pw2, w_out,
      g_post.reshape(1, -1))


def kernel(x_prompt, x_sample, mem_prompt, cache_attn_k, cache_attn_v, state_conv, cache_mem_k, cache_mem_v,
           rel_bias, norm_pre_g, w_in, w_dw, b_dw, ln_conv_g, ln_conv_b, w_pw2, w_mem_kv, w_out, norm_post_g):
    bp, s_len, _ = x_prompt.shape
    bs, t_len, _ = x_sample.shape
    l_buf = cache_attn_k.shape[2]
    l_prompt = min(WIN, s_len)

    w_in_b, w_pw2_b = w_in.astype(BF16), w_pw2.astype(BF16)
    w_mem_b, w_out_b = w_mem_kv.astype(BF16), w_out.astype(BF16)

    tab_near = _distance_table(rel_bias, np.arange(NEAR_REACH + 1), NEAR_PATTERNS)
    tab_far = _distance_table(rel_bias, FAR_D * np.arange(FAR_KEYS + 1), DIL_PATTERNS[-1:])
    bias_near = _toeplitz(tab_near * LOG2E, NEAR_REACH, Q_BLK, NEAR_KBLK * Q_BLK)
    bias_far = _toeplitz(tab_far * LOG2E, FAR_KEYS, Q_BLK, 2 * Q_BLK)
    tab_all = _distance_table(rel_bias, np.arange(WIN + 1), DIL_PATTERNS)
    bias_s = _head_matched_bias(tab_all, l_buf, t_len, l_buf)
    bias_s_new = _head_matched_bias(tab_all, 0, t_len, t_len)

    mkv = _mem_proj(mem_prompt.reshape(bp * N_MEM, D_MODEL), w_mem_b).reshape(DEPTH, bp, N_MEM, 2 * X_W)
    cinit_p = jnp.zeros((bp, HALO, CONV_CH), F32)
    pad_s = jnp.zeros((DEPTH, bs, HALO - (CONV_K - 1), CONV_CH), F32)
    cinit_s = jnp.concatenate([pad_s, state_conv], axis=2)
    xp, xs = x_prompt, x_sample
    stacks = None
    after = (w_in_b[0, 0, :LANES].astype(F32) + w_out_b[0, 0, :LANES].astype(F32)
             + w_mem_b[0, 0, :LANES].astype(F32) + mkv[0, 0, 0, :LANES])
    aks, avs = _shift_caches(cache_attn_k, cache_attn_v, after, shift=t_len)
    cvp, cvs = [], []
    layer_w = [(w_dw[li], b_dw[li], ln_conv_g[li], ln_conv_b[li], w_pw2_b, w_out_b, norm_post_g[li])
               for li in range(DEPTH)]
    for li in range(DEPTH):
        zp, akp, avp = _proj_kv(xp.reshape(bp * s_len, D_MODEL), norm_pre_g[li], w_in_b, stacks,
                                li=li, seq=s_len, l_keep=l_prompt)
        stacks = (akp, avp)
        zp = zp.reshape(bp, s_len, IN_W)
        a = _prompt_attn(zp, bias_near, bias_far)
        mk_block, mv_block = (((None, MIX_NB, N_MEM, X_W), lambda bi, li=li, half=half: (li, bi, 0, half))
                              for half in (0, 1))
        xp, cst = _mix(xp, zp, a, cinit_p, mkv, mk_block, mkv, mv_block, *layer_w[li],
                       li=li, nb=MIX_NB, tile=MIX_TILE)
        cvp.append(cst)

    for li in range(DEPTH):
        wargs = layer_w[li]
        zs = _proj(xs.reshape(bs * t_len, D_MODEL), norm_pre_g[li], w_in_b, li=li, bn=COL_BLK)
        zs = zs.reshape(bs, t_len, IN_W)
        a, aks, avs = _sample_attn(zs, cache_attn_k, cache_attn_v, bias_s, bias_s_new, aks, avs, li=li)
        mem_block = ((None, bs, N_MEM, N_X_HEADS, HEAD_DIM), lambda bi, li=li: (li, bi, 0, 0, 0))
        xs, cst = _mix(xs, zs, a, cinit_s[li], cache_mem_k, mem_block, cache_mem_v, mem_block, *wargs,
                       li=li, nb=bs, tile=t_len)
        cvs.append(cst)

    kv5 = (DEPTH, bp, l_prompt, N_ATT_HEADS, HEAD_DIM)
    mem5 = (DEPTH, bp, N_MEM, N_X_HEADS, HEAD_DIM)
    return (xp, xs, akp.reshape(kv5), avp.reshape(kv5), jnp.stack(cvp),
            mkv[..., :X_W].reshape(mem5), mkv[..., X_W:].reshape(mem5), aks, avs, jnp.stack(cvs))
```

```python
import functools

import numpy as np
import jax
import jax.numpy as jnp
from jax import lax
from jax.experimental import pallas as pl
from jax.experimental.pallas import tpu as pltpu

D_MODEL = 2048
DEPTH = 4
N_MEM = 256
HEAD_DIM = 128
ATT_W = D_MODEL // 2
N_ATT_HEADS = ATT_W // HEAD_DIM
DIL_PATTERNS = ((128, 1), (512, 4), (2048, 16))
WIN = max(w for w, _ in DIL_PATTERNS)
N_BUCKETS = 32
MAX_DIST = WIN
CONV_CH = D_MODEL // 4
CONV_K = 31
X_W = D_MODEL // 4
N_X_HEADS = X_W // HEAD_DIM
MIX_W = ATT_W + CONV_CH + X_W
IN_W = 4 * ATT_W + 3 * CONV_CH + 2 * X_W
EPS = 1e-6
NEG = -1e30
SCALE = HEAD_DIM ** -0.5
LOG2E = 1.4426950408889634

COL_BLK = 512
CB_Q, CB_K, CB_V, CB_GA = 0, 2, 4, 6
CB_UV, CB_UG, CB_GC, CB_QM, CB_GM = 8, 9, 10, 11, 12
HEADS_PER_COL_BLK = COL_BLK // HEAD_DIM

Q_BLK = 128
FAR_W, FAR_D = DIL_PATTERNS[-1]
NEAR_PATTERNS = DIL_PATTERNS[:-1]
NEAR_REACH = max(w for w, _ in NEAR_PATTERNS)
NEAR_KBLK = NEAR_REACH // Q_BLK + 1
FAR_KEYS = FAR_W // FAR_D
FAR_STEP = 4
assert FAR_KEYS == Q_BLK and NEAR_REACH % Q_BLK == 0 and FAR_STEP * FAR_STEP == FAR_D
MIX_NB, MIX_TILE = 1, 512
STAGE_TILES = 16
PROJ_BM = 1024
COPY_CHUNKS_PER_SEQ = 6
COPY_SLOTS = 3
SAMPLE_RB = 1024
HALO = 32
SUBLANES = 8
LANES = 128
VMEM_LIMIT = 56 * 1024 * 1024

BF16 = jnp.bfloat16
F32 = jnp.float32


def _t5_bucket(dist):
    dist = np.asarray(dist)
    max_exact = N_BUCKETS // 2
    large = max_exact + (np.log(np.maximum(dist, 1) / max_exact)
                         / np.log(MAX_DIST / max_exact) * (N_BUCKETS - max_exact)).astype(np.int32)
    large = np.minimum(large, N_BUCKETS - 1)
    return np.where(dist < max_exact, dist, large).astype(np.int32)


def _distance_table(rel_bias, dists, patterns):
    dists = np.asarray(dists)
    cnt = np.zeros(dists.shape, np.int64)
    for w, dil in patterns:
        cnt += ((dists % dil == 0) & (dists <= w)).astype(np.int64)
    logc = jnp.log(jnp.asarray(np.maximum(cnt, 1), F32))
    tab = rel_bias[_t5_bucket(dists)].T.astype(F32) + logc[None, :]
    tab = jnp.where(jnp.asarray(cnt > 0)[None, :], tab, NEG)
    return jnp.concatenate([tab, jnp.full((tab.shape[0], 1), NEG, F32)], axis=1)


def _toeplitz(tab, off, rows, cols):
    n_valid = tab.shape[1] - 1
    lp = rows + cols
    j = np.concatenate([np.arange(cols), np.full(lp - (rows + cols - 1), off + n_valid),
                        np.arange(-(rows - 1), 0)])
    d = off - j
    idx = np.where((d >= 0) & (d < n_valid), d, n_valid)
    v = jnp.take(tab, jnp.asarray(idx, jnp.int32), axis=1)
    flat = jnp.tile(v, (1, rows))[:, :rows * (lp - 1)]
    return flat.reshape(tab.shape[0], rows, lp - 1)[:, :, :cols]


def _head_matched_bias(tab, off, t_len, n_rows):
    n_h, n_valid = tab.shape[0], tab.shape[1] - 1
    n_cols = n_rows * n_h
    gaps = jnp.full((n_h, n_valid, n_h - 1), NEG, F32)
    s = jnp.concatenate([tab[:, :n_valid, None], gaps], axis=2).reshape(n_h, n_valid * n_h)
    k_max = n_h * (off + t_len - 1) + n_h - 1
    pad_l = max(0, n_cols - 1 - n_h * off)
    pad_r = max(0, k_max - (n_valid * n_h - 1))
    s = jnp.concatenate([jnp.full((n_h, pad_l), NEG, F32), s, jnp.full((n_h, pad_r), NEG, F32)], axis=1)
    n_s = s.shape[1]
    lane_pad = -n_s % LANES
    rev = jnp.concatenate([s[:, ::-1], jnp.full((n_h, lane_pad), NEG, F32)], axis=1)
    starts = tuple((h, n_s - 1 - (n_h * (off + t) + h) - pad_l) for h in range(n_h) for t in range(t_len))
    return pl.pallas_call(
        functools.partial(_slice_rows_kernel, starts=starts, n_cols=n_cols),
        out_shape=jax.ShapeDtypeStruct((len(starts), n_cols), F32),
        name="bias_rows",
    )(rev)


def _slice_rows_kernel(src_ref, out_ref, *, starts, n_cols):
    for r, (h, start) in enumerate(starts):
        lo = start // LANES * LANES
        hi = -(-(start + n_cols) // LANES) * LANES
        window = src_ref[h:h + 1, lo:hi]
        out_ref[r:r + 1, :] = window[:, start - lo:start - lo + n_cols]


def _norm_rows(x_ref, g_ref, h_ref, normalize):
    x = x_ref[...]
    if normalize:
        x = x * lax.rsqrt(jnp.mean(x * x, axis=-1, keepdims=True) + EPS) * g_ref[...]
    h_ref[...] = x.astype(BF16)


def _proj_kernel(x_ref, g_ref, w_ref, o_ref, h_ref, *, normalize):
    @pl.when(pl.program_id(1) == 0)
    def _():
        _norm_rows(x_ref, g_ref, h_ref, normalize)

    o_ref[...] = jnp.dot(h_ref[...], w_ref[...], preferred_element_type=F32)


def _proj_cast_kernel(x_ref, g_ref, w_ref, o_ref, wb_ref, h_ref):
    @pl.when(pl.program_id(0) == 0)
    def _():
        _norm_rows(x_ref, g_ref, h_ref, True)

    w = w_ref[...].astype(BF16)
    wb_ref[...] = w
    o_ref[...] = jnp.dot(h_ref[...], w, preferred_element_type=F32)


def _proj_cast(x, g, w, *, li, bn):
    m, d = x.shape
    n = w.shape[2]
    return pl.pallas_call(
        _proj_cast_kernel,
        grid=(n // bn,),
        in_specs=[
            pl.BlockSpec((m, d), lambda j: (0, 0)),
            pl.BlockSpec((1, d), lambda j: (0, 0)),
            pl.BlockSpec((None, d, bn), lambda j: (li, 0, j)),
        ],
        out_specs=[pl.BlockSpec((m, bn), lambda j: (0, j)), pl.BlockSpec((d, bn), lambda j: (0, j))],
        out_shape=[jax.ShapeDtypeStruct((m, n), F32), jax.ShapeDtypeStruct((d, n), BF16)],
        scratch_shapes=[pltpu.VMEM((m, d), BF16)],
        compiler_params=pltpu.CompilerParams(
            dimension_semantics=("arbitrary",), vmem_limit_bytes=VMEM_LIMIT),
        name="proj_cast",
    )(x, g.reshape(1, d), w)


def _mem_proj(x, w):
    m, d = x.shape
    depth, _, n = w.shape
    bn = COL_BLK
    return pl.pallas_call(
        functools.partial(_proj_kernel, normalize=False),
        grid=(depth, n // bn),
        in_specs=[
            pl.BlockSpec((m, d), lambda li, j: (0, 0)),
            pl.BlockSpec((1, d), lambda li, j: (0, 0)),
            pl.BlockSpec((None, d, bn), lambda li, j: (li, 0, j)),
        ],
        out_specs=pl.BlockSpec((None, m, bn), lambda li, j: (li, 0, j)),
        out_shape=jax.ShapeDtypeStruct((depth, m, n), F32),
        scratch_shapes=[pltpu.VMEM((m, d), BF16)],
        compiler_params=pltpu.CompilerParams(
            dimension_semantics=("arbitrary", "arbitrary"), vmem_limit_bytes=VMEM_LIMIT),
        name="mem_proj",
    )(x, jnp.ones((1, d), F32), w)


def _proj_kv_kernel(*refs, li, tiles_per_seq, tail_tiles, aliased, shift, chunk):
    x_ref, g_ref, w_ref, ck_hbm, cv_hbm = refs[:5]
    o_ref, ko_ref, vo_ref, sk_hbm, sv_hbm, h_ref, ring_ref, sem_in, sem_out = refs[9:] if aliased else refs[5:]
    i, j = pl.program_id(0), pl.program_id(1)
    step = i * pl.num_programs(1) + j
    n_seq, l_buf = ck_hbm.shape[1], ck_hbm.shape[2]
    per_seq = (l_buf - shift) // chunk
    n_chunks = 2 * n_seq * per_seq

    def chunk_dma(c, which, read):
        src, dst = ((ck_hbm, sk_hbm), (cv_hbm, sv_hbm))[which]
        b, r = (c // per_seq) % n_seq, c % per_seq
        slot = c % COPY_SLOTS
        if read:
            return pltpu.make_async_copy(src.at[li, b, pl.ds(shift + r * chunk, chunk)], ring_ref.at[slot],
                                         sem_in.at[slot])
        return pltpu.make_async_copy(ring_ref.at[slot], dst.at[li, b, pl.ds(r * chunk, chunk)],
                                     sem_out.at[slot])

    def on_chunk(c, live, action):
        for which in (0, 1):
            @pl.when(live & (c // (n_seq * per_seq) == which))
            def _(which=which):
                action(c, which)

    on_chunk(step - 2, (step >= 2) & (step - 2 < n_chunks), lambda c, w: chunk_dma(c, w, False).wait())

    def turn_around(c, w):
        chunk_dma(c, w, True).wait()
        chunk_dma(c, w, False).start()

    on_chunk(step - 1, (step >= 1) & (step - 1 < n_chunks), turn_around)
    on_chunk(step, step < n_chunks, lambda c, w: chunk_dma(c, w, True).start())

    @pl.when(j == 0)
    def _():
        _norm_rows(x_ref, g_ref, h_ref, True)

    o_ref[...] = jnp.dot(h_ref[...], w_ref[...], preferred_element_type=F32)

    in_tail = i % tiles_per_seq >= tiles_per_seq - tail_tiles
    for cb0, dst in ((CB_K, ko_ref), (CB_V, vo_ref)):
        for cb in range(cb0, cb0 + ATT_W // COL_BLK):
            @pl.when(in_tail & (j == cb))
            def _(cb=cb, cb0=cb0, dst=dst):
                for hh in range(HEADS_PER_COL_BLK):
                    head = (cb - cb0) * HEADS_PER_COL_BLK + hh
                    dst[pl.ds(head, o_ref.shape[0], stride=N_ATT_HEADS), :] = (
                        o_ref[:, hh * HEAD_DIM:(hh + 1) * HEAD_DIM])


def _proj_kv(x, g, w, cache_k, cache_v, prev, *, li, seq, l_keep, shift):
    m, d = x.shape
    n = w.shape[1]
    bsz = m // seq
    bm, bn = PROJ_BM, COL_BLK
    tiles_per_seq, tail_tiles = seq // bm, l_keep // bm
    assert seq % bm == 0 and l_keep % bm == 0
    aliased = prev is not None
    kv_shape = jax.ShapeDtypeStruct((DEPTH, bsz, l_keep * N_ATT_HEADS, HEAD_DIM), F32)
    cache_shape = jax.ShapeDtypeStruct(cache_k.shape, F32)
    any_spec = pl.BlockSpec(memory_space=pl.ANY)
    n_steps = (m // bm) * (n // bn)
    n_seq, l_buf = cache_k.shape[1], cache_k.shape[2]
    chunk = (l_buf - shift) // COPY_CHUNKS_PER_SEQ
    assert (l_buf - shift) % COPY_CHUNKS_PER_SEQ == 0 and 2 * n_seq * COPY_CHUNKS_PER_SEQ + 2 <= n_steps

    def kv_map(i, j):
        return (li, i // tiles_per_seq, jnp.maximum(i % tiles_per_seq - (tiles_per_seq - tail_tiles), 0), 0)

    kv_spec = pl.BlockSpec((None, None, bm * N_ATT_HEADS, HEAD_DIM), kv_map)
    in_specs = [
        pl.BlockSpec((bm, d), lambda i, j: (i, 0)),
        pl.BlockSpec((1, d), lambda i, j: (0, 0)),
        pl.BlockSpec((d, bn), lambda i, j: (0, j)),
        any_spec, any_spec,
    ]
    args = [x, g.reshape(1, d), w, cache_k, cache_v]
    if aliased:
        in_specs += [any_spec] * 4
        args += list(prev)
    return pl.pallas_call(
        functools.partial(_proj_kv_kernel, li=li, tiles_per_seq=tiles_per_seq, tail_tiles=tail_tiles,
                          aliased=aliased, shift=shift, chunk=chunk),
        grid=(m // bm, n // bn),
        in_specs=in_specs,
        out_specs=[pl.BlockSpec((bm, bn), lambda i, j: (i, j)), kv_spec, kv_spec, any_spec, any_spec],
        out_shape=[jax.ShapeDtypeStruct((m, n), F32), kv_shape, kv_shape, cache_shape, cache_shape],
        scratch_shapes=[pltpu.VMEM((bm, d), BF16),
                        pltpu.VMEM((COPY_SLOTS, chunk) + cache_k.shape[3:], F32),
                        pltpu.SemaphoreType.DMA((COPY_SLOTS,)), pltpu.SemaphoreType.DMA((COPY_SLOTS,))],
        input_output_aliases={5: 1, 6: 2, 7: 3, 8: 4} if aliased else {},
        compiler_params=pltpu.CompilerParams(
            dimension_semantics=("arbitrary", "arbitrary"), vmem_limit_bytes=VMEM_LIMIT),
        name="proj_kv",
    )(*args)


def _qk(q, k):
    return lax.dot_general(q, k, (((1,), (1,)), ((), ())), preferred_element_type=F32)


def _prompt_attn_kernel(q_ref, k_ref, v_ref, bn_ref, bf_ref, o_ref,
                        qb_ref, kb_ref, vb_ref, qf_ref, kf_ref, vf_ref, of_ref, lf_ref, tmp_ref, tmp2_ref,
                        *, seq):
    sub = seq // FAR_D
    chunk = 2 * Q_BLK
    val = slice(0, HEAD_DIM)

    for r0 in range(0, seq, chunk):
        rows = pl.ds(r0, chunk)
        qb_ref[rows, :] = (q_ref[rows, :] * (SCALE * LOG2E)).astype(BF16)
        kb_ref[rows, :] = k_ref[rows, :].astype(BF16)
        vb_ref[rows, val] = v_ref[rows, :].astype(BF16)
        vb_ref[rows, HEAD_DIM:] = jnp.ones((chunk, HEAD_DIM), BF16)
        vf_ref[rows, HEAD_DIM:] = jnp.ones((chunk, HEAD_DIM), BF16)

    quarter = seq // FAR_STEP
    for src_ref, dst_ref, scale in ((q_ref, qf_ref, SCALE * LOG2E), (k_ref, kf_ref, None),
                                    (v_ref, vf_ref, None)):
        for r1 in range(FAR_STEP):
            tmp_ref[r1 * quarter:(r1 + 1) * quarter, :] = src_ref[pl.ds(r1, quarter, stride=FAR_STEP), :]
        for r1 in range(FAR_STEP):
            for r2 in range(FAR_STEP):
                r = FAR_STEP * r2 + r1
                x = tmp_ref[pl.ds(r1 * quarter + r2, sub, stride=FAR_STEP), :]
                if scale is not None:
                    x = x * scale
                dst_ref[r * sub:(r + 1) * sub, val] = x.astype(BF16)

    far_geo = []
    for r in range(FAR_D):
        for a_blk in range(sub // Q_BLK):
            n_kt = min(a_blk, 1) + 1
            rows = pl.ds(r * sub + a_blk * Q_BLK, Q_BLK)
            keys = pl.ds(r * sub + (a_blk + 1 - n_kt) * Q_BLK, n_kt * Q_BLK)
            r1, r2 = r % FAR_STEP, r // FAR_STEP
            dst = pl.ds(r1 * quarter + FAR_STEP * a_blk * Q_BLK + r2, Q_BLK, stride=FAR_STEP)
            far_geo.append((rows, keys, n_kt, dst))
    near_geo = []
    for i in range(seq // Q_BLK):
        n_kt = min(i + 1, NEAR_KBLK)
        near_geo.append((pl.ds(i * Q_BLK, Q_BLK), pl.ds((i + 1 - n_kt) * Q_BLK, n_kt * Q_BLK), n_kt))

    for g0 in range(0, len(far_geo), STAGE_TILES):
        geo = far_geo[g0:g0 + STAGE_TILES]
        scores = [_qk(qf_ref[rows, :], kf_ref[keys, :]) + bf_ref[:, (2 - n_kt) * Q_BLK:]
                  for rows, keys, n_kt, _ in geo]
        probs = []
        for s in scores:
            m = jnp.max(s, axis=-1, keepdims=True)
            probs.append((jnp.exp2(s - m).astype(BF16), m))
        for (_, keys, _, dst), (p, m) in zip(geo, probs):
            o = jnp.dot(p, vf_ref[keys, :], preferred_element_type=F32)
            l = o[:, HEAD_DIM:]
            tmp_ref[dst, :] = o[:, val] / l
            tmp2_ref[dst, :] = m + jnp.log2(l)
    for r1 in range(FAR_STEP):
        of_ref[pl.ds(r1, quarter, stride=FAR_STEP), :] = tmp_ref[r1 * quarter:(r1 + 1) * quarter, :]
        lf_ref[pl.ds(r1, quarter, stride=FAR_STEP), :] = tmp2_ref[r1 * quarter:(r1 + 1) * quarter, :]

    for g0 in range(0, len(near_geo), STAGE_TILES):
        geo = near_geo[g0:g0 + STAGE_TILES]
        scores = [_qk(qb_ref[rows, :], kb_ref[keys, :]) + bn_ref[:, (NEAR_KBLK - n_kt) * Q_BLK:]
                  for rows, keys, n_kt in geo]
        probs = []
        for (rows, _, _), s in zip(geo, scores):
            m_far = lf_ref[rows, 0:1]
            m = jnp.maximum(jnp.max(s, axis=-1, keepdims=True), m_far)
            probs.append((jnp.exp2(s - m).astype(BF16), jnp.exp2(m_far - m)))
        for (rows, keys, _), (p, w_far) in zip(geo, probs):
            o = jnp.dot(p, vb_ref[keys, :], preferred_element_type=F32)
            acc = w_far * of_ref[rows, :] + o[:, val]
            o_ref[rows, :] = acc / (w_far + o[:, HEAD_DIM:])


def _prompt_attn(z, bias_near, bias_far):
    b, s, _ = z.shape
    assert s % (FAR_D * Q_BLK) == 0

    def head_cols(cb):
        return pl.BlockSpec((None, s, HEAD_DIM), lambda bi, h: (bi, 0, cb * HEADS_PER_COL_BLK + h))

    return pl.pallas_call(
        functools.partial(_prompt_attn_kernel, seq=s),
        grid=(b, N_ATT_HEADS),
        in_specs=[
            head_cols(CB_Q), head_cols(CB_K), head_cols(CB_V),
            pl.BlockSpec((None, Q_BLK, NEAR_KBLK * Q_BLK), lambda bi, h: (h, 0, 0)),
            pl.BlockSpec((None, Q_BLK, 2 * Q_BLK), lambda bi, h: (h, 0, 0)),
        ],
        out_specs=pl.BlockSpec((None, s, HEAD_DIM), lambda bi, h: (bi, 0, h)),
        out_shape=jax.ShapeDtypeStruct((b, s, ATT_W), F32),
        scratch_shapes=[pltpu.VMEM((s, HEAD_DIM), BF16), pltpu.VMEM((s, HEAD_DIM), BF16),
                        pltpu.VMEM((s, 2 * HEAD_DIM), BF16)] * 2 + [pltpu.VMEM((s, HEAD_DIM), F32)] * 4,
        compiler_params=pltpu.CompilerParams(
            dimension_semantics=("parallel", "parallel"), vmem_limit_bytes=VMEM_LIMIT),
        name="prompt_attn",
    )(z, z, z, bias_near, bias_far)


def _sample_attn_kernel(q_ref, kn_ref, vn_ref, kc_ref, vc_ref, bias_ref, biasn_ref, sk_hbm, sv_hbm,
                        o_ref, ko_ref, vo_ref, qa_ref, kn3_ref, vn3_ref, m_ref, l_ref, acc_ref, *, t_len, rb):
    del sk_hbm, sv_hbm
    j = pl.program_id(1)
    last = pl.num_programs(1) - 1
    n_heads = N_ATT_HEADS

    @pl.when(j == 0)
    def _():
        for h in range(n_heads):
            cols = slice(h * HEAD_DIM, (h + 1) * HEAD_DIM)
            qa_ref[h * t_len:(h + 1) * t_len, :] = q_ref[:, cols] * SCALE
            kn3_ref[:, h, :] = kn_ref[:, cols]
            vn3_ref[:, h, :] = vn_ref[:, cols]
        m_ref[...] = jnp.full(m_ref.shape, NEG, F32)
        l_ref[...] = jnp.zeros(l_ref.shape, F32)
        acc_ref[...] = jnp.zeros(acc_ref.shape, F32)

    qa = qa_ref[...].astype(BF16)

    def accumulate(k, v, bias):
        s = _qk(qa, k.astype(BF16)) + bias
        m_old = m_ref[...]
        m_new = jnp.maximum(m_old, jnp.max(s, axis=-1, keepdims=True))
        alpha = jnp.exp(m_old - m_new)
        p = jnp.exp(s - m_new)
        l_ref[...] = alpha * l_ref[...] + jnp.sum(p, axis=-1, keepdims=True)
        acc_ref[...] = alpha * acc_ref[...] + jnp.dot(p.astype(BF16), v.astype(BF16),
                                                      preferred_element_type=F32)
        m_ref[...] = m_new

    accumulate(kc_ref[...].reshape(rb * n_heads, HEAD_DIM), vc_ref[...].reshape(rb * n_heads, HEAD_DIM),
               bias_ref[...])

    @pl.when(j == last)
    def _():
        ko_ref[...] = kn3_ref[...]
        vo_ref[...] = vn3_ref[...]
        accumulate(kn3_ref[...].reshape(t_len * n_heads, HEAD_DIM),
                   vn3_ref[...].reshape(t_len * n_heads, HEAD_DIM), biasn_ref[...])
        o = acc_ref[...] / l_ref[...]
        for h in range(n_heads):
            o_ref[:, h * HEAD_DIM:(h + 1) * HEAD_DIM] = o[h * t_len:(h + 1) * t_len, :]


def _sample_attn(z, cache_k, cache_v, bias, bias_new, stack_k, stack_v, *, li):
    b, t_len, _ = z.shape
    l_buf = cache_k.shape[2]
    rb = SAMPLE_RB
    assert t_len == SUBLANES and l_buf % rb == 0
    rows = N_ATT_HEADS * t_len

    def zspec(cb):
        return pl.BlockSpec((None, t_len, ATT_W), lambda bi, j: (bi, 0, cb * COL_BLK // ATT_W))

    blk_spec = pl.BlockSpec((None, None, rb, N_ATT_HEADS, HEAD_DIM), lambda bi, j: (li, bi, j, 0, 0))
    new_spec = pl.BlockSpec((None, None, t_len, N_ATT_HEADS, HEAD_DIM),
                            lambda bi, j: (li, bi, l_buf // t_len - 1, 0, 0))
    any_spec = pl.BlockSpec(memory_space=pl.ANY)
    in_specs = [
        zspec(CB_Q), zspec(CB_K), zspec(CB_V), blk_spec, blk_spec,
        pl.BlockSpec((rows, rb * N_ATT_HEADS), lambda bi, j: (0, j)),
        pl.BlockSpec((rows, rows), lambda bi, j: (0, 0)),
        any_spec, any_spec,
    ]
    args = [z, z, z, cache_k, cache_v, bias, bias_new, stack_k, stack_v]
    return pl.pallas_call(
        functools.partial(_sample_attn_kernel, t_len=t_len, rb=rb),
        grid=(b, l_buf // rb),
        in_specs=in_specs,
        out_specs=[pl.BlockSpec((None, t_len, ATT_W), lambda bi, j: (bi, 0, 0)), new_spec, new_spec],
        out_shape=[
            jax.ShapeDtypeStruct((b, t_len, ATT_W), F32),
            jax.ShapeDtypeStruct(cache_k.shape, F32),
            jax.ShapeDtypeStruct(cache_v.shape, F32),
        ],
        scratch_shapes=[
            pltpu.VMEM((rows, HEAD_DIM), F32),
            pltpu.VMEM((t_len, N_ATT_HEADS, HEAD_DIM), F32), pltpu.VMEM((t_len, N_ATT_HEADS, HEAD_DIM), F32),
            pltpu.VMEM((rows, 1), F32), pltpu.VMEM((rows, 1), F32), pltpu.VMEM((rows, HEAD_DIM), F32),
        ],
        input_output_aliases={7: 1, 8: 2},
        compiler_params=pltpu.CompilerParams(
            dimension_semantics=("arbitrary", "arbitrary"), vmem_limit_bytes=VMEM_LIMIT),
        name="sample_attn",
    )(*args)


def _silu(x):
    return x * jax.nn.sigmoid(x)


def _mix_kernel(x_ref, a_ref, ga_ref, uv_ref, ug_ref, gc_ref, qm_ref, gm_ref, cinit_ref, mk_ref, mv_ref,
                wdw_ref, bdw_ref, lng_ref, lnb_ref, wpw_ref, wout_ref, gpost_ref,
                xo_ref, cs_ref, uext_ref, mix_ref, *, nb, tile, conv_chunk):
    @pl.when(pl.program_id(1) == 0)
    def _():
        uext_ref[:, 0:HALO, :] = cinit_ref[...]

    first = HALO - (CONV_K - 1)
    chunks = []
    for b in range(nb):
        ub_ref = uext_ref.at[b]
        ub_ref[HALO:HALO + tile, :] = uv_ref[b] * jax.nn.sigmoid(ug_ref[b])
        for c0 in range(0, tile, conv_chunk):
            acc = jnp.broadcast_to(bdw_ref[...], (conv_chunk, CONV_CH))
            for phase in range(SUBLANES):
                rows = conv_chunk + (SUBLANES if phase else 0)
                part = None
                for k in range(CONV_K):
                    if (first + k) % SUBLANES == phase:
                        base = c0 + first + k - phase
                        term = ub_ref[base:base + rows, :] * wdw_ref[k:k + 1, :]
                        part = term if part is None else part + term
                if part is not None:
                    acc = acc + part[phase:phase + conv_chunk, :]
            chunks.append(acc)
        cs_ref[b] = ub_ref[tile + first:tile + HALO, :]
        tail = ub_ref[tile:tile + HALO, :]
        ub_ref[0:HALO, :] = tail
    c = jnp.concatenate(chunks, axis=0) if len(chunks) > 1 else chunks[0]

    mu = jnp.mean(c, axis=-1, keepdims=True)
    var = jnp.mean(jnp.square(c - mu), axis=-1, keepdims=True)
    c = (c - mu) * lax.rsqrt(var + EPS) * lng_ref[...] + lnb_ref[...]
    c = jnp.dot(_silu(c).astype(BF16), wpw_ref[...], preferred_element_type=F32)

    def mem_head(ref, h):
        if len(ref.shape) == 3:
            return ref[:, h, :].astype(BF16)
        return ref[:, h * HEAD_DIM:(h + 1) * HEAD_DIM].astype(BF16)

    for b in range(nb):
        rows = slice(b * tile, (b + 1) * tile)
        mix_ref[rows, 0:ATT_W] = (a_ref[b] * _silu(ga_ref[b])).astype(BF16)
        mix_ref[rows, ATT_W:ATT_W + CONV_CH] = (c[rows] * _silu(gc_ref[b])).astype(BF16)
        for h in range(N_X_HEADS):
            cols = slice(h * HEAD_DIM, (h + 1) * HEAD_DIM)
            q = (qm_ref[b, :, cols] * SCALE).astype(BF16)
            s = _qk(q, mem_head(mk_ref.at[b], h))
            p = jnp.exp(s - jnp.max(s, axis=-1, keepdims=True))
            o = jnp.dot(p.astype(BF16), mem_head(mv_ref.at[b], h), preferred_element_type=F32)
            o = o / jnp.sum(p, axis=-1, keepdims=True)
            off = ATT_W + CONV_CH + h * HEAD_DIM
            mix_ref[rows, off:off + HEAD_DIM] = (o * _silu(gm_ref[b, :, cols])).astype(BF16)

    y = jnp.dot(mix_ref[...], wout_ref[...], preferred_element_type=F32)
    y = y * lax.rsqrt(jnp.mean(y * y, axis=-1, keepdims=True) + EPS) * gpost_ref[...]
    for b in range(nb):
        xo_ref[b] = x_ref[b] + y[b * tile:(b + 1) * tile]


def _mix(x, z, a, cinit, mk_arr, mk_block, mv_arr, mv_block, w_dw, b_dw, ln_g, ln_b, w_pw2, w_out, g_post,
         *, li, nb, tile):
    b, s, _ = x.shape
    conv_chunk = min(tile, 32)
    assert b % nb == 0 and s % tile == 0

    def rows(width, cb=0):
        return pl.BlockSpec((nb, tile, width), lambda bi, t: (bi, t, cb * COL_BLK // width))

    def const(shape):
        return pl.BlockSpec(shape, lambda bi, t: (0,) * len(shape))

    def layer(shape):
        return pl.BlockSpec((None,) + shape, lambda bi, t: (li,) + (0,) * len(shape))

    def mem(block):
        shape, index = block
        return pl.BlockSpec(shape, lambda bi, t: index(bi))

    return pl.pallas_call(
        functools.partial(_mix_kernel, nb=nb, tile=tile, conv_chunk=conv_chunk),
        grid=(b // nb, s // tile),
        in_specs=[
            rows(D_MODEL), rows(ATT_W),
            rows(ATT_W, CB_GA), rows(COL_BLK, CB_UV), rows(COL_BLK, CB_UG), rows(COL_BLK, CB_GC),
            rows(COL_BLK, CB_QM), rows(COL_BLK, CB_GM),
            pl.BlockSpec((nb, HALO, CONV_CH), lambda bi, t: (bi, 0, 0)),
            mem(mk_block), mem(mv_block),
            const((CONV_K, CONV_CH)), const((1, CONV_CH)), const((1, CONV_CH)), const((1, CONV_CH)),
            layer((CONV_CH, CONV_CH)), layer((MIX_W, D_MODEL)), const((1, D_MODEL)),
        ],
        out_specs=[
            rows(D_MODEL),
            pl.BlockSpec((nb, CONV_K - 1, CONV_CH), lambda bi, t: (bi, 0, 0)),
        ],
        out_shape=[
            jax.ShapeDtypeStruct((b, s, D_MODEL), F32),
            jax.ShapeDtypeStruct((b, CONV_K - 1, CONV_CH), F32),
        ],
        scratch_shapes=[pltpu.VMEM((nb, HALO + tile, CONV_CH), F32), pltpu.VMEM((nb * tile, MIX_W), BF16)],
        compiler_params=pltpu.CompilerParams(
            dimension_semantics=("parallel", "arbitrary"), vmem_limit_bytes=VMEM_LIMIT),
        name="mix",
    )(x, a, z, z, z, z, z, z, cinit, mk_arr, mv_arr,
      w_dw, b_dw.reshape(1, -1), ln_g.reshape(1, -1), ln_b.reshape(1, -1), w_pw2, w_out,
      g_post.reshape(1, -1))


def kernel(x_prompt, x_sample, mem_prompt, cache_attn_k, cache_attn_v, state_conv, cache_mem_k, cache_mem_v,
           rel_bias, norm_pre_g, w_in, w_dw, b_dw, ln_conv_g, ln_conv_b, w_pw2, w_mem_kv, w_out, norm_post_g):
    bp, s_len, _ = x_prompt.shape
    bs, t_len, _ = x_sample.shape
    l_buf = cache_attn_k.shape[2]
    l_prompt = min(WIN, s_len)

    w_pw2_b = w_pw2.astype(BF16)
    w_mem_b, w_out_b = w_mem_kv.astype(BF16), w_out.astype(BF16)

    tab_near = _distance_table(rel_bias, np.arange(NEAR_REACH + 1), NEAR_PATTERNS)
    tab_far = _distance_table(rel_bias, FAR_D * np.arange(FAR_KEYS + 1), DIL_PATTERNS[-1:])
    bias_near = _toeplitz(tab_near * LOG2E, NEAR_REACH, Q_BLK, NEAR_KBLK * Q_BLK)
    bias_far = _toeplitz(tab_far * LOG2E, FAR_KEYS, Q_BLK, 2 * Q_BLK)
    tab_all = _distance_table(rel_bias, np.arange(WIN + 1), DIL_PATTERNS)
    bias_s = _head_matched_bias(tab_all, l_buf, t_len, l_buf)
    bias_s_new = _head_matched_bias(tab_all, 0, t_len, t_len)

    mkv = _mem_proj(mem_prompt.reshape(bp * N_MEM, D_MODEL), w_mem_b).reshape(DEPTH, bp, N_MEM, 2 * X_W)
    cinit_p = jnp.zeros((bp, HALO, CONV_CH), F32)
    pad_s = jnp.zeros((DEPTH, bs, HALO - (CONV_K - 1), CONV_CH), F32)
    cinit_s = jnp.concatenate([pad_s, state_conv], axis=2)
    xp, xs = x_prompt, x_sample
    stacks = None
    cvp, cvs = [], []
    for li in range(DEPTH):
        wargs = (w_dw[li], b_dw[li], ln_conv_g[li], ln_conv_b[li], w_pw2_b, w_out_b, norm_post_g[li])

        zs, w_in_li = _proj_cast(xs.reshape(bs * t_len, D_MODEL), norm_pre_g[li], w_in, li=li, bn=COL_BLK)
        zs = zs.reshape(bs, t_len, IN_W)

        zp, akp, avp, aks, avs = _proj_kv(
            xp.reshape(bp * s_len, D_MODEL), norm_pre_g[li], w_in_li, cache_attn_k, cache_attn_v, stacks,
            li=li, seq=s_len, l_keep=l_prompt, shift=t_len)
        zp = zp.reshape(bp, s_len, IN_W)
        a = _prompt_attn(zp, bias_near, bias_far)
        mk_block, mv_block = (((None, MIX_NB, N_MEM, X_W), lambda bi, li=li, half=half: (li, bi, 0, half))
                              for half in (0, 1))
        xp, cst = _mix(xp, zp, a, cinit_p, mkv, mk_block, mkv, mv_block, *wargs,
                       li=li, nb=MIX_NB, tile=MIX_TILE)
        cvp.append(cst)

        a, aks, avs = _sample_attn(zs, cache_attn_k, cache_attn_v, bias_s, bias_s_new, aks, avs, li=li)
        stacks = (akp, avp, aks, avs)
        mem_block = ((None, bs, N_MEM, N_X_HEADS, HEAD_DIM), lambda bi, li=li: (li, bi, 0, 0, 0))
        xs, cst = _mix(xs, zs, a, cinit_s[li], cache_mem_k, mem_block, cache_mem_v, mem_block, *wargs,
                       li=li, nb=bs, tile=t_len)
        cvs.append(cst)

    kv5 = (DEPTH, bp, l_prompt, N_ATT_HEADS, HEAD_DIM)
    mem5 = (DEPTH, bp, N_MEM, N_X_HEADS, HEAD_DIM)
    return (xp, xs, akp.reshape(kv5), avp.reshape(kv5), jnp.stack(cvp),
            mkv[..., :X_W].reshape(mem5), mkv[..., X_W:].reshape(mem5), aks, avs, jnp.stack(cvs))
```

```python
import functools

import numpy as np
import jax
import jax.numpy as jnp
from jax import lax
from jax.experimental import pallas as pl
from jax.experimental.pallas import tpu as pltpu

D_MODEL = 2048
DEPTH = 4
N_MEM = 256
HEAD_DIM = 128
ATT_W = D_MODEL // 2
N_ATT_HEADS = ATT_W // HEAD_DIM
DIL_PATTERNS = ((128, 1), (512, 4), (2048, 16))
WIN = max(w for w, _ in DIL_PATTERNS)
N_BUCKETS = 32
MAX_DIST = WIN
CONV_CH = D_MODEL // 4
CONV_K = 31
X_W = D_MODEL // 4
N_X_HEADS = X_W // HEAD_DIM
MIX_W = ATT_W + CONV_CH + X_W
IN_W = 4 * ATT_W + 3 * CONV_CH + 2 * X_W
EPS = 1e-6
NEG = -1e30
SCALE = HEAD_DIM ** -0.5
LOG2E = 1.4426950408889634

COL_BLK = 512
CB_Q, CB_K, CB_V, CB_GA = 0, 2, 4, 6
CB_UV, CB_UG, CB_GC, CB_QM, CB_GM = 8, 9, 10, 11, 12
HEADS_PER_COL_BLK = COL_BLK // HEAD_DIM

Q_BLK = 128
FAR_W, FAR_D = DIL_PATTERNS[-1]
NEAR_PATTERNS = DIL_PATTERNS[:-1]
NEAR_REACH = max(w for w, _ in NEAR_PATTERNS)
NEAR_KBLK = NEAR_REACH // Q_BLK + 1
FAR_KEYS = FAR_W // FAR_D
FAR_STEP = 4
assert FAR_KEYS == Q_BLK and NEAR_REACH % Q_BLK == 0 and FAR_STEP * FAR_STEP == FAR_D
MIX_NB, MIX_TILE = 1, 512
STAGE_TILES = 16
PROJ_BM = 1024
COPY_CHUNKS_PER_SEQ = 6
COPY_SLOTS = 3
SAMPLE_RB = 1024
HALO = 32
SUBLANES = 8
LANES = 128
VMEM_LIMIT = 56 * 1024 * 1024

BF16 = jnp.bfloat16
F32 = jnp.float32


def _t5_bucket(dist):
    dist = np.asarray(dist)
    max_exact = N_BUCKETS // 2
    large = max_exact + (np.log(np.maximum(dist, 1) / max_exact)
                         / np.log(MAX_DIST / max_exact) * (N_BUCKETS - max_exact)).astype(np.int32)
    large = np.minimum(large, N_BUCKETS - 1)
    return np.where(dist < max_exact, dist, large).astype(np.int32)


def _distance_table(rel_bias, dists, patterns):
    dists = np.asarray(dists)
    cnt = np.zeros(dists.shape, np.int64)
    for w, dil in patterns:
        cnt += ((dists % dil == 0) & (dists <= w)).astype(np.int64)
    logc = jnp.log(jnp.asarray(np.maximum(cnt, 1), F32))
    tab = rel_bias[_t5_bucket(dists)].T.astype(F32) + logc[None, :]
    tab = jnp.where(jnp.asarray(cnt > 0)[None, :], tab, NEG)
    return jnp.concatenate([tab, jnp.full((tab.shape[0], 1), NEG, F32)], axis=1)


def _toeplitz(tab, off, rows, cols):
    n_valid = tab.shape[1] - 1
    lp = rows + cols
    j = np.concatenate([np.arange(cols), np.full(lp - (rows + cols - 1), off + n_valid),
                        np.arange(-(rows - 1), 0)])
    d = off - j
    idx = np.where((d >= 0) & (d < n_valid), d, n_valid)
    v = jnp.take(tab, jnp.asarray(idx, jnp.int32), axis=1)
    flat = jnp.tile(v, (1, rows))[:, :rows * (lp - 1)]
    return flat.reshape(tab.shape[0], rows, lp - 1)[:, :, :cols]


def _head_matched_bias(tab, off, t_len, n_rows):
    n_h, n_valid = tab.shape[0], tab.shape[1] - 1
    n_cols = n_rows * n_h
    gaps = jnp.full((n_h, n_valid, n_h - 1), NEG, F32)
    s = jnp.concatenate([tab[:, :n_valid, None], gaps], axis=2).reshape(n_h, n_valid * n_h)
    k_max = n_h * (off + t_len - 1) + n_h - 1
    pad_l = max(0, n_cols - 1 - n_h * off)
    pad_r = max(0, k_max - (n_valid * n_h - 1))
    s = jnp.concatenate([jnp.full((n_h, pad_l), NEG, F32), s, jnp.full((n_h, pad_r), NEG, F32)], axis=1)
    n_s = s.shape[1]
    lane_pad = -n_s % LANES
    rev = jnp.concatenate([s[:, ::-1], jnp.full((n_h, lane_pad), NEG, F32)], axis=1)
    starts = tuple((h, n_s - 1 - (n_h * (off + t) + h) - pad_l) for h in range(n_h) for t in range(t_len))
    return pl.pallas_call(
        functools.partial(_slice_rows_kernel, starts=starts, n_cols=n_cols),
        out_shape=jax.ShapeDtypeStruct((len(starts), n_cols), F32),
        name="bias_rows",
    )(rev)


def _slice_rows_kernel(src_ref, out_ref, *, starts, n_cols):
    for r, (h, start) in enumerate(starts):
        lo = start // LANES * LANES
        hi = -(-(start + n_cols) // LANES) * LANES
        window = src_ref[h:h + 1, lo:hi]
        out_ref[r:r + 1, :] = window[:, start - lo:start - lo + n_cols]


def _norm_rows(x_ref, g_ref, h_ref, normalize):
    x = x_ref[...]
    if normalize:
        x = x * lax.rsqrt(jnp.mean(x * x, axis=-1, keepdims=True) + EPS) * g_ref[...]
    h_ref[...] = x.astype(BF16)


def _proj_kernel(x_ref, g_ref, w_ref, o_ref, h_ref, *, normalize):
    @pl.when(pl.program_id(1) == 0)
    def _():
        _norm_rows(x_ref, g_ref, h_ref, normalize)

    o_ref[...] = jnp.dot(h_ref[...], w_ref[...], preferred_element_type=F32)


def _proj_cast_kernel(x_ref, g_ref, w_ref, o_ref, wb_ref, h_ref):
    @pl.when(pl.program_id(0) == 0)
    def _():
        _norm_rows(x_ref, g_ref, h_ref, True)

    w = w_ref[...].astype(BF16)
    wb_ref[...] = w
    o_ref[...] = jnp.dot(h_ref[...], w, preferred_element_type=F32)


def _proj_cast(x, g, w, *, li, bn):
    m, d = x.shape
    n = w.shape[2]
    return pl.pallas_call(
        _proj_cast_kernel,
        grid=(n // bn,),
        in_specs=[
            pl.BlockSpec((m, d), lambda j: (0, 0)),
            pl.BlockSpec((1, d), lambda j: (0, 0)),
            pl.BlockSpec((None, d, bn), lambda j: (li, 0, j)),
        ],
        out_specs=[pl.BlockSpec((m, bn), lambda j: (0, j)), pl.BlockSpec((d, bn), lambda j: (0, j))],
        out_shape=[jax.ShapeDtypeStruct((m, n), F32), jax.ShapeDtypeStruct((d, n), BF16)],
        scratch_shapes=[pltpu.VMEM((m, d), BF16)],
        compiler_params=pltpu.CompilerParams(
            dimension_semantics=("arbitrary",), vmem_limit_bytes=VMEM_LIMIT),
        name="proj_cast",
    )(x, g.reshape(1, d), w)


def _mem_proj(x, w):
    m, d = x.shape
    depth, _, n = w.shape
    bn = COL_BLK
    return pl.pallas_call(
        functools.partial(_proj_kernel, normalize=False),
        grid=(depth, n // bn),
        in_specs=[
            pl.BlockSpec((m, d), lambda li, j: (0, 0)),
            pl.BlockSpec((1, d), lambda li, j: (0, 0)),
            pl.BlockSpec((None, d, bn), lambda li, j: (li, 0, j)),
        ],
        out_specs=pl.BlockSpec((None, m, bn), lambda li, j: (li, 0, j)),
        out_shape=jax.ShapeDtypeStruct((depth, m, n), F32),
        scratch_shapes=[pltpu.VMEM((m, d), BF16)],
        compiler_params=pltpu.CompilerParams(
            dimension_semantics=("arbitrary", "arbitrary"), vmem_limit_bytes=VMEM_LIMIT),
        name="mem_proj",
    )(x, jnp.ones((1, d), F32), w)


def _proj_kv_kernel(*refs, li, tiles_per_seq, tail_tiles, aliased, shift, chunk):
    x_ref, g_ref, w_ref, ck_hbm, cv_hbm = refs[:5]
    o_ref, ko_ref, vo_ref, sk_hbm, sv_hbm, h_ref, ring_ref, sem_in, sem_out = refs[9:] if aliased else refs[5:]
    i, j = pl.program_id(0), pl.program_id(1)
    step = i * pl.num_programs(1) + j
    n_seq, l_buf = ck_hbm.shape[1], ck_hbm.shape[2]
    per_seq = (l_buf - shift) // chunk
    n_chunks = 2 * n_seq * per_seq

    def chunk_dma(c, which, read):
        src, dst = ((ck_hbm, sk_hbm), (cv_hbm, sv_hbm))[which]
        b, r = (c // per_seq) % n_seq, c % per_seq
        slot = c % COPY_SLOTS
        if read:
            return pltpu.make_async_copy(src.at[li, b, pl.ds(shift + r * chunk, chunk)], ring_ref.at[slot],
                                         sem_in.at[slot])
        return pltpu.make_async_copy(ring_ref.at[slot], dst.at[li, b, pl.ds(r * chunk, chunk)],
                                     sem_out.at[slot])

    def on_chunk(c, live, action):
        for which in (0, 1):
            @pl.when(live & (c // (n_seq * per_seq) == which))
            def _(which=which):
                action(c, which)

    on_chunk(step - 2, (step >= 2) & (step - 2 < n_chunks), lambda c, w: chunk_dma(c, w, False).wait())

    def turn_around(c, w):
        chunk_dma(c, w, True).wait()
        chunk_dma(c, w, False).start(priority=1)

    on_chunk(step - 1, (step >= 1) & (step - 1 < n_chunks), turn_around)
    on_chunk(step, step < n_chunks, lambda c, w: chunk_dma(c, w, True).start(priority=1))

    @pl.when(j == 0)
    def _():
        _norm_rows(x_ref, g_ref, h_ref, True)

    o_ref[...] = jnp.dot(h_ref[...], w_ref[...], preferred_element_type=F32)

    in_tail = i % tiles_per_seq >= tiles_per_seq - tail_tiles
    for cb0, dst in ((CB_K, ko_ref), (CB_V, vo_ref)):
        for cb in range(cb0, cb0 + ATT_W // COL_BLK):
            @pl.when(in_tail & (j == cb))
            def _(cb=cb, cb0=cb0, dst=dst):
                for hh in range(HEADS_PER_COL_BLK):
                    head = (cb - cb0) * HEADS_PER_COL_BLK + hh
                    dst[pl.ds(head, o_ref.shape[0], stride=N_ATT_HEADS), :] = (
                        o_ref[:, hh * HEAD_DIM:(hh + 1) * HEAD_DIM])


def _proj_kv(x, g, w, cache_k, cache_v, prev, *, li, seq, l_keep, shift):
    m, d = x.shape
    n = w.shape[1]
    bsz = m // seq
    bm, bn = PROJ_BM, COL_BLK
    tiles_per_seq, tail_tiles = seq // bm, l_keep // bm
    assert seq % bm == 0 and l_keep % bm == 0
    aliased = prev is not None
    kv_shape = jax.ShapeDtypeStruct((DEPTH, bsz, l_keep * N_ATT_HEADS, HEAD_DIM), F32)
    cache_shape = jax.ShapeDtypeStruct(cache_k.shape, F32)
    any_spec = pl.BlockSpec(memory_space=pl.ANY)
    n_steps = (m // bm) * (n // bn)
    n_seq, l_buf = cache_k.shape[1], cache_k.shape[2]
    chunk = (l_buf - shift) // COPY_CHUNKS_PER_SEQ
    assert (l_buf - shift) % COPY_CHUNKS_PER_SEQ == 0 and 2 * n_seq * COPY_CHUNKS_PER_SEQ + 2 <= n_steps

    def kv_map(i, j):
        return (li, i // tiles_per_seq, jnp.maximum(i % tiles_per_seq - (tiles_per_seq - tail_tiles), 0), 0)

    kv_spec = pl.BlockSpec((None, None, bm * N_ATT_HEADS, HEAD_DIM), kv_map)
    in_specs = [
        pl.BlockSpec((bm, d), lambda i, j: (i, 0)),
        pl.BlockSpec((1, d), lambda i, j: (0, 0)),
        pl.BlockSpec((d, bn), lambda i, j: (0, j)),
        any_spec, any_spec,
    ]
    args = [x, g.reshape(1, d), w, cache_k, cache_v]
    if aliased:
        in_specs += [any_spec] * 4
        args += list(prev)
    return pl.pallas_call(
        functools.partial(_proj_kv_kernel, li=li, tiles_per_seq=tiles_per_seq, tail_tiles=tail_tiles,
                          aliased=aliased, shift=shift, chunk=chunk),
        grid=(m // bm, n // bn),
        in_specs=in_specs,
        out_specs=[pl.BlockSpec((bm, bn), lambda i, j: (i, j)), kv_spec, kv_spec, any_spec, any_spec],
        out_shape=[jax.ShapeDtypeStruct((m, n), F32), kv_shape, kv_shape, cache_shape, cache_shape],
        scratch_shapes=[pltpu.VMEM((bm, d), BF16),
                        pltpu.VMEM((COPY_SLOTS, chunk) + cache_k.shape[3:], F32),
                        pltpu.SemaphoreType.DMA((COPY_SLOTS,)), pltpu.SemaphoreType.DMA((COPY_SLOTS,))],
        input_output_aliases={5: 1, 6: 2, 7: 3, 8: 4} if aliased else {},
        compiler_params=pltpu.CompilerParams(
            dimension_semantics=("arbitrary", "arbitrary"), vmem_limit_bytes=VMEM_LIMIT),
        name="proj_kv",
    )(*args)


def _qk(q, k):
    return lax.dot_general(q, k, (((1,), (1,)), ((), ())), preferred_element_type=F32)


def _prompt_attn_kernel(q_ref, k_ref, v_ref, bn_ref, bf_ref, o_ref,
                        qb_ref, kb_ref, vb_ref, qf_ref, kf_ref, vf_ref, of_ref, lf_ref, tmp_ref, tmp2_ref,
                        *, seq):
    sub = seq // FAR_D
    chunk = 2 * Q_BLK
    val = slice(0, HEAD_DIM)

    for r0 in range(0, seq, chunk):
        rows = pl.ds(r0, chunk)
        qb_ref[rows, :] = (q_ref[rows, :] * (SCALE * LOG2E)).astype(BF16)
        kb_ref[rows, :] = k_ref[rows, :].astype(BF16)
        vb_ref[rows, val] = v_ref[rows, :].astype(BF16)
        vb_ref[rows, HEAD_DIM:] = jnp.ones((chunk, HEAD_DIM), BF16)
        vf_ref[rows, HEAD_DIM:] = jnp.ones((chunk, HEAD_DIM), BF16)

    quarter = seq // FAR_STEP
    for src_ref, dst_ref, scale in ((q_ref, qf_ref, SCALE * LOG2E), (k_ref, kf_ref, None),
                                    (v_ref, vf_ref, None)):
        for r1 in range(FAR_STEP):
            tmp_ref[r1 * quarter:(r1 + 1) * quarter, :] = src_ref[pl.ds(r1, quarter, stride=FAR_STEP), :]
        for r1 in range(FAR_STEP):
            for r2 in range(FAR_STEP):
                r = FAR_STEP * r2 + r1
                x = tmp_ref[pl.ds(r1 * quarter + r2, sub, stride=FAR_STEP), :]
                if scale is not None:
                    x = x * scale
                dst_ref[r * sub:(r + 1) * sub, val] = x.astype(BF16)

    far_geo = []
    for r in range(FAR_D):
        for a_blk in range(sub // Q_BLK):
            n_kt = min(a_blk, 1) + 1
            rows = pl.ds(r * sub + a_blk * Q_BLK, Q_BLK)
            keys = pl.ds(r * sub + (a_blk + 1 - n_kt) * Q_BLK, n_kt * Q_BLK)
            r1, r2 = r % FAR_STEP, r // FAR_STEP
            dst = pl.ds(r1 * quarter + FAR_STEP * a_blk * Q_BLK + r2, Q_BLK, stride=FAR_STEP)
            far_geo.append((rows, keys, n_kt, dst))
    near_geo = []
    for i in range(seq // Q_BLK):
        n_kt = min(i + 1, NEAR_KBLK)
        near_geo.append((pl.ds(i * Q_BLK, Q_BLK), pl.ds((i + 1 - n_kt) * Q_BLK, n_kt * Q_BLK), n_kt))

    for g0 in range(0, len(far_geo), STAGE_TILES):
        geo = far_geo[g0:g0 + STAGE_TILES]
        scores = [_qk(qf_ref[rows, :], kf_ref[keys, :]) + bf_ref[:, (2 - n_kt) * Q_BLK:]
                  for rows, keys, n_kt, _ in geo]
        probs = []
        for s in scores:
            m = jnp.max(s, axis=-1, keepdims=True)
            probs.append((jnp.exp2(s - m).astype(BF16), m))
        for (_, keys, _, dst), (p, m) in zip(geo, probs):
            o = jnp.dot(p, vf_ref[keys, :], preferred_element_type=F32)
            l = o[:, HEAD_DIM:]
            tmp_ref[dst, :] = o[:, val] / l
            tmp2_ref[dst, :] = m + jnp.log2(l)
    for r1 in range(FAR_STEP):
        of_ref[pl.ds(r1, quarter, stride=FAR_STEP), :] = tmp_ref[r1 * quarter:(r1 + 1) * quarter, :]
        lf_ref[pl.ds(r1, quarter, stride=FAR_STEP), :] = tmp2_ref[r1 * quarter:(r1 + 1) * quarter, :]

    for g0 in range(0, len(near_geo), STAGE_TILES):
        geo = near_geo[g0:g0 + STAGE_TILES]
        scores = [_qk(qb_ref[rows, :], kb_ref[keys, :]) + bn_ref[:, (NEAR_KBLK - n_kt) * Q_BLK:]
                  for rows, keys, n_kt in geo]
        probs = []
        for (rows, _, _), s in zip(geo, scores):
            m_far = lf_ref[rows, 0:1]
            m = jnp.maximum(jnp.max(s, axis=-1, keepdims=True), m_far)
            probs.append((jnp.exp2(s - m).astype(BF16), jnp.exp2(m_far - m)))
        for (rows, keys, _), (p, w_far) in zip(geo, probs):
            o = jnp.dot(p, vb_ref[keys, :], preferred_element_type=F32)
            acc = w_far * of_ref[rows, :] + o[:, val]
            o_ref[rows, :] = acc / (w_far + o[:, HEAD_DIM:])


def _prompt_attn(z, bias_near, bias_far):
    b, s, _ = z.shape
    assert s % (FAR_D * Q_BLK) == 0

    def head_cols(cb):
        return pl.BlockSpec((None, s, HEAD_DIM), lambda bi, h: (bi, 0, cb * HEADS_PER_COL_BLK + h))

    return pl.pallas_call(
        functools.partial(_prompt_attn_kernel, seq=s),
        grid=(b, N_ATT_HEADS),
        in_specs=[
            head_cols(CB_Q), head_cols(CB_K), head_cols(CB_V),
            pl.BlockSpec((None, Q_BLK, NEAR_KBLK * Q_BLK), lambda bi, h: (h, 0, 0)),
            pl.BlockSpec((None, Q_BLK, 2 * Q_BLK), lambda bi, h: (h, 0, 0)),
        ],
        out_specs=pl.BlockSpec((None, s, HEAD_DIM), lambda bi, h: (bi, 0, h)),
        out_shape=jax.ShapeDtypeStruct((b, s, ATT_W), F32),
        scratch_shapes=[pltpu.VMEM((s, HEAD_DIM), BF16), pltpu.VMEM((s, HEAD_DIM), BF16),
                        pltpu.VMEM((s, 2 * HEAD_DIM), BF16)] * 2 + [pltpu.VMEM((s, HEAD_DIM), F32)] * 4,
        compiler_params=pltpu.CompilerParams(
            dimension_semantics=("parallel", "parallel"), vmem_limit_bytes=VMEM_LIMIT),
        name="prompt_attn",
    )(z, z, z, bias_near, bias_far)


def _sample_attn_kernel(q_ref, kn_ref, vn_ref, kc_ref, vc_ref, bias_ref, biasn_ref, sk_hbm, sv_hbm,
                        o_ref, ko_ref, vo_ref, qa_ref, kn3_ref, vn3_ref, m_ref, l_ref, acc_ref, *, t_len, rb):
    del sk_hbm, sv_hbm
    j = pl.program_id(1)
    last = pl.num_programs(1) - 1
    n_heads = N_ATT_HEADS

    @pl.when(j == 0)
    def _():
        for h in range(n_heads):
            cols = slice(h * HEAD_DIM, (h + 1) * HEAD_DIM)
            qa_ref[h * t_len:(h + 1) * t_len, :] = q_ref[:, cols] * SCALE
            kn3_ref[:, h, :] = kn_ref[:, cols]
            vn3_ref[:, h, :] = vn_ref[:, cols]
        m_ref[...] = jnp.full(m_ref.shape, NEG, F32)
        l_ref[...] = jnp.zeros(l_ref.shape, F32)
        acc_ref[...] = jnp.zeros(acc_ref.shape, F32)

    qa = qa_ref[...].astype(BF16)

    def accumulate(k, v, bias):
        s = _qk(qa, k.astype(BF16)) + bias
        m_old = m_ref[...]
        m_new = jnp.maximum(m_old, jnp.max(s, axis=-1, keepdims=True))
        alpha = jnp.exp(m_old - m_new)
        p = jnp.exp(s - m_new)
        l_ref[...] = alpha * l_ref[...] + jnp.sum(p, axis=-1, keepdims=True)
        acc_ref[...] = alpha * acc_ref[...] + jnp.dot(p.astype(BF16), v.astype(BF16),
                                                      preferred_element_type=F32)
        m_ref[...] = m_new

    accumulate(kc_ref[...].reshape(rb * n_heads, HEAD_DIM), vc_ref[...].reshape(rb * n_heads, HEAD_DIM),
               bias_ref[...])

    @pl.when(j == last)
    def _():
        ko_ref[...] = kn3_ref[...]
        vo_ref[...] = vn3_ref[...]
        accumulate(kn3_ref[...].reshape(t_len * n_heads, HEAD_DIM),
                   vn3_ref[...].reshape(t_len * n_heads, HEAD_DIM), biasn_ref[...])
        o = acc_ref[...] / l_ref[...]
        for h in range(n_heads):
            o_ref[:, h * HEAD_DIM:(h + 1) * HEAD_DIM] = o[h * t_len:(h + 1) * t_len, :]


def _sample_attn(z, cache_k, cache_v, bias, bias_new, stack_k, stack_v, *, li):
    b, t_len, _ = z.shape
    l_buf = cache_k.shape[2]
    rb = SAMPLE_RB
    assert t_len == SUBLANES and l_buf % rb == 0
    rows = N_ATT_HEADS * t_len

    def zspec(cb):
        return pl.BlockSpec((None, t_len, ATT_W), lambda bi, j: (bi, 0, cb * COL_BLK // ATT_W))

    blk_spec = pl.BlockSpec((None, None, rb, N_ATT_HEADS, HEAD_DIM), lambda bi, j: (li, bi, j, 0, 0))
    new_spec = pl.BlockSpec((None, None, t_len, N_ATT_HEADS, HEAD_DIM),
                            lambda bi, j: (li, bi, l_buf // t_len - 1, 0, 0))
    any_spec = pl.BlockSpec(memory_space=pl.ANY)
    in_specs = [
        zspec(CB_Q), zspec(CB_K), zspec(CB_V), blk_spec, blk_spec,
        pl.BlockSpec((rows, rb * N_ATT_HEADS), lambda bi, j: (0, j)),
        pl.BlockSpec((rows, rows), lambda bi, j: (0, 0)),
        any_spec, any_spec,
    ]
    args = [z, z, z, cache_k, cache_v, bias, bias_new, stack_k, stack_v]
    return pl.pallas_call(
        functools.partial(_sample_attn_kernel, t_len=t_len, rb=rb),
        grid=(b, l_buf // rb),
        in_specs=in_specs,
        out_specs=[pl.BlockSpec((None, t_len, ATT_W), lambda bi, j: (bi, 0, 0)), new_spec, new_spec],
        out_shape=[
            jax.ShapeDtypeStruct((b, t_len, ATT_W), F32),
            jax.ShapeDtypeStruct(cache_k.shape, F32),
            jax.ShapeDtypeStruct(cache_v.shape, F32),
        ],
        scratch_shapes=[
            pltpu.VMEM((rows, HEAD_DIM), F32),
            pltpu.VMEM((t_len, N_ATT_HEADS, HEAD_DIM), F32), pltpu.VMEM((t_len, N_ATT_HEADS, HEAD_DIM), F32),
            pltpu.VMEM((rows, 1), F32), pltpu.VMEM((rows, 1), F32), pltpu.VMEM((rows, HEAD_DIM), F32),
        ],
        input_output_aliases={7: 1, 8: 2},
        compiler_params=pltpu.CompilerParams(
            dimension_semantics=("arbitrary", "arbitrary"), vmem_limit_bytes=VMEM_LIMIT),
        name="sample_attn",
    )(*args)


def _silu(x):
    return x * jax.nn.sigmoid(x)


def _mix_kernel(x_ref, a_ref, ga_ref, uv_ref, ug_ref, gc_ref, qm_ref, gm_ref, cinit_ref, mk_ref, mv_ref,
                wdw_ref, bdw_ref, lng_ref, lnb_ref, wpw_ref, wout_ref, gpost_ref,
                xo_ref, cs_ref, uext_ref, mix_ref, *, nb, tile, conv_chunk):
    @pl.when(pl.program_id(1) == 0)
    def _():
        uext_ref[:, 0:HALO, :] = cinit_ref[...]

    first = HALO - (CONV_K - 1)
    chunks = []
    for b in range(nb):
        ub_ref = uext_ref.at[b]
        ub_ref[HALO:HALO + tile, :] = uv_ref[b] * jax.nn.sigmoid(ug_ref[b])
        for c0 in range(0, tile, conv_chunk):
            acc = jnp.broadcast_to(bdw_ref[...], (conv_chunk, CONV_CH))
            for phase in range(SUBLANES):
                rows = conv_chunk + (SUBLANES if phase else 0)
                part = None
                for k in range(CONV_K):
                    if (first + k) % SUBLANES == phase:
                        base = c0 + first + k - phase
                        term = ub_ref[base:base + rows, :] * wdw_ref[k:k + 1, :]
                        part = term if part is None else part + term
                if part is not None:
                    acc = acc + part[phase:phase + conv_chunk, :]
            chunks.append(acc)
        cs_ref[b] = ub_ref[tile + first:tile + HALO, :]
        tail = ub_ref[tile:tile + HALO, :]
        ub_ref[0:HALO, :] = tail
    c = jnp.concatenate(chunks, axis=0) if len(chunks) > 1 else chunks[0]

    mu = jnp.mean(c, axis=-1, keepdims=True)
    var = jnp.mean(jnp.square(c - mu), axis=-1, keepdims=True)
    c = (c - mu) * lax.rsqrt(var + EPS) * lng_ref[...] + lnb_ref[...]
    c = jnp.dot(_silu(c).astype(BF16), wpw_ref[...], preferred_element_type=F32)

    def mem_head(ref, h):
        if len(ref.shape) == 3:
            return ref[:, h, :].astype(BF16)
        return ref[:, h * HEAD_DIM:(h + 1) * HEAD_DIM].astype(BF16)

    for b in range(nb):
        rows = slice(b * tile, (b + 1) * tile)
        mix_ref[rows, 0:ATT_W] = (a_ref[b] * _silu(ga_ref[b])).astype(BF16)
        mix_ref[rows, ATT_W:ATT_W + CONV_CH] = (c[rows] * _silu(gc_ref[b])).astype(BF16)
        for h in range(N_X_HEADS):
            cols = slice(h * HEAD_DIM, (h + 1) * HEAD_DIM)
            q = (qm_ref[b, :, cols] * SCALE).astype(BF16)
            s = _qk(q, mem_head(mk_ref.at[b], h))
            p = jnp.exp(s - jnp.max(s, axis=-1, keepdims=True))
            o = jnp.dot(p.astype(BF16), mem_head(mv_ref.at[b], h), preferred_element_type=F32)
            o = o / jnp.sum(p, axis=-1, keepdims=True)
            off = ATT_W + CONV_CH + h * HEAD_DIM
            mix_ref[rows, off:off + HEAD_DIM] = (o * _silu(gm_ref[b, :, cols])).astype(BF16)

    y = jnp.dot(mix_ref[...], wout_ref[...], preferred_element_type=F32)
    y = y * lax.rsqrt(jnp.mean(y * y, axis=-1, keepdims=True) + EPS) * gpost_ref[...]
    for b in range(nb):
        xo_ref[b] = x_ref[b] + y[b * tile:(b + 1) * tile]


def _mix(x, z, a, cinit, mk_arr, mk_block, mv_arr, mv_block, w_dw, b_dw, ln_g, ln_b, w_pw2, w_out, g_post,
         *, li, nb, tile):
    b, s, _ = x.shape
    conv_chunk = min(tile, 32)
    assert b % nb == 0 and s % tile == 0

    def rows(width, cb=0):
        return pl.BlockSpec((nb, tile, width), lambda bi, t: (bi, t, cb * COL_BLK // width))

    def const(shape):
        return pl.BlockSpec(shape, lambda bi, t: (0,) * len(shape))

    def layer(shape):
        return pl.BlockSpec((None,) + shape, lambda bi, t: (li,) + (0,) * len(shape))

    def mem(block):
        shape, index = block
        return pl.BlockSpec(shape, lambda bi, t: index(bi))

    return pl.pallas_call(
        functools.partial(_mix_kernel, nb=nb, tile=tile, conv_chunk=conv_chunk),
        grid=(b // nb, s // tile),
        in_specs=[
            rows(D_MODEL), rows(ATT_W),
            rows(ATT_W, CB_GA), rows(COL_BLK, CB_UV), rows(COL_BLK, CB_UG), rows(COL_BLK, CB_GC),
            rows(COL_BLK, CB_QM), rows(COL_BLK, CB_GM),
            pl.BlockSpec((nb, HALO, CONV_CH), lambda bi, t: (bi, 0, 0)),
            mem(mk_block), mem(mv_block),
            const((CONV_K, CONV_CH)), const((1, CONV_CH)), const((1, CONV_CH)), const((1, CONV_CH)),
            layer((CONV_CH, CONV_CH)), layer((MIX_W, D_MODEL)), const((1, D_MODEL)),
        ],
        out_specs=[
            rows(D_MODEL),
            pl.BlockSpec((nb, CONV_K - 1, CONV_CH), lambda bi, t: (bi, 0, 0)),
        ],
        out_shape=[
            jax.ShapeDtypeStruct((b, s, D_MODEL), F32),
            jax.ShapeDtypeStruct((b, CONV_K - 1, CONV_CH), F32),
        ],
        scratch_shapes=[pltpu.VMEM((nb, HALO + tile, CONV_CH), F32), pltpu.VMEM((nb * tile, MIX_W), BF16)],
        compiler_params=pltpu.CompilerParams(
            dimension_semantics=("parallel", "arbitrary"), vmem_limit_bytes=VMEM_LIMIT),
        name="mix",
    )(x, a, z, z, z, z, z, z, cinit, mk_arr, mv_arr,
      w_dw, b_dw.reshape(1, -1), ln_g.reshape(1, -1), ln_b.reshape(1, -1), w_pw2, w_out,
      g_post.reshape(1, -1))


def kernel(x_prompt, x_sample, mem_prompt, cache_attn_k, cache_attn_v, state_conv, cache_mem_k, cache_mem_v,
           rel_bias, norm_pre_g, w_in, w_dw, b_dw, ln_conv_g, ln_conv_b, w_pw2, w_mem_kv, w_out, norm_post_g):
    bp, s_len, _ = x_prompt.shape
    bs, t_len, _ = x_sample.shape
    l_buf = cache_attn_k.shape[2]
    l_prompt = min(WIN, s_len)

    w_pw2_b = w_pw2.astype(BF16)
    w_mem_b, w_out_b = w_mem_kv.astype(BF16), w_out.astype(BF16)

    tab_near = _distance_table(rel_bias, np.arange(NEAR_REACH + 1), NEAR_PATTERNS)
    tab_far = _distance_table(rel_bias, FAR_D * np.arange(FAR_KEYS + 1), DIL_PATTERNS[-1:])
    bias_near = _toeplitz(tab_near * LOG2E, NEAR_REACH, Q_BLK, NEAR_KBLK * Q_BLK)
    bias_far = _toeplitz(tab_far * LOG2E, FAR_KEYS, Q_BLK, 2 * Q_BLK)
    tab_all = _distance_table(rel_bias, np.arange(WIN + 1), DIL_PATTERNS)
    bias_s = _head_matched_bias(tab_all, l_buf, t_len, l_buf)
    bias_s_new = _head_matched_bias(tab_all, 0, t_len, t_len)

    mkv = _mem_proj(mem_prompt.reshape(bp * N_MEM, D_MODEL), w_mem_b).reshape(DEPTH, bp, N_MEM, 2 * X_W)
    cinit_p = jnp.zeros((bp, HALO, CONV_CH), F32)
    pad_s = jnp.zeros((DEPTH, bs, HALO - (CONV_K - 1), CONV_CH), F32)
    cinit_s = jnp.concatenate([pad_s, state_conv], axis=2)
    xp, xs = x_prompt, x_sample
    stacks = None
    cvp, cvs = [], []
    for li in range(DEPTH):
        wargs = (w_dw[li], b_dw[li], ln_conv_g[li], ln_conv_b[li], w_pw2_b, w_out_b, norm_post_g[li])

        zs, w_in_li = _proj_cast(xs.reshape(bs * t_len, D_MODEL), norm_pre_g[li], w_in, li=li, bn=COL_BLK)
        zs = zs.reshape(bs, t_len, IN_W)

        zp, akp, avp, aks, avs = _proj_kv(
            xp.reshape(bp * s_len, D_MODEL), norm_pre_g[li], w_in_li, cache_attn_k, cache_attn_v, stacks,
            li=li, seq=s_len, l_keep=l_prompt, shift=t_len)
        zp = zp.reshape(bp, s_len, IN_W)
        a = _prompt_attn(zp, bias_near, bias_far)
        mk_block, mv_block = (((None, MIX_NB, N_MEM, X_W), lambda bi, li=li, half=half: (li, bi, 0, half))
                              for half in (0, 1))
        xp, cst = _mix(xp, zp, a, cinit_p, mkv, mk_block, mkv, mv_block, *wargs,
                       li=li, nb=MIX_NB, tile=MIX_TILE)
        cvp.append(cst)

        a, aks, avs = _sample_attn(zs, cache_attn_k, cache_attn_v, bias_s, bias_s_new, aks, avs, li=li)
        stacks = (akp, avp, aks, avs)
        mem_block = ((None, bs, N_MEM, N_X_HEADS, HEAD_DIM), lambda bi, li=li: (li, bi, 0, 0, 0))
        xs, cst = _mix(xs, zs, a, cinit_s[li], cache_mem_k, mem_block, cache_mem_v, mem_block, *wargs,
                       li=li, nb=bs, tile=t_len)
        cvs.append(cst)

    kv5 = (DEPTH, bp, l_prompt, N_ATT_HEADS, HEAD_DIM)
    mem5 = (DEPTH, bp, N_MEM, N_X_HEADS, HEAD_DIM)
    return (xp, xs, akp.reshape(kv5), avp.reshape(kv5), jnp.stack(cvp),
            mkv[..., :X_W].reshape(mem5), mkv[..., X_W:].reshape(mem5), aks, avs, jnp.stack(cvs))
```
